```python
import jax, jax.numpy as jnp
from jax import lax
import numpy as np

D_MODEL = 1024
BATCH = 1
SEQ = 16384
DEPTH = 1
DEC_BATCH = 32
DEC_SEQ = 8
PAST_LEN = 16384
PAGE_SIZE = 128

HEAD_DIM = 64
ATTN_WIDTH = D_MODEL // 2
N_HEADS = ATTN_WIDTH // HEAD_DIM
POOL_WINDOWS = (2, 4, 8, 16)
N_POOL_GROUPS = len(POOL_WINDOWS)
POOL_WIDTH = D_MODEL - ATTN_WIDTH
POOL_CH = POOL_WIDTH // N_POOL_GROUPS
POOL_STATE = max(POOL_WINDOWS) - 1
MIX_WIDTH = ATTN_WIDTH + POOL_WIDTH
IN_WIDTH = 3 * ATTN_WIDTH + POOL_WIDTH
D_FF = 4 * D_MODEL
MOBA_BLOCK = 256
MOBA_TOP_K = 3
Q_CHUNK = 128
ROPE_THETA = 500000.0
ROT_DIM = HEAD_DIM // 4
ATTN_SCALE = HEAD_DIM ** -0.5
RMS_EPS = 1e-6

kernel_name = 'moba_pool_hybrid_step'


def rmsnorm(x, g):
    xf = x.astype(jnp.float32)
    y = xf * lax.rsqrt(jnp.mean(xf * xf, axis=-1, keepdims=True) + RMS_EPS)
    return (y * g.astype(jnp.float32)).astype(x.dtype)


def partial_rope(x, pos):
    half = ROT_DIM // 2
    inv = jnp.power(ROPE_THETA, -jnp.arange(half, dtype=jnp.float32) * (2.0 / ROT_DIM))
    ang = pos.astype(jnp.float32)[:, None] * inv[None, :]
    cos = jnp.cos(ang)[:, None, :]
    sin = jnp.sin(ang)[:, None, :]
    xf = x.astype(jnp.float32)
    x1 = xf[..., :half]
    x2 = xf[..., half:ROT_DIM]
    out = jnp.concatenate([x1 * cos - x2 * sin, x2 * cos + x1 * sin, xf[..., ROT_DIM:]], axis=-1)
    return out.astype(x.dtype)


def mixer_inputs(x, pos, g, w_in):
    b, l, _ = x.shape
    z = rmsnorm(x, g) @ w_in
    q = z[..., :ATTN_WIDTH].reshape(b, l, N_HEADS, HEAD_DIM)
    k = z[..., ATTN_WIDTH:2 * ATTN_WIDTH].reshape(b, l, N_HEADS, HEAD_DIM)
    v = z[..., 2 * ATTN_WIDTH:3 * ATTN_WIDTH].reshape(b, l, N_HEADS, HEAD_DIM)
    u = z[..., 3 * ATTN_WIDTH:]
    return partial_rope(q, pos), partial_rope(k, pos), v, u


def to_blocks(x):
    t = x.shape[0]
    nb = -(-t // MOBA_BLOCK)
    xp = jnp.pad(x, ((0, nb * MOBA_BLOCK - t), (0, 0), (0, 0)))
    return xp.reshape(nb, MOBA_BLOCK, N_HEADS, HEAD_DIM)


def moba_core(q, q_pos, kb, vb, kmean):
    nq = q.shape[0]
    nb = kb.shape[0]
    qf = q.astype(jnp.float32)
    b_q = q_pos // MOBA_BLOCK
    b_own = b_q[0]
    gate = jnp.einsum('qhd,nhd->qhn', qf, kmean)
    past_ok = (jnp.arange(nb)[None, :] < b_q[:, None])[:, None, :]
    gate = jnp.where(past_ok, gate, -jnp.inf)
    n_sel = min(MOBA_TOP_K, nb)
    _, sel = lax.top_k(gate, n_sel)
    sel_ok = sel < b_q[:, None, None]
    kh = jnp.transpose(kb, (2, 0, 1, 3))
    vh = jnp.transpose(vb, (2, 0, 1, 3))
    h_idx = jnp.arange(N_HEADS)[None, :, None]
    k_sel = kh[h_idx, sel].astype(jnp.float32)
    v_sel = vh[h_idx, sel].astype(jnp.float32)
    k_own = lax.dynamic_index_in_dim(kb, b_own, axis=0, keepdims=False).astype(jnp.float32)
    v_own = lax.dynamic_index_in_dim(vb, b_own, axis=0, keepdims=False).astype(jnp.float32)
    own_pos = b_own * MOBA_BLOCK + jnp.arange(MOBA_BLOCK)
    s_sel = jnp.einsum('qhd,qhsbd->qhsb', qf, k_sel) * ATTN_SCALE
    s_sel = jnp.where(sel_ok[..., None], s_sel, -jnp.inf).reshape(nq, N_HEADS, n_sel * MOBA_BLOCK)
    s_own = jnp.einsum('qhd,bhd->qhb', qf, k_own) * ATTN_SCALE
    s_own = jnp.where((own_pos[None, :] <= q_pos[:, None])[:, None, :], s_own, -jnp.inf)
    p = jax.nn.softmax(jnp.concatenate([s_sel, s_own], axis=-1), axis=-1)
    p_sel = p[..., :n_sel * MOBA_BLOCK].reshape(nq, N_HEADS, n_sel, MOBA_BLOCK)
    p_own = p[..., n_sel * MOBA_BLOCK:]
    out = jnp.einsum('qhsb,qhsbd->qhd', p_sel, v_sel) + jnp.einsum('qhb,bhd->qhd', p_own, v_own)
    return out.astype(q.dtype)


def prompt_attn_seq(q, k, v):
    kb = to_blocks(k)
    vb = to_blocks(v)
    kmean = jnp.mean(kb.astype(jnp.float32), axis=1)
    l = q.shape[0]
    nc = l // Q_CHUNK
    qc = q.reshape(nc, Q_CHUNK, N_HEADS, HEAD_DIM)
    pc = jnp.arange(l, dtype=jnp.int32).reshape(nc, Q_CHUNK)
    out = lax.map(lambda a: moba_core(a[0], a[1], kb, vb, kmean), (qc, pc))
    return out.reshape(l, N_HEADS, HEAD_DIM)


def sample_attn_seq(q, k_new, v_new, pages, pool_k, pool_v):
    past_k = pool_k[pages].reshape(-1, N_HEADS, HEAD_DIM).astype(k_new.dtype)
    past_v = pool_v[pages].reshape(-1, N_HEADS, HEAD_DIM).astype(v_new.dtype)
    past_len = past_k.shape[0]
    kb = to_blocks(jnp.concatenate([past_k, k_new], axis=0))
    vb = to_blocks(jnp.concatenate([past_v, v_new], axis=0))
    kmean = jnp.mean(kb.astype(jnp.float32), axis=1)
    pos = past_len + jnp.arange(q.shape[0], dtype=jnp.int32)
    return moba_core(q, pos, kb, vb, kmean)


def pool_mix(u_ext, n_new, w_pool, pool_scale):
    b, l_ext, _ = u_ext.shape
    uf = u_ext.astype(jnp.float32)
    idx = jnp.arange(l_ext)
    outs = []
    for g, w in enumerate(POOL_WINDOWS):
        ug = uf[..., g * POOL_CH:(g + 1) * POOL_CH]
        cs = jnp.cumsum(ug, axis=1)
        cs_lag = jnp.pad(cs, ((0, 0), (w, 0), (0, 0)))[:, :l_ext]
        cnt = jnp.minimum(idx + 1, w).astype(jnp.float32)[None, :, None]
        outs.append((cs - cs_lag) / cnt - ug)
    d = jnp.stack(outs, axis=2)[:, l_ext - n_new:]
    y = jnp.einsum('blgc,gce->blge', d, w_pool.astype(jnp.float32))
    y = y * pool_scale.astype(jnp.float32).reshape(N_POOL_GROUPS, POOL_CH)
    return y.reshape(b, n_new, POOL_WIDTH).astype(u_ext.dtype)


def finish(x, attn, pool, w_o, g2, w_up, w_down):
    b, l, _ = x.shape
    mix = jnp.concatenate([attn.reshape(b, l, ATTN_WIDTH), pool], axis=-1)
    x = x + mix @ w_o
    hid = jax.nn.relu(rmsnorm(x, g2) @ w_up)
    return x + (hid * hid) @ w_down


def setup_inputs(seed: int = 0) -> dict:
    key = jax.random.key(seed)
    ks = jax.random.split(key, 16)
    n_pages = PAST_LEN // PAGE_SIZE
    n_used = DEC_BATCH * n_pages
    n_phys = n_used + n_used // 4

    def nrm(k, shape, scale=1.0):
        return jax.random.normal(k, shape, jnp.float32) * scale

    x_prompt = nrm(ks[0], (BATCH, SEQ, D_MODEL))
    x_sample = nrm(ks[1], (DEC_BATCH, DEC_SEQ, D_MODEL))
    cache_k = nrm(ks[2], (DEPTH, n_phys, PAGE_SIZE, N_HEADS, HEAD_DIM))
    cache_v = nrm(ks[3], (DEPTH, n_phys, PAGE_SIZE, N_HEADS, HEAD_DIM))
    state_pool = nrm(ks[4], (DEPTH, DEC_BATCH, POOL_STATE, POOL_WIDTH))
    page_table = jax.random.permutation(ks[5], n_phys)[:n_used].reshape(DEC_BATCH, n_pages).astype(jnp.int32)
    norm1_g = 1.0 + 0.02 * nrm(ks[6], (DEPTH, D_MODEL))
    w_in = nrm(ks[7], (DEPTH, D_MODEL, IN_WIDTH), D_MODEL ** -0.5)
    w_pool = nrm(ks[8], (DEPTH, N_POOL_GROUPS, POOL_CH, POOL_CH), POOL_CH ** -0.5)
    pool_scale = 1.0 + 0.1 * nrm(ks[9], (DEPTH, POOL_WIDTH))
    w_o = nrm(ks[10], (DEPTH, MIX_WIDTH, D_MODEL), MIX_WIDTH ** -0.5)
    norm2_g = 1.0 + 0.02 * nrm(ks[11], (DEPTH, D_MODEL))
    w_up = nrm(ks[12], (DEPTH, D_MODEL, D_FF), D_MODEL ** -0.5)
    w_down = nrm(ks[13], (DEPTH, D_FF, D_MODEL), D_FF ** -0.5)
    final_g = 1.0 + 0.02 * nrm(ks[14], (D_MODEL,))
    return {'x_prompt': x_prompt, 'x_sample': x_sample, 'cache_k': cache_k, 'cache_v': cache_v,
            'state_pool': state_pool, 'page_table': page_table, 'norm1_g': norm1_g, 'w_in': w_in,
            'w_pool': w_pool, 'pool_scale': pool_scale, 'w_o': w_o, 'norm2_g': norm2_g,
            'w_up': w_up, 'w_down': w_down, 'final_g': final_g}


def reference(x_prompt, x_sample, cache_k, cache_v, state_pool, page_table, norm1_g, w_in,
              w_pool, pool_scale, w_o, norm2_g, w_up, w_down, final_g):
    seq = x_prompt.shape[1]
    dec_seq = x_sample.shape[1]
    past_len = page_table.shape[1] * cache_k.shape[2]
    pos_p = jnp.arange(seq, dtype=jnp.int32)
    pos_s = past_len + jnp.arange(dec_seq, dtype=jnp.int32)
    xp = x_prompt
    xs = x_sample
    k_prompt_l, v_prompt_l, pool_prompt_l = [], [], []
    k_sample_l, v_sample_l, pool_sample_l = [], [], []
    for l in range(DEPTH):
        qp, kp, vp, up = mixer_inputs(xp, pos_p, norm1_g[l], w_in[l])
        attn_p = lax.map(lambda a: prompt_attn_seq(a[0], a[1], a[2]), (qp, kp, vp))
        pool_p = pool_mix(up, seq, w_pool[l], pool_scale[l])
        xp = finish(xp, attn_p, pool_p, w_o[l], norm2_g[l], w_up[l], w_down[l])
        k_prompt_l.append(kp)
        v_prompt_l.append(vp)
        pool_prompt_l.append(up[:, seq - POOL_STATE:])

        qs, ks_, vs, us = mixer_inputs(xs, pos_s, norm1_g[l], w_in[l])
        ck = cache_k[l]
        cv = cache_v[l]
        attn_s = lax.map(lambda a: sample_attn_seq(a[0], a[1], a[2], a[3], ck, cv), (qs, ks_, vs, page_table))
        u_ext = jnp.concatenate([state_pool[l].astype(us.dtype), us], axis=1)
        pool_s = pool_mix(u_ext, dec_seq, w_pool[l], pool_scale[l])
        xs = finish(xs, attn_s, pool_s, w_o[l], norm2_g[l], w_up[l], w_down[l])
        k_sample_l.append(ks_)
        v_sample_l.append(vs)
        pool_sample_l.append(u_ext[:, u_ext.shape[1] - POOL_STATE:])
    y_prompt = rmsnorm(xp, final_g)
    y_sample = rmsnorm(xs, final_g)
    return (y_prompt, y_sample, jnp.stack(k_prompt_l), jnp.stack(v_prompt_l), jnp.stack(pool_prompt_l),
            jnp.stack(k_sample_l), jnp.stack(v_sample_l), jnp.stack(pool_sample_l))
```

```python
import functools

import jax
import jax.numpy as jnp
from jax import lax
from jax.experimental import pallas as pl
from jax.experimental.pallas import tpu as pltpu

F32 = jnp.float32
BF16 = jnp.bfloat16

D_MODEL = 1024
HEAD_DIM = 64
N_HEADS = 8
ATTN_WIDTH = N_HEADS * HEAD_DIM
POOL_WINDOWS = (2, 4, 8, 16)
POOL_CH = 128
POOL_WIDTH = POOL_CH * len(POOL_WINDOWS)
POOL_STATE = max(POOL_WINDOWS) - 1
HALO = 16
D_FF = 4 * D_MODEL
MOBA_BLOCK = 256
MOBA_TOP_K = 3
ROPE_THETA = 500000.0
ROT_DIM = HEAD_DIM // 4
ROT_HALF = ROT_DIM // 2
ATTN_SCALE = HEAD_DIM ** -0.5
RMS_EPS = 1e-6
NEG = -1e30
LANES = 128
VMEM_LIMIT = 56 * 1024 * 1024

_NT = (((1,), (1,)), ((), ()))


def _dot(a, b):
    return jnp.dot(a, b, preferred_element_type=F32)


def _dot_nt(a, b):
    return lax.dot_general(a, b, _NT, preferred_element_type=F32)


def _split_bf16(x):
    hi = x.astype(BF16)
    lo = (x - hi.astype(F32)).astype(BF16)
    return hi, lo


def _dot3(a, b):
    a_hi, a_lo = _split_bf16(a)
    b_hi, b_lo = _split_bf16(b)
    return _dot(a_hi, b_hi) + _dot(a_hi, b_lo) + _dot(a_lo, b_hi)


def _rmsnorm(x, g):
    ms = jnp.mean(x * x, axis=-1, keepdims=True)
    return x * lax.rsqrt(ms + RMS_EPS) * g


def _top_k_mask(gate, valid, index, axis):
    limit = gate.shape[axis]
    g = jnp.where(valid, gate, -jnp.inf)
    sel = jnp.zeros(gate.shape, jnp.bool_)
    for _ in range(MOBA_TOP_K):
        mx = jnp.max(g, axis=axis, keepdims=True)
        first = jnp.min(jnp.where(g == mx, index, limit), axis=axis, keepdims=True)
        pick = jnp.logical_and(index == first, mx > -jnp.inf)
        sel = jnp.logical_or(sel, pick)
        g = jnp.where(pick, -jnp.inf, g)
    return sel


def _proj_kernel(x_ref, g_ref, wqkvT_ref, wku_ref, cosT_ref, sinT_ref, c_ref, s1_ref, s2_ref,
                 wpool_ref, pscale_ref,
                 qT_ref, kT_ref, vT_ref, vblk_ref, kaug_ref, kmean_ref, pool_ref, utail_ref,
                 ext_ref):
    i = pl.program_id(0)
    tl = x_ref.shape[0]
    hb = _rmsnorm(x_ref[...], g_ref[...]).astype(BF16)

    zT = _dot_nt(wqkvT_ref[...], hb)
    cosT = cosT_ref[...]
    sinT = sinT_ref[...]
    for hd in range(2 * N_HEADS):
        base = hd * HEAD_DIM
        x1 = zT[base:base + ROT_HALF]
        x2 = zT[base + ROT_HALF:base + ROT_DIM]
        rest = zT[base + ROT_DIM:base + HEAD_DIM]
        r1 = x1 * cosT - x2 * sinT
        r2 = x2 * cosT + x1 * sinT
        if hd < N_HEADS:
            qT_ref[0, base:base + ROT_HALF, :] = r1 * ATTN_SCALE
            qT_ref[0, base + ROT_HALF:base + ROT_DIM, :] = r2 * ATTN_SCALE
            qT_ref[0, base + ROT_DIM:base + HEAD_DIM, :] = rest * ATTN_SCALE
        else:
            kb = base - ATTN_WIDTH
            kT_ref[kb:kb + ROT_HALF, :] = r1
            kT_ref[kb + ROT_HALF:kb + ROT_DIM, :] = r2
            kT_ref[kb + ROT_DIM:kb + HEAD_DIM, :] = rest
    vT = zT[2 * ATTN_WIDTH:3 * ATTN_WIDTH]
    vT_ref[...] = vT
    vblk_ref[0] = vT.astype(BF16)

    z2 = _dot(hb, wku_ref[...])
    lane = lax.broadcasted_iota(jnp.int32, (tl, LANES), 1)
    block_tag = jnp.where(lane - HEAD_DIM == i, 1.0, 0.0)
    low = lane < HEAD_DIM
    for g in range(ATTN_WIDTH // LANES):
        xg = z2[:, LANES * g:LANES * (g + 1)]
        kr = (xg * c_ref[...] + pltpu.roll(xg, ROT_HALF, 1) * s2_ref[...]
              + pltpu.roll(xg, LANES - ROT_HALF, 1) * s1_ref[...])
        kaug_ref[2 * g] = jnp.where(low, kr, block_tag).astype(BF16)
        kaug_ref[2 * g + 1] = jnp.where(low, pltpu.roll(kr, HEAD_DIM, 1), block_tag).astype(BF16)
        ksum = jnp.sum(kr, axis=0, keepdims=True) * (1.0 / tl)
        kmean_ref[0, :, LANES * g:LANES * (g + 1)] = jnp.broadcast_to(ksum, (8, LANES))

    u = z2[:, ATTN_WIDTH:]

    @pl.when(i == 0)
    def _():
        ext_ref[0:HALO, :] = jnp.zeros((HALO, POOL_WIDTH), F32)

    ext_ref[HALO:HALO + tl, :] = u
    row = lax.broadcasted_iota(jnp.int32, (tl, 1), 0) + i * tl
    for g, w in enumerate(POOL_WINDOWS):
        lanes = slice(POOL_CH * g, POOL_CH * (g + 1))
        ug = ext_ref[HALO:HALO + tl, lanes]
        acc = ug
        for j in range(1, w):
            acc = acc + ext_ref[HALO - j:HALO - j + tl, lanes]
        cnt = jnp.minimum(row + 1, w).astype(F32)
        d = acc / cnt - ug
        y = _dot(d.astype(BF16), wpool_ref[g]) * pscale_ref[:, lanes]
        pool_ref[:, lanes] = y.astype(BF16)
    tail = ext_ref[tl:tl + HALO, :]
    ext_ref[0:HALO, :] = tail
    utail_ref[...] = tail


def _prompt_proj(x, g1, wqkvT, wku, cosT, sinT, c_tab, s1_tab, s2_tab, wpool, pscale):
    seq = x.shape[0]
    tl = MOBA_BLOCK
    nb = seq // tl
    const2 = lambda i: (0, 0)
    return pl.pallas_call(
        _proj_kernel,
        grid=(nb,),
        in_specs=[
            pl.BlockSpec((tl, D_MODEL), lambda i: (i, 0)),
            pl.BlockSpec((1, D_MODEL), const2),
            pl.BlockSpec((3 * ATTN_WIDTH, D_MODEL), const2),
            pl.BlockSpec((D_MODEL, ATTN_WIDTH + POOL_WIDTH), const2),
            pl.BlockSpec((ROT_HALF, tl), lambda i: (0, i)),
            pl.BlockSpec((ROT_HALF, tl), lambda i: (0, i)),
            pl.BlockSpec((tl, LANES), lambda i: (i, 0)),
            pl.BlockSpec((tl, LANES), lambda i: (i, 0)),
            pl.BlockSpec((tl, LANES), lambda i: (i, 0)),
            pl.BlockSpec((len(POOL_WINDOWS), POOL_CH, POOL_CH), lambda i: (0, 0, 0)),
            pl.BlockSpec((1, POOL_WIDTH), const2),
        ],
        out_specs=[
            pl.BlockSpec((1, ATTN_WIDTH, tl), lambda i: (i, 0, 0)),
            pl.BlockSpec((ATTN_WIDTH, tl), lambda i: (0, i)),
            pl.BlockSpec((ATTN_WIDTH, tl), lambda i: (0, i)),
            pl.BlockSpec((1, ATTN_WIDTH, tl), lambda i: (i, 0, 0)),
            pl.BlockSpec((N_HEADS, tl, LANES), lambda i: (0, i, 0)),
            pl.BlockSpec((1, 8, ATTN_WIDTH), lambda i: (i, 0, 0)),
            pl.BlockSpec((tl, POOL_WIDTH), lambda i: (i, 0)),
            pl.BlockSpec((HALO, POOL_WIDTH), const2),
        ],
        out_shape=[
            jax.ShapeDtypeStruct((nb, ATTN_WIDTH, tl), F32),
            jax.ShapeDtypeStruct((ATTN_WIDTH, seq), F32),
            jax.ShapeDtypeStruct((ATTN_WIDTH, seq), F32),
            jax.ShapeDtypeStruct((nb, ATTN_WIDTH, tl), BF16),
            jax.ShapeDtypeStruct((N_HEADS, seq, LANES), BF16),
            jax.ShapeDtypeStruct((nb, 8, ATTN_WIDTH), F32),
            jax.ShapeDtypeStruct((seq, POOL_WIDTH), BF16),
            jax.ShapeDtypeStruct((HALO, POOL_WIDTH), F32),
        ],
        scratch_shapes=[pltpu.VMEM((HALO + tl, POOL_WIDTH), F32)],
        compiler_params=pltpu.CompilerParams(dimension_semantics=("arbitrary",),
                                             vmem_limit_bytes=VMEM_LIMIT),
        name="prompt_proj",
    )(x, g1, wqkvT, wku, cosT, sinT, c_tab, s1_tab, s2_tab, wpool, pscale)


def _attn_kernel(q_ref, k_ref, v_ref, km_ref, o_ref):
    t = pl.program_id(1)
    nblk = km_ref.shape[1]
    tq = q_ref.shape[2]
    blk_idx = lax.broadcasted_iota(jnp.int32, (nblk, tq), 0)
    key_idx = lax.broadcasted_iota(jnp.int32, (MOBA_BLOCK, tq), 0)
    qry_idx = lax.broadcasted_iota(jnp.int32, (MOBA_BLOCK, tq), 1)
    ones = jnp.ones((8, MOBA_BLOCK), BF16)

    q_aug = []
    for hh in range(2):
        qT = q_ref[0, HEAD_DIM * hh:HEAD_DIM * (hh + 1), :]
        gate = _dot3(km_ref[hh], qT)
        sel = _top_k_mask(gate, blk_idx < t, blk_idx, 0)
        bias = jnp.where(jnp.logical_or(sel, blk_idx == t), 0.0, NEG)
        q_aug.append(jnp.concatenate([qT.astype(BF16), bias.astype(BF16)], axis=0))

    def scores(hh, n):
        start = pl.multiple_of(n * MOBA_BLOCK, MOBA_BLOCK)
        return _dot(k_ref[hh, pl.ds(start, MOBA_BLOCK), :], q_aug[hh])

    def values(hh, n):
        return v_ref[n, HEAD_DIM * hh:HEAD_DIM * (hh + 1), :]

    state = []
    for hh in range(2):
        s = jnp.where(key_idx <= qry_idx, scores(hh, t), NEG)
        m = jnp.max(s, axis=0, keepdims=True)
        p = jnp.exp(s - m).astype(BF16)
        state += [m, _dot(ones, p)[0:1], _dot(values(hh, t), p)]

    def body(n, carry):
        out = []
        for hh in range(2):
            m, l, acc = carry[3 * hh:3 * hh + 3]
            s = scores(hh, n)
            m_new = jnp.maximum(m, jnp.max(s, axis=0, keepdims=True))
            alpha = jnp.exp(m - m_new)
            p = jnp.exp(s - m_new).astype(BF16)
            out += [m_new, l * alpha + _dot(ones, p)[0:1], acc * alpha + _dot(values(hh, n), p)]
        return tuple(out)

    state = lax.fori_loop(0, t, body, tuple(state))
    oT = jnp.concatenate([state[2] / state[1], state[5] / state[4]], axis=0)
    o_ref[...] = oT.T.astype(BF16)


def _prompt_attn(qT_blk, k_aug, v_blk, kmean_hm):
    nb, _, tq = qT_blk.shape
    seq = k_aug.shape[1]
    return pl.pallas_call(
        _attn_kernel,
        grid=(N_HEADS // 2, nb),
        in_specs=[
            pl.BlockSpec((1, 2 * HEAD_DIM, tq), lambda p, t: (t, p, 0)),
            pl.BlockSpec((2, seq, LANES), lambda p, t: (p, 0, 0)),
            pl.BlockSpec((nb, 2 * HEAD_DIM, MOBA_BLOCK), lambda p, t: (0, p, 0)),
            pl.BlockSpec((2, nb, HEAD_DIM), lambda p, t: (p, 0, 0)),
        ],
        out_specs=pl.BlockSpec((tq, 2 * HEAD_DIM), lambda p, t: (t, p)),
        out_shape=jax.ShapeDtypeStruct((seq, ATTN_WIDTH), BF16),
        compiler_params=pltpu.CompilerParams(dimension_semantics=("arbitrary", "arbitrary"),
                                             vmem_limit_bytes=VMEM_LIMIT),
        name="prompt_attn",
    )(qT_blk, k_aug, v_blk, kmean_hm)


def _mlp_kernel(x_ref, attn_ref, pool_ref, wo_ref, g2_ref, wup_ref, wdown_ref, gf_ref, y_ref,
                x1_ref, hn_ref, acc_ref):
    j = pl.program_id(1)

    @pl.when(j == 0)
    def _():
        x1 = (x_ref[...] + _dot(attn_ref[...], wo_ref[0:ATTN_WIDTH, :])
              + _dot(pool_ref[...], wo_ref[ATTN_WIDTH:, :]))
        x1_ref[...] = x1
        hn_ref[...] = _rmsnorm(x1, g2_ref[...]).astype(BF16)
        acc_ref[...] = jnp.zeros(acc_ref.shape, F32)

    hid = jnp.maximum(_dot(hn_ref[...], wup_ref[...]), 0.0)
    acc_ref[...] += _dot((hid * hid).astype(BF16), wdown_ref[...])

    @pl.when(j == pl.num_programs(1) - 1)
    def _():
        y_ref[...] = _rmsnorm(x1_ref[...] + acc_ref[...], gf_ref[...])


def _out_mlp(x, attn, pool, wo, g2, wup, wdown, gf, tl, tf):
    rows = x.shape[0]
    const2 = lambda i, j: (0, 0)
    return pl.pallas_call(
        _mlp_kernel,
        grid=(rows // tl, D_FF // tf),
        in_specs=[
            pl.BlockSpec((tl, D_MODEL), lambda i, j: (i, 0)),
            pl.BlockSpec((tl, ATTN_WIDTH), lambda i, j: (i, 0)),
            pl.BlockSpec((tl, POOL_WIDTH), lambda i, j: (i, 0)),
            pl.BlockSpec((D_MODEL, D_MODEL), const2),
            pl.BlockSpec((1, D_MODEL), const2),
            pl.BlockSpec((D_MODEL, tf), lambda i, j: (0, j)),
            pl.BlockSpec((tf, D_MODEL), lambda i, j: (j, 0)),
            pl.BlockSpec((1, D_MODEL), const2),
        ],
        out_specs=pl.BlockSpec((tl, D_MODEL), lambda i, j: (i, 0)),
        out_shape=jax.ShapeDtypeStruct((rows, D_MODEL), F32),
        scratch_shapes=[pltpu.VMEM((tl, D_MODEL), F32), pltpu.VMEM((tl, D_MODEL), BF16),
                        pltpu.VMEM((tl, D_MODEL), F32)],
        compiler_params=pltpu.CompilerParams(dimension_semantics=("arbitrary", "arbitrary"),
                                             vmem_limit_bytes=VMEM_LIMIT),
        name="out_mlp",
    )(x, attn, pool, wo, g2, wup, wdown, gf)


def _sproj_kernel(x_ref, g_ref, win_ref, c_ref, s1_ref, s2_ref, state_ref, wpool_ref, pscale_ref,
                  q_ref, k_ref, v_ref, u_ref, pool_ref, ext_ref):
    rows = x_ref.shape[0]
    nseq, dec = state_ref.shape[0], rows // state_ref.shape[0]
    hb = _rmsnorm(x_ref[...], g_ref[...]).astype(BF16)
    z = _dot(hb, win_ref[...])
    for g in range(2 * ATTN_WIDTH // LANES):
        xg = z[:, LANES * g:LANES * (g + 1)]
        r = (xg * c_ref[...] + pltpu.roll(xg, ROT_HALF, 1) * s2_ref[...]
             + pltpu.roll(xg, LANES - ROT_HALF, 1) * s1_ref[...])
        if LANES * g < ATTN_WIDTH:
            q_ref[:, LANES * g:LANES * (g + 1)] = r * ATTN_SCALE
        else:
            k_ref[:, LANES * g - ATTN_WIDTH:LANES * (g + 1) - ATTN_WIDTH] = r
    v_ref[...] = z[:, 2 * ATTN_WIDTH:3 * ATTN_WIDTH]
    u = z[:, 3 * ATTN_WIDTH:]
    u_ref[...] = u

    ext_ref[:, 0:HALO, :] = state_ref[...]
    ext_ref[:, HALO:HALO + dec, :] = u.reshape(nseq, dec, POOL_WIDTH)
    for g, w in enumerate(POOL_WINDOWS):
        lanes = slice(POOL_CH * g, POOL_CH * (g + 1))
        ug = ext_ref[:, HALO:HALO + dec, lanes]
        acc = ug
        for j in range(1, w):
            acc = acc + ext_ref[:, HALO - j:HALO - j + dec, lanes]
        d = (acc / float(w) - ug).reshape(rows, POOL_CH)
        y = _dot(d.astype(BF16), wpool_ref[g]) * pscale_ref[:, lanes]
        pool_ref[:, lanes] = y.astype(BF16)


def _sample_proj(x, g1, win, c_tab, s1_tab, s2_tab, state_pad, wpool, pscale):
    rows = x.shape[0]
    nseq = state_pad.shape[0]
    shapes = [jax.ShapeDtypeStruct((rows, ATTN_WIDTH), F32)] * 3 + [
        jax.ShapeDtypeStruct((rows, POOL_WIDTH), F32), jax.ShapeDtypeStruct((rows, POOL_WIDTH), BF16)]
    return pl.pallas_call(
        _sproj_kernel,
        out_shape=shapes,
        scratch_shapes=[pltpu.VMEM((nseq, HALO + rows // nseq, POOL_WIDTH), F32)],
        compiler_params=pltpu.CompilerParams(vmem_limit_bytes=VMEM_LIMIT),
        name="sample_proj",
    )(x, g1, win, c_tab, s1_tab, s2_tab, state_pad, wpool, pscale)


N_BUF = 8


def _dec_kernel(pt_ref, q_ref, kn_ref, vn_ref, ck_hbm, cv_hbm, o_ref,
                kbuf, vbuf, ksem, vsem, s_scr, ksum_scr):
    b = pl.program_id(0)
    n_pages = pt_ref.shape[1]
    page = kbuf.shape[-1]
    ppb = MOBA_BLOCK // page
    nblk = n_pages // ppb
    dec = q_ref.shape[1]
    rows = N_HEADS * dec

    def k_copy(j, slot):
        return pltpu.make_async_copy(ck_hbm.at[pt_ref[b, j]], kbuf.at[slot], ksem.at[slot])

    def v_copy(j, slot):
        return pltpu.make_async_copy(cv_hbm.at[pt_ref[b, j]], vbuf.at[slot], vsem.at[slot])

    for s in range(N_BUF):
        k_copy(s, s).start()
    for s in range(N_BUF):
        v_copy(s, s).start()

    r_idx = lax.broadcasted_iota(jnp.int32, (rows, ATTN_WIDTH), 0)
    c_idx = lax.broadcasted_iota(jnp.int32, (rows, ATTN_WIDTH), 1)
    head_mask = (c_idx // HEAD_DIM) == (r_idx // dec)
    q_bd = jnp.where(head_mask, jnp.concatenate([q_ref[0]] * N_HEADS, axis=0), 0.0)
    q_bd16 = q_bd.astype(BF16)
    lane = lax.broadcasted_iota(jnp.int32, (rows, LANES), 1)

    ksum_scr[...] = jnp.zeros(ksum_scr.shape, F32)
    lane_k = lax.broadcasted_iota(jnp.int32, (ATTN_WIDTH, LANES), 1)

    def k_block(n, carry):
        bsum = jnp.zeros((ATTN_WIDTH, page), F32)
        for e in range(ppb):
            j = n * ppb + e
            slot = j % N_BUF
            k_copy(j, slot).wait()
            kt = kbuf[slot].reshape(ATTN_WIDTH, page)
            s_scr[j] = _dot(q_bd16, kt.astype(BF16))
            bsum = bsum + kt

            @pl.when(j + N_BUF < n_pages)
            def _():
                k_copy(j + N_BUF, slot).start()
        col = jnp.sum(bsum, axis=1, keepdims=True)
        ksum_scr[...] = jnp.where(lane_k == n, col, ksum_scr[...])
        return carry

    lax.fori_loop(0, nblk, k_block, 0)

    gate = _dot3(q_bd, ksum_scr[...] * (1.0 / MOBA_BLOCK))
    sel = _top_k_mask(gate, lane < nblk, lane, 1)
    bias = jnp.where(sel, 0.0, NEG)

    def block_bias(n):
        return jnp.sum(jnp.where(lane == n, bias, 0.0), axis=1, keepdims=True)

    q_pos = lax.broadcasted_iota(jnp.int32, (rows, LANES), 0) % dec
    s_own = jnp.where(lane <= q_pos, _dot_nt(q_bd16, kn_ref[0]), NEG)
    m0 = jnp.max(s_own, axis=1, keepdims=True)

    def max_block(n, m):
        col = block_bias(n)
        for e in range(ppb):
            m = jnp.maximum(m, jnp.max(s_scr[n * ppb + e] + col, axis=1, keepdims=True))
        return m

    m = lax.fori_loop(0, nblk, max_block, m0)
    p_own = jnp.exp(s_own - m)
    l0 = jnp.sum(p_own, axis=1, keepdims=True)

    def exp_block(n, l):
        col = block_bias(n) - m
        for e in range(ppb):
            p = jnp.exp(s_scr[n * ppb + e] + col)
            s_scr[n * ppb + e] = p
            l = l + jnp.sum(p, axis=1, keepdims=True)
        return l

    l = lax.fori_loop(0, nblk, exp_block, l0)

    def v_page(j, acc):
        slot = j % N_BUF
        v_copy(j, slot).wait()
        vt = vbuf[slot].reshape(ATTN_WIDTH, page)
        acc = acc + _dot_nt(s_scr[j].astype(BF16), vt.astype(BF16))

        @pl.when(j + N_BUF < n_pages)
        def _():
            v_copy(j + N_BUF, slot).start()
        return acc

    acc = lax.fori_loop(0, n_pages, v_page, _dot(p_own.astype(BF16), vn_ref[0]))
    o = jnp.where(head_mask, acc / l, 0.0)
    out = o[0:dec]
    for h in range(1, N_HEADS):
        out = out + o[h * dec:(h + 1) * dec]
    o_ref[0] = out


def _sample_attn(page_table, q_s, kn_pad, vn_pad, cache_kT, cache_vT):
    nseq, n_pages = page_table.shape
    dec = q_s.shape[1]
    page = cache_kT.shape[-1]
    grid_spec = pltpu.PrefetchScalarGridSpec(
        num_scalar_prefetch=1,
        grid=(nseq,),
        in_specs=[
            pl.BlockSpec((1, dec, ATTN_WIDTH), lambda b, pt: (b, 0, 0)),
            pl.BlockSpec((1, LANES, ATTN_WIDTH), lambda b, pt: (b, 0, 0)),
            pl.BlockSpec((1, LANES, ATTN_WIDTH), lambda b, pt: (b, 0, 0)),
            pl.BlockSpec(memory_space=pl.ANY),
            pl.BlockSpec(memory_space=pl.ANY),
        ],
        out_specs=pl.BlockSpec((1, dec, ATTN_WIDTH), lambda b, pt: (b, 0, 0)),
        scratch_shapes=[
            pltpu.VMEM((N_BUF, N_HEADS, HEAD_DIM, page), F32),
            pltpu.VMEM((N_BUF, N_HEADS, HEAD_DIM, page), F32),
            pltpu.SemaphoreType.DMA((N_BUF,)),
            pltpu.SemaphoreType.DMA((N_BUF,)),
            pltpu.VMEM((n_pages, N_HEADS * dec, page), F32),
            pltpu.VMEM((ATTN_WIDTH, LANES), F32),
        ],
    )
    return pl.pallas_call(
        _dec_kernel,
        grid_spec=grid_spec,
        out_shape=jax.ShapeDtypeStruct((nseq, dec, ATTN_WIDTH), F32),
        compiler_params=pltpu.CompilerParams(dimension_semantics=("arbitrary",),
                                             vmem_limit_bytes=VMEM_LIMIT),
        name="sample_attn",
    )(page_table, q_s, kn_pad, vn_pad, cache_kT, cache_vT)


def _rope_tables(pos):
    inv = jnp.power(ROPE_THETA, -jnp.arange(ROT_HALF, dtype=F32) * (2.0 / ROT_DIM))
    ang = pos.astype(F32)[:, None] * inv[None, :]
    cos, sin = jnp.cos(ang), jnp.sin(ang)
    lane = jnp.arange(LANES) % HEAD_DIM
    f = lane % ROT_HALF
    c_tab = jnp.where(lane[None, :] < ROT_DIM, cos[:, f], 1.0)
    s1_tab = jnp.where(lane[None, :] < ROT_HALF, -sin[:, f], 0.0)
    s2_tab = jnp.where((lane[None, :] >= ROT_HALF) & (lane[None, :] < ROT_DIM), sin[:, f], 0.0)
    return cos.T, sin.T, c_tab, s1_tab, s2_tab


def _layer(l, xp, xs, cache_k, cache_v, state_pool, page_table, norm1_g, w_in, w_pool, pool_scale,
           w_o, norm2_g, w_up, w_down, final_g):
    seq = xp.shape[0]
    nseq, dec, _ = xs.shape
    past_len = page_table.shape[1] * cache_k.shape[2]

    g1 = norm1_g[l][None, :]
    g2 = norm2_g[l][None, :]
    win = w_in[l].astype(BF16)
    wqkvT = win[:, :3 * ATTN_WIDTH].T
    wku = jnp.concatenate([win[:, ATTN_WIDTH:2 * ATTN_WIDTH], win[:, 3 * ATTN_WIDTH:]], axis=1)
    wpool = w_pool[l].astype(BF16)
    pscale = pool_scale[l][None, :]
    wo = w_o[l].astype(BF16)
    wup = w_up[l].astype(BF16)
    wdown = w_down[l].astype(BF16)
    gf = final_g[None, :]

    cosT, sinT, c_tab, s1_tab, s2_tab = _rope_tables(jnp.arange(seq, dtype=jnp.int32))
    qT_blk, kT, vT, v_blk, k_aug, kmean, pool_p, u_tail = _prompt_proj(
        xp, g1, wqkvT, wku, cosT, sinT, c_tab, s1_tab, s2_tab, wpool, pscale)
    nb = seq // MOBA_BLOCK
    kmean_hm = kmean[:, 0, :].reshape(nb, N_HEADS, HEAD_DIM).transpose(1, 0, 2)
    attn_p = _prompt_attn(qT_blk, k_aug, v_blk, kmean_hm)
    yp = _out_mlp(xp, attn_p, pool_p, wo, g2, wup, wdown, gf, tl=512, tf=1024)
    k_prompt = kT.reshape(N_HEADS, HEAD_DIM, seq).transpose(2, 0, 1)[None]
    v_prompt = vT.reshape(N_HEADS, HEAD_DIM, seq).transpose(2, 0, 1)[None]
    pool_prompt = u_tail[HALO - POOL_STATE:][None]

    rows = nseq * dec
    pos_s = past_len + jnp.arange(dec, dtype=jnp.int32)
    _, _, c_s, s1_s, s2_s = _rope_tables(jnp.tile(pos_s, nseq))
    state = state_pool[l].astype(F32)
    state_pad = jnp.pad(state, ((0, 0), (HALO - POOL_STATE, 0), (0, 0)))
    q_s, k_s, v_s, u_s, pool_s = _sample_proj(
        xs.reshape(rows, D_MODEL), g1, win, c_s, s1_s, s2_s, state_pad, wpool, pscale)
    pad = ((0, 0), (0, LANES - dec), (0, 0))
    kn_pad = jnp.pad(k_s.reshape(nseq, dec, ATTN_WIDTH), pad).astype(BF16)
    vn_pad = jnp.pad(v_s.reshape(nseq, dec, ATTN_WIDTH), pad).astype(BF16)
    cache_kT = jnp.transpose(cache_k[l], (0, 2, 3, 1))
    cache_vT = jnp.transpose(cache_v[l], (0, 2, 3, 1))
    attn_s = _sample_attn(page_table, q_s.reshape(nseq, dec, ATTN_WIDTH), kn_pad, vn_pad,
                          cache_kT, cache_vT)
    ys = _out_mlp(xs.reshape(rows, D_MODEL), attn_s.reshape(rows, ATTN_WIDTH).astype(BF16), pool_s,
                  wo, g2, wup, wdown, gf, tl=rows, tf=1024)
    k_sample = k_s.reshape(nseq, dec, N_HEADS, HEAD_DIM)
    v_sample = v_s.reshape(nseq, dec, N_HEADS, HEAD_DIM)
    u_ext = jnp.concatenate([state, u_s.reshape(nseq, dec, POOL_WIDTH)], axis=1)
    pool_sample = u_ext[:, u_ext.shape[1] - POOL_STATE:]
    return (yp, ys.reshape(nseq, dec, D_MODEL), k_prompt, v_prompt, pool_prompt,
            k_sample, v_sample, pool_sample)


def kernel(x_prompt, x_sample, cache_k, cache_v, state_pool, page_table, norm1_g, w_in, w_pool,
           pool_scale, w_o, norm2_g, w_up, w_down, final_g):
    depth = w_in.shape[0]
    assert depth == 1 and x_prompt.shape[0] == 1, "single layer, single prompt sequence"
    (yp, ys, k_p, v_p, pool_p, k_s, v_s, pool_s) = _layer(
        0, x_prompt[0], x_sample, cache_k, cache_v, state_pool, page_table, norm1_g, w_in, w_pool,
        pool_scale, w_o, norm2_g, w_up, w_down, final_g)
    return (yp[None], ys, k_p[None], v_p[None], pool_p[None], k_s[None], v_s[None], pool_s[None])
```

```python
import functools

import jax
import jax.numpy as jnp
from jax import lax
from jax.experimental import pallas as pl
from jax.experimental.pallas import tpu as pltpu

F32 = jnp.float32
BF16 = jnp.bfloat16

D_MODEL = 1024
HEAD_DIM = 64
N_HEADS = 8
ATTN_WIDTH = N_HEADS * HEAD_DIM
POOL_WINDOWS = (2, 4, 8, 16)
POOL_CH = 128
POOL_WIDTH = POOL_CH * len(POOL_WINDOWS)
POOL_STATE = max(POOL_WINDOWS) - 1
HALO = 16
D_FF = 4 * D_MODEL
MOBA_BLOCK = 256
MOBA_TOP_K = 3
ROPE_THETA = 500000.0
ROT_DIM = HEAD_DIM // 4
ROT_HALF = ROT_DIM // 2
ATTN_SCALE = HEAD_DIM ** -0.5
RMS_EPS = 1e-6
NEG = -1e30
LANES = 128
ATTN_TILE = 512
VMEM_LIMIT = 56 * 1024 * 1024

_NT = (((1,), (1,)), ((), ()))


def _dot(a, b):
    return jnp.dot(a, b, preferred_element_type=F32)


def _dot_nt(a, b):
    return lax.dot_general(a, b, _NT, preferred_element_type=F32)


def _split_bf16(x):
    hi = x.astype(BF16)
    lo = (x - hi.astype(F32)).astype(BF16)
    return hi, lo


def _dot3(a, b):
    a_hi, a_lo = _split_bf16(a)
    b_hi, b_lo = _split_bf16(b)
    return _dot(a_hi, b_hi) + _dot(a_hi, b_lo) + _dot(a_lo, b_hi)


def _rmsnorm(x, g):
    ms = jnp.mean(x * x, axis=-1, keepdims=True)
    return x * lax.rsqrt(ms + RMS_EPS) * g


def _top_k_mask(gate, valid, index, axis):
    limit = gate.shape[axis]
    g = jnp.where(valid, gate, -jnp.inf)
    sel = jnp.zeros(gate.shape, jnp.bool_)
    for _ in range(MOBA_TOP_K):
        mx = jnp.max(g, axis=axis, keepdims=True)
        first = jnp.min(jnp.where(g == mx, index, limit), axis=axis, keepdims=True)
        pick = jnp.logical_and(index == first, mx > -jnp.inf)
        sel = jnp.logical_or(sel, pick)
        g = jnp.where(pick, -jnp.inf, g)
    return sel


def _proj_kernel(x_ref, g_ref, wqkvT_ref, wku_ref, cosT_ref, sinT_ref, c_ref, s1_ref, s2_ref,
                 wpool_ref, pscale_ref,
                 qT_ref, kT_ref, vT_ref, vblk_ref, kaug_ref, kmean_ref, pool_ref, utail_ref,
                 ext_ref):
    i = pl.program_id(0)
    tl = x_ref.shape[0]
    hb = _rmsnorm(x_ref[...], g_ref[...]).astype(BF16)

    zT = _dot_nt(wqkvT_ref[...], hb)
    cosT = cosT_ref[...]
    sinT = sinT_ref[...]
    for hd in range(2 * N_HEADS):
        base = hd * HEAD_DIM
        x1 = zT[base:base + ROT_HALF]
        x2 = zT[base + ROT_HALF:base + ROT_DIM]
        rest = zT[base + ROT_DIM:base + HEAD_DIM]
        r1 = x1 * cosT - x2 * sinT
        r2 = x2 * cosT + x1 * sinT
        if hd < N_HEADS:
            qT_ref[base:base + ROT_HALF, :] = r1 * ATTN_SCALE
            qT_ref[base + ROT_HALF:base + ROT_DIM, :] = r2 * ATTN_SCALE
            qT_ref[base + ROT_DIM:base + HEAD_DIM, :] = rest * ATTN_SCALE
        else:
            kb = base - ATTN_WIDTH
            kT_ref[kb:kb + ROT_HALF, :] = r1
            kT_ref[kb + ROT_HALF:kb + ROT_DIM, :] = r2
            kT_ref[kb + ROT_DIM:kb + HEAD_DIM, :] = rest
    vT = zT[2 * ATTN_WIDTH:3 * ATTN_WIDTH]
    vT_ref[...] = vT
    vblk_ref[0] = vT.astype(BF16)

    z2 = _dot(hb, wku_ref[...])
    lane = lax.broadcasted_iota(jnp.int32, (tl, LANES), 1)
    block_tag = jnp.where(lane - HEAD_DIM == i, 1.0, 0.0)
    low = lane < HEAD_DIM
    for g in range(ATTN_WIDTH // LANES):
        xg = z2[:, LANES * g:LANES * (g + 1)]
        kr = (xg * c_ref[...] + pltpu.roll(xg, ROT_HALF, 1) * s2_ref[...]
              + pltpu.roll(xg, LANES - ROT_HALF, 1) * s1_ref[...])
        kaug_ref[2 * g] = jnp.where(low, kr, block_tag).astype(BF16)
        kaug_ref[2 * g + 1] = jnp.where(low, pltpu.roll(kr, HEAD_DIM, 1), block_tag).astype(BF16)
        ksum = jnp.sum(kr, axis=0, keepdims=True) * (1.0 / tl)
        kmean_ref[0, :, LANES * g:LANES * (g + 1)] = jnp.broadcast_to(ksum, (8, LANES))

    u = z2[:, ATTN_WIDTH:]

    @pl.when(i == 0)
    def _():
        ext_ref[0:HALO, :] = jnp.zeros((HALO, POOL_WIDTH), F32)

    ext_ref[HALO:HALO + tl, :] = u
    row = lax.broadcasted_iota(jnp.int32, (tl, 1), 0) + i * tl
    for g, w in enumerate(POOL_WINDOWS):
        lanes = slice(POOL_CH * g, POOL_CH * (g + 1))
        ug = ext_ref[HALO:HALO + tl, lanes]
        acc = ug
        for j in range(1, w):
            acc = acc + ext_ref[HALO - j:HALO - j + tl, lanes]
        cnt = jnp.minimum(row + 1, w).astype(F32)
        d = acc / cnt - ug
        y = _dot(d.astype(BF16), wpool_ref[g]) * pscale_ref[:, lanes]
        pool_ref[:, lanes] = y.astype(BF16)
    tail = ext_ref[tl:tl + HALO, :]
    ext_ref[0:HALO, :] = tail
    utail_ref[...] = tail


def _prompt_proj(x, g1, wqkvT, wku, cosT, sinT, c_tab, s1_tab, s2_tab, wpool, pscale):
    seq = x.shape[0]
    tl = MOBA_BLOCK
    nb = seq // tl
    bpt = ATTN_TILE // tl
    const2 = lambda i: (0, 0)
    return pl.pallas_call(
        _proj_kernel,
        grid=(nb,),
        in_specs=[
            pl.BlockSpec((tl, D_MODEL), lambda i: (i, 0)),
            pl.BlockSpec((1, D_MODEL), const2),
            pl.BlockSpec((3 * ATTN_WIDTH, D_MODEL), const2),
            pl.BlockSpec((D_MODEL, ATTN_WIDTH + POOL_WIDTH), const2),
            pl.BlockSpec((ROT_HALF, tl), lambda i: (0, i)),
            pl.BlockSpec((ROT_HALF, tl), lambda i: (0, i)),
            pl.BlockSpec((tl, LANES), lambda i: (i, 0)),
            pl.BlockSpec((tl, LANES), lambda i: (i, 0)),
            pl.BlockSpec((tl, LANES), lambda i: (i, 0)),
            pl.BlockSpec((len(POOL_WINDOWS), POOL_CH, POOL_CH), lambda i: (0, 0, 0)),
            pl.BlockSpec((1, POOL_WIDTH), const2),
        ],
        out_specs=[
            pl.BlockSpec((ATTN_WIDTH, tl), lambda i: (0, i)),
            pl.BlockSpec((ATTN_WIDTH, tl), lambda i: (0, i)),
            pl.BlockSpec((ATTN_WIDTH, tl), lambda i: (0, i)),
            pl.BlockSpec((1, ATTN_WIDTH, tl), lambda i: (i // bpt, 0, i % bpt)),
            pl.BlockSpec((N_HEADS, tl, LANES), lambda i: (0, i, 0)),
            pl.BlockSpec((1, 8, ATTN_WIDTH), lambda i: (i, 0, 0)),
            pl.BlockSpec((tl, POOL_WIDTH), lambda i: (i, 0)),
            pl.BlockSpec((HALO, POOL_WIDTH), const2),
        ],
        out_shape=[
            jax.ShapeDtypeStruct((ATTN_WIDTH, seq), F32),
            jax.ShapeDtypeStruct((ATTN_WIDTH, seq), F32),
            jax.ShapeDtypeStruct((ATTN_WIDTH, seq), F32),
            jax.ShapeDtypeStruct((nb // bpt, ATTN_WIDTH, bpt * tl), BF16),
            jax.ShapeDtypeStruct((N_HEADS, seq, LANES), BF16),
            jax.ShapeDtypeStruct((nb, 8, ATTN_WIDTH), F32),
            jax.ShapeDtypeStruct((seq, POOL_WIDTH), BF16),
            jax.ShapeDtypeStruct((HALO, POOL_WIDTH), F32),
        ],
        scratch_shapes=[pltpu.VMEM((HALO + tl, POOL_WIDTH), F32)],
        compiler_params=pltpu.CompilerParams(dimension_semantics=("arbitrary",),
                                             vmem_limit_bytes=VMEM_LIMIT),
        name="prompt_proj",
    )(x, g1, wqkvT, wku, cosT, sinT, c_tab, s1_tab, s2_tab, wpool, pscale)


def _attn_kernel(q_ref, k_ref, v_ref, km_ref, o_ref, s_ref, cmax_ref):
    tile_id = pl.program_id(1)
    nblk = km_ref.shape[1]
    tile = q_ref.shape[1]
    blk_idx = lax.broadcasted_iota(jnp.int32, (nblk, tile), 0)
    q_blk = (tile_id * (tile // MOBA_BLOCK)
             + lax.broadcasted_iota(jnp.int32, (nblk, tile), 1) // MOBA_BLOCK)
    key_idx = lax.broadcasted_iota(jnp.int32, (tile, tile), 0)
    qry_idx = lax.broadcasted_iota(jnp.int32, (tile, tile), 1)
    ones = jnp.ones((8, tile), BF16)

    q_aug = []
    for hh in range(2):
        qT = q_ref[HEAD_DIM * hh:HEAD_DIM * (hh + 1), :]
        gate = _dot3(km_ref[hh], qT)
        sel = _top_k_mask(gate, blk_idx < q_blk, blk_idx, 0)
        bias = jnp.where(jnp.logical_or(sel, blk_idx == q_blk), 0.0, NEG)
        q_aug.append(jnp.concatenate([qT.astype(BF16), bias.astype(BF16)], axis=0))

    def scores(hh, c):
        start = pl.multiple_of(c * tile, tile)
        return _dot(k_ref[hh, pl.ds(start, tile), :], q_aug[hh])

    def values(hh, c):
        return v_ref[c, HEAD_DIM * hh:HEAD_DIM * (hh + 1), :]

    def produce(slot, c, causal):
        for hh in range(2):
            s = scores(hh, c)
            if causal:
                s = jnp.where(key_idx <= qry_idx, s, NEG)
            s_ref[slot, hh] = s
            cmax_ref[slot, hh] = jnp.broadcast_to(jnp.max(s, axis=0, keepdims=True), (8, tile))

    def consume(slot, c, carry):
        out = []
        for hh in range(2):
            m, l, acc = carry[3 * hh:3 * hh + 3]
            m_new = jnp.maximum(m, cmax_ref[slot, hh, 0:1, :])
            alpha = jnp.exp(m - m_new)
            p = jnp.exp(s_ref[slot, hh] - m_new).astype(BF16)
            out += [m_new, l * alpha + _dot(ones, p)[0:1], acc * alpha + _dot(values(hh, c), p)]
        return tuple(out)

    produce(0, tile_id, True)
    init = (jnp.full((1, tile), NEG, F32), jnp.zeros((1, tile), F32), jnp.zeros((HEAD_DIM, tile), F32))

    def pair(i, carry):
        produce(1, 2 * i, False)
        carry = consume(0, jnp.where(i == 0, tile_id, 2 * i - 1), carry)
        produce(0, 2 * i + 1, False)
        return consume(1, 2 * i, carry)

    state = lax.fori_loop(0, tile_id // 2, pair, init + init)
    last = tile_id - 1

    def odd_tail(carry):
        produce(1, last, False)
        carry = consume(0, jnp.where(last == 0, tile_id, last - 1), carry)
        return consume(1, last, carry)

    def even_tail(carry):
        return consume(0, jnp.where(tile_id == 0, tile_id, last), carry)

    state = lax.cond(tile_id % 2 == 1, odd_tail, even_tail, state)
    oT = jnp.concatenate([state[2] / state[1], state[5] / state[4]], axis=0)
    o_ref[...] = oT.T.astype(BF16)


def _prompt_attn(qT, k_aug, v_tiles, kmean_hm):
    seq = qT.shape[1]
    nt, _, tile = v_tiles.shape
    nblk = kmean_hm.shape[1]
    return pl.pallas_call(
        _attn_kernel,
        grid=(N_HEADS // 2, nt),
        in_specs=[
            pl.BlockSpec((2 * HEAD_DIM, tile), lambda p, t: (p, t)),
            pl.BlockSpec((2, seq, LANES), lambda p, t: (p, 0, 0)),
            pl.BlockSpec((nt, 2 * HEAD_DIM, tile), lambda p, t: (0, p, 0)),
            pl.BlockSpec((2, nblk, HEAD_DIM), lambda p, t: (p, 0, 0)),
        ],
        out_specs=pl.BlockSpec((tile, 2 * HEAD_DIM), lambda p, t: (t, p)),
        out_shape=jax.ShapeDtypeStruct((seq, ATTN_WIDTH), BF16),
        scratch_shapes=[pltpu.VMEM((2, 2, tile, tile), F32), pltpu.VMEM((2, 2, 8, tile), F32)],
        compiler_params=pltpu.CompilerParams(dimension_semantics=("arbitrary", "arbitrary"),
                                             vmem_limit_bytes=VMEM_LIMIT),
        name="prompt_attn",
    )(qT, k_aug, v_tiles, kmean_hm)


def _mlp_kernel(x_ref, attn_ref, pool_ref, wo_ref, g2_ref, wup_ref, wdown_ref, gf_ref, y_ref,
                x1_ref, hn_ref, acc_ref):
    j = pl.program_id(1)

    @pl.when(j == 0)
    def _():
        x1 = (x_ref[...] + _dot(attn_ref[...], wo_ref[0:ATTN_WIDTH, :])
              + _dot(pool_ref[...], wo_ref[ATTN_WIDTH:, :]))
        x1_ref[...] = x1
        hn_ref[...] = _rmsnorm(x1, g2_ref[...]).astype(BF16)
        acc_ref[...] = jnp.zeros(acc_ref.shape, F32)

    hid = jnp.maximum(_dot(hn_ref[...], wup_ref[...]), 0.0)
    acc_ref[...] += _dot((hid * hid).astype(BF16), wdown_ref[...])

    @pl.when(j == pl.num_programs(1) - 1)
    def _():
        y_ref[...] = _rmsnorm(x1_ref[...] + acc_ref[...], gf_ref[...])


def _out_mlp(x, attn, pool, wo, g2, wup, wdown, gf, tl, tf):
    rows = x.shape[0]
    const2 = lambda i, j: (0, 0)
    return pl.pallas_call(
        _mlp_kernel,
        grid=(rows // tl, D_FF // tf),
        in_specs=[
            pl.BlockSpec((tl, D_MODEL), lambda i, j: (i, 0)),
            pl.BlockSpec((tl, ATTN_WIDTH), lambda i, j: (i, 0)),
            pl.BlockSpec((tl, POOL_WIDTH), lambda i, j: (i, 0)),
            pl.BlockSpec((D_MODEL, D_MODEL), const2),
            pl.BlockSpec((1, D_MODEL), const2),
            pl.BlockSpec((D_MODEL, tf), lambda i, j: (0, j)),
            pl.BlockSpec((tf, D_MODEL), lambda i, j: (j, 0)),
            pl.BlockSpec((1, D_MODEL), const2),
        ],
        out_specs=pl.BlockSpec((tl, D_MODEL), lambda i, j: (i, 0)),
        out_shape=jax.ShapeDtypeStruct((rows, D_MODEL), F32),
        scratch_shapes=[pltpu.VMEM((tl, D_MODEL), F32), pltpu.VMEM((tl, D_MODEL), BF16),
                        pltpu.VMEM((tl, D_MODEL), F32)],
        compiler_params=pltpu.CompilerParams(dimension_semantics=("arbitrary", "arbitrary"),
                                             vmem_limit_bytes=VMEM_LIMIT),
        name="out_mlp",
    )(x, attn, pool, wo, g2, wup, wdown, gf)


def _sproj_kernel(x_ref, g_ref, win_ref, c_ref, s1_ref, s2_ref, state_ref, wpool_ref, pscale_ref,
                  q_ref, k_ref, v_ref, u_ref, pool_ref, ext_ref):
    rows = x_ref.shape[0]
    nseq, dec = state_ref.shape[0], rows // state_ref.shape[0]
    hb = _rmsnorm(x_ref[...], g_ref[...]).astype(BF16)
    z = _dot(hb, win_ref[...])
    for g in range(2 * ATTN_WIDTH // LANES):
        xg = z[:, LANES * g:LANES * (g + 1)]
        r = (xg * c_ref[...] + pltpu.roll(xg, ROT_HALF, 1) * s2_ref[...]
             + pltpu.roll(xg, LANES - ROT_HALF, 1) * s1_ref[...])
        if LANES * g < ATTN_WIDTH:
            q_ref[:, LANES * g:LANES * (g + 1)] = r * ATTN_SCALE
        else:
            k_ref[:, LANES * g - ATTN_WIDTH:LANES * (g + 1) - ATTN_WIDTH] = r
    v_ref[...] = z[:, 2 * ATTN_WIDTH:3 * ATTN_WIDTH]
    u = z[:, 3 * ATTN_WIDTH:]
    u_ref[...] = u

    ext_ref[:, 0:HALO, :] = state_ref[...]
    ext_ref[:, HALO:HALO + dec, :] = u.reshape(nseq, dec, POOL_WIDTH)
    for g, w in enumerate(POOL_WINDOWS):
        lanes = slice(POOL_CH * g, POOL_CH * (g + 1))
        ug = ext_ref[:, HALO:HALO + dec, lanes]
        acc = ug
        for j in range(1, w):
            acc = acc + ext_ref[:, HALO - j:HALO - j + dec, lanes]
        d = (acc / float(w) - ug).reshape(rows, POOL_CH)
        y = _dot(d.astype(BF16), wpool_ref[g]) * pscale_ref[:, lanes]
        pool_ref[:, lanes] = y.astype(BF16)


def _sample_proj(x, g1, win, c_tab, s1_tab, s2_tab, state_pad, wpool, pscale):
    rows = x.shape[0]
    nseq = state_pad.shape[0]
    shapes = [jax.ShapeDtypeStruct((rows, ATTN_WIDTH), F32)] * 3 + [
        jax.ShapeDtypeStruct((rows, POOL_WIDTH), F32), jax.ShapeDtypeStruct((rows, POOL_WIDTH), BF16)]
    return pl.pallas_call(
        _sproj_kernel,
        out_shape=shapes,
        scratch_shapes=[pltpu.VMEM((nseq, HALO + rows // nseq, POOL_WIDTH), F32)],
        compiler_params=pltpu.CompilerParams(vmem_limit_bytes=VMEM_LIMIT),
        name="sample_proj",
    )(x, g1, win, c_tab, s1_tab, s2_tab, state_pad, wpool, pscale)


GROUP = 8
N_BUF = 2 * GROUP


def _dec_kernel(pt_ref, q_ref, kn_ref, vn_ref, ck_hbm, cv_hbm, o_ref,
                kbuf, vbuf, ksem, vsem, s_scr, p_scr, bias_scr):
    b = pl.program_id(0)
    n_pages = pt_ref.shape[1]
    page = kbuf.shape[-1]
    ppb = MOBA_BLOCK // page
    nblk = n_pages // ppb
    n_groups = n_pages // GROUP
    dec = q_ref.shape[1]
    rows = N_HEADS * dec

    def slot_of(g, e):
        return (g % (N_BUF // GROUP)) * GROUP + e

    def k_copy(j, slot):
        return pltpu.make_async_copy(ck_hbm.at[pt_ref[b, j]], kbuf.at[slot], ksem.at[slot])

    def v_copy(j, slot):
        return pltpu.make_async_copy(cv_hbm.at[pt_ref[b, j]], vbuf.at[slot], vsem.at[slot])

    for s in range(N_BUF):
        k_copy(s, s).start()
    for s in range(N_BUF):
        v_copy(s, s).start()

    r_idx = lax.broadcasted_iota(jnp.int32, (rows, ATTN_WIDTH), 0)
    c_idx = lax.broadcasted_iota(jnp.int32, (rows, ATTN_WIDTH), 1)
    head_mask = (c_idx // HEAD_DIM) == (r_idx // dec)
    q_bd = jnp.where(head_mask, jnp.concatenate([q_ref[0]] * N_HEADS, axis=0), 0.0)
    q_hi, q_lo = _split_bf16(q_bd)
    lane = lax.broadcasted_iota(jnp.int32, (rows, LANES), 1)

    def refill(copy, g):
        @pl.when((g + N_BUF // GROUP) * GROUP < n_pages)
        def _():
            for e in range(GROUP):
                copy(g * GROUP + e + N_BUF, slot_of(g, e)).start()

    def k_group(g, gate):
        for e in range(GROUP):
            k_copy(g * GROUP + e, slot_of(g, e)).wait()
        for e in range(GROUP):
            j = g * GROUP + e
            kt = kbuf[slot_of(g, e)].reshape(ATTN_WIDTH, page).astype(BF16)
            s = _dot(q_hi, kt) + _dot(q_lo, kt)
            s_scr[j] = s
            blk = s if e % ppb == 0 else blk + s
            if e % ppb == ppb - 1:
                col = jnp.sum(blk, axis=1, keepdims=True)
                gate = jnp.where(lane == j // ppb, col, gate)
        refill(k_copy, g)
        return gate

    gate = lax.fori_loop(0, n_groups, k_group, jnp.zeros((rows, LANES), F32)) * (1.0 / MOBA_BLOCK)

    sel = _top_k_mask(gate, lane < nblk, lane, 1)
    bias = jnp.where(sel, 0.0, NEG)

    def bias_group(g, carry):
        for e in range(GROUP):
            n = g * GROUP + e
            col = jnp.sum(jnp.where(lane == n, bias, 0.0), axis=1, keepdims=True)
            bias_scr[n] = jnp.broadcast_to(col, (rows, LANES))
        return carry

    lax.fori_loop(0, nblk // GROUP, bias_group, 0)

    q_pos = lax.broadcasted_iota(jnp.int32, (rows, LANES), 0) % dec
    s_own = jnp.where(lane <= q_pos, _dot_nt(q_hi, kn_ref[0]), NEG)

    def max_group(g, mv):
        for e in range(GROUP):
            j = g * GROUP + e
            mv = jnp.maximum(mv, s_scr[j] + bias_scr[j // ppb])
        return mv

    m = jnp.max(lax.fori_loop(0, n_groups, max_group, s_own), axis=1, keepdims=True)
    p_own = jnp.exp(s_own - m)

    def exp_group(g, lv):
        for e in range(GROUP):
            j = g * GROUP + e
            p = jnp.exp(s_scr[j] + (bias_scr[j // ppb] - m))
            p_scr[j] = p.astype(BF16)
            lv = lv + p
        return lv

    l = jnp.sum(lax.fori_loop(0, n_groups, exp_group, p_own), axis=1, keepdims=True)

    def v_group(g, acc):
        for e in range(GROUP):
            v_copy(g * GROUP + e, slot_of(g, e)).wait()
        for e in range(GROUP):
            vt = vbuf[slot_of(g, e)].reshape(ATTN_WIDTH, page).astype(BF16)
            acc = acc + _dot_nt(p_scr[g * GROUP + e], vt)
        refill(v_copy, g)
        return acc

    acc = lax.fori_loop(0, n_groups, v_group, _dot(p_own.astype(BF16), vn_ref[0]))
    o = jnp.where(head_mask, acc / l, 0.0)
    out = o[0:dec]
    for h in range(1, N_HEADS):
        out = out + o[h * dec:(h + 1) * dec]
    o_ref[0] = out


def _sample_attn(page_table, q_s, kn_pad, vn_pad, cache_kT, cache_vT):
    nseq, n_pages = page_table.shape
    dec = q_s.shape[1]
    page = cache_kT.shape[-1]
    grid_spec = pltpu.PrefetchScalarGridSpec(
        num_scalar_prefetch=1,
        grid=(nseq,),
        in_specs=[
            pl.BlockSpec((1, dec, ATTN_WIDTH), lambda b, pt: (b, 0, 0)),
            pl.BlockSpec((1, LANES, ATTN_WIDTH), lambda b, pt: (b, 0, 0)),
            pl.BlockSpec((1, LANES, ATTN_WIDTH), lambda b, pt: (b, 0, 0)),
            pl.BlockSpec(memory_space=pl.ANY),
            pl.BlockSpec(memory_space=pl.ANY),
        ],
        out_specs=pl.BlockSpec((1, dec, ATTN_WIDTH), lambda b, pt: (b, 0, 0)),
        scratch_shapes=[
            pltpu.VMEM((N_BUF, N_HEADS, HEAD_DIM, page), F32),
            pltpu.VMEM((N_BUF, N_HEADS, HEAD_DIM, page), F32),
            pltpu.SemaphoreType.DMA((N_BUF,)),
            pltpu.SemaphoreType.DMA((N_BUF,)),
            pltpu.VMEM((n_pages, N_HEADS * dec, page), F32),
            pltpu.VMEM((n_pages, N_HEADS * dec, page), BF16),
            pltpu.VMEM((n_pages * page // MOBA_BLOCK, N_HEADS * dec, LANES), F32),
        ],
    )
    return pl.pallas_call(
        _dec_kernel,
        grid_spec=grid_spec,
        out_shape=jax.ShapeDtypeStruct((nseq, dec, ATTN_WIDTH), F32),
        compiler_params=pltpu.CompilerParams(dimension_semantics=("arbitrary",),
                                             vmem_limit_bytes=VMEM_LIMIT),
        name="sample_attn",
    )(page_table, q_s, kn_pad, vn_pad, cache_kT, cache_vT)


def _rope_tables(pos):
    inv = jnp.power(ROPE_THETA, -jnp.arange(ROT_HALF, dtype=F32) * (2.0 / ROT_DIM))
    ang = pos.astype(F32)[:, None] * inv[None, :]
    cos, sin = jnp.cos(ang), jnp.sin(ang)
    lane = jnp.arange(LANES) % HEAD_DIM
    f = lane % ROT_HALF
    c_tab = jnp.where(lane[None, :] < ROT_DIM, cos[:, f], 1.0)
    s1_tab = jnp.where(lane[None, :] < ROT_HALF, -sin[:, f], 0.0)
    s2_tab = jnp.where((lane[None, :] >= ROT_HALF) & (lane[None, :] < ROT_DIM), sin[:, f], 0.0)
    return cos.T, sin.T, c_tab, s1_tab, s2_tab


def _layer(l, xp, xs, cache_k, cache_v, state_pool, page_table, norm1_g, w_in, w_pool, pool_scale,
           w_o, norm2_g, w_up, w_down, final_g):
    seq = xp.shape[0]
    nseq, dec, _ = xs.shape
    past_len = page_table.shape[1] * cache_k.shape[2]

    g1 = norm1_g[l][None, :]
    g2 = norm2_g[l][None, :]
    win = w_in[l].astype(BF16)
    wqkvT = win[:, :3 * ATTN_WIDTH].T
    wku = jnp.concatenate([win[:, ATTN_WIDTH:2 * ATTN_WIDTH], win[:, 3 * ATTN_WIDTH:]], axis=1)
    wpool = w_pool[l].astype(BF16)
    pscale = pool_scale[l][None, :]
    wo = w_o[l].astype(BF16)
    wup = w_up[l].astype(BF16)
    wdown = w_down[l].astype(BF16)
    gf = final_g[None, :]

    cosT, sinT, c_tab, s1_tab, s2_tab = _rope_tables(jnp.arange(seq, dtype=jnp.int32))
    qT, kT, vT, v_tiles, k_aug, kmean, pool_p, u_tail = _prompt_proj(
        xp, g1, wqkvT, wku, cosT, sinT, c_tab, s1_tab, s2_tab, wpool, pscale)
    nb = seq // MOBA_BLOCK
    kmean_hm = kmean[:, 0, :].reshape(nb, N_HEADS, HEAD_DIM).transpose(1, 0, 2)
    attn_p = _prompt_attn(qT, k_aug, v_tiles, kmean_hm)
    yp = _out_mlp(xp, attn_p, pool_p, wo, g2, wup, wdown, gf, tl=512, tf=1024)
    k_prompt = kT.reshape(N_HEADS, HEAD_DIM, seq).transpose(2, 0, 1)[None]
    v_prompt = vT.reshape(N_HEADS, HEAD_DIM, seq).transpose(2, 0, 1)[None]
    pool_prompt = u_tail[HALO - POOL_STATE:][None]

    rows = nseq * dec
    pos_s = past_len + jnp.arange(dec, dtype=jnp.int32)
    _, _, c_s, s1_s, s2_s = _rope_tables(jnp.tile(pos_s, nseq))
    state = state_pool[l].astype(F32)
    state_pad = jnp.pad(state, ((0, 0), (HALO - POOL_STATE, 0), (0, 0)))
    q_s, k_s, v_s, u_s, pool_s = _sample_proj(
        xs.reshape(rows, D_MODEL), g1, win, c_s, s1_s, s2_s, state_pad, wpool, pscale)
    pad = ((0, 0), (0, LANES - dec), (0, 0))
    kn_pad = jnp.pad(k_s.reshape(nseq, dec, ATTN_WIDTH), pad).astype(BF16)
    vn_pad = jnp.pad(v_s.reshape(nseq, dec, ATTN_WIDTH), pad).astype(BF16)
    cache_kT = jnp.transpose(cache_k[l], (0, 2, 3, 1))
    cache_vT = jnp.transpose(cache_v[l], (0, 2, 3, 1))
    attn_s = _sample_attn(page_table, q_s.reshape(nseq, dec, ATTN_WIDTH), kn_pad, vn_pad,
                          cache_kT, cache_vT)
    ys = _out_mlp(xs.reshape(rows, D_MODEL), attn_s.reshape(rows, ATTN_WIDTH).astype(BF16), pool_s,
                  wo, g2, wup, wdown, gf, tl=rows, tf=1024)
    k_sample = k_s.reshape(nseq, dec, N_HEADS, HEAD_DIM)
    v_sample = v_s.reshape(nseq, dec, N_HEADS, HEAD_DIM)
    u_ext = jnp.concatenate([state, u_s.reshape(nseq, dec, POOL_WIDTH)], axis=1)
    pool_sample = u_ext[:, u_ext.shape[1] - POOL_STATE:]
    return (yp, ys.reshape(nseq, dec, D_MODEL), k_prompt, v_prompt, pool_prompt,
            k_sample, v_sample, pool_sample)


def kernel(x_prompt, x_sample, cache_k, cache_v, state_pool, page_table, norm1_g, w_in, w_pool,
           pool_scale, w_o, norm2_g, w_up, w_down, final_g):
    depth = w_in.shape[0]
    assert depth == 1 and x_prompt.shape[0] == 1, "single layer, single prompt sequence"
    (yp, ys, k_p, v_p, pool_p, k_s, v_s, pool_s) = _layer(
        0, x_prompt[0], x_sample, cache_k, cache_v, state_pool, page_table, norm1_g, w_in, w_pool,
        pool_scale, w_o, norm2_g, w_up, w_down, final_g)
    return (yp[None], ys, k_p[None], v_p[None], pool_p[None], k_s[None], v_s[None], pool_s[None])
```

```python
import functools

import jax
import jax.numpy as jnp
from jax import lax
from jax.experimental import pallas as pl
from jax.experimental.pallas import tpu as pltpu

F32 = jnp.float32
BF16 = jnp.bfloat16

D_MODEL = 1024
HEAD_DIM = 64
N_HEADS = 8
ATTN_WIDTH = N_HEADS * HEAD_DIM
POOL_WINDOWS = (2, 4, 8, 16)
POOL_CH = 128
POOL_WIDTH = POOL_CH * len(POOL_WINDOWS)
POOL_STATE = max(POOL_WINDOWS) - 1
HALO = 16
D_FF = 4 * D_MODEL
MOBA_BLOCK = 256
MOBA_TOP_K = 3
ROPE_THETA = 500000.0
ROT_DIM = HEAD_DIM // 4
ROT_HALF = ROT_DIM // 2
ATTN_SCALE = HEAD_DIM ** -0.5
RMS_EPS = 1e-6
NEG = -1e30
LANES = 128
ATTN_TILE = 512
V_ROWS = HEAD_DIM + 16
VMEM_LIMIT = 56 * 1024 * 1024

_NT = (((1,), (1,)), ((), ()))


def _dot(a, b):
    return jnp.dot(a, b, preferred_element_type=F32)


def _dot_nt(a, b):
    return lax.dot_general(a, b, _NT, preferred_element_type=F32)


def _split_bf16(x):
    hi = x.astype(BF16)
    lo = (x - hi.astype(F32)).astype(BF16)
    return hi, lo


def _dot3(a, b):
    a_hi, a_lo = _split_bf16(a)
    b_hi, b_lo = _split_bf16(b)
    return _dot(a_hi, b_hi) + _dot(a_hi, b_lo) + _dot(a_lo, b_hi)


def _rmsnorm(x, g):
    ms = jnp.mean(x * x, axis=-1, keepdims=True)
    return x * lax.rsqrt(ms + RMS_EPS) * g


def _top_k_mask(gate, valid, index, axis):
    limit = gate.shape[axis]
    g = jnp.where(valid, gate, -jnp.inf)
    sel = jnp.zeros(gate.shape, jnp.bool_)
    for _ in range(MOBA_TOP_K):
        mx = jnp.max(g, axis=axis, keepdims=True)
        first = jnp.min(jnp.where(g == mx, index, limit), axis=axis, keepdims=True)
        pick = jnp.logical_and(index == first, mx > -jnp.inf)
        sel = jnp.logical_or(sel, pick)
        g = jnp.where(pick, -jnp.inf, g)
    return sel


def _proj_kernel(x_ref, g_ref, wqkvT_ref, wku_ref, cosT_ref, sinT_ref, c_ref, s1_ref, s2_ref,
                 wpool_ref, pscale_ref,
                 qT_ref, kT_ref, vT_ref, vblk_ref, kaug_ref, kmean_ref, pool_ref, utail_ref,
                 ext_ref):
    i = pl.program_id(0)
    tl = x_ref.shape[0]
    hb = _rmsnorm(x_ref[...], g_ref[...]).astype(BF16)

    zT = _dot_nt(wqkvT_ref[...], hb)
    cosT = cosT_ref[...]
    sinT = sinT_ref[...]
    for hd in range(2 * N_HEADS):
        base = hd * HEAD_DIM
        x1 = zT[base:base + ROT_HALF]
        x2 = zT[base + ROT_HALF:base + ROT_DIM]
        rest = zT[base + ROT_DIM:base + HEAD_DIM]
        r1 = x1 * cosT - x2 * sinT
        r2 = x2 * cosT + x1 * sinT
        if hd < N_HEADS:
            qT_ref[base:base + ROT_HALF, :] = r1 * ATTN_SCALE
            qT_ref[base + ROT_HALF:base + ROT_DIM, :] = r2 * ATTN_SCALE
            qT_ref[base + ROT_DIM:base + HEAD_DIM, :] = rest * ATTN_SCALE
        else:
            kb = base - ATTN_WIDTH
            kT_ref[kb:kb + ROT_HALF, :] = r1
            kT_ref[kb + ROT_HALF:kb + ROT_DIM, :] = r2
            kT_ref[kb + ROT_DIM:kb + HEAD_DIM, :] = rest
    vT = zT[2 * ATTN_WIDTH:3 * ATTN_WIDTH]
    vT_ref[...] = vT
    ones_rows = jnp.ones((V_ROWS - HEAD_DIM, tl), BF16)
    vT16 = vT.astype(BF16)
    vblk_ref[0] = jnp.concatenate(
        [piece for h in range(N_HEADS)
         for piece in (vT16[HEAD_DIM * h:HEAD_DIM * (h + 1)], ones_rows)], axis=0)

    z2 = _dot(hb, wku_ref[...])
    lane = lax.broadcasted_iota(jnp.int32, (tl, LANES), 1)
    block_tag = jnp.where(lane - HEAD_DIM == i, 1.0, 0.0)
    low = lane < HEAD_DIM
    for g in range(ATTN_WIDTH // LANES):
        xg = z2[:, LANES * g:LANES * (g + 1)]
        kr = (xg * c_ref[...] + pltpu.roll(xg, ROT_HALF, 1) * s2_ref[...]
              + pltpu.roll(xg, LANES - ROT_HALF, 1) * s1_ref[...])
        kaug_ref[2 * g] = jnp.where(low, kr, block_tag).astype(BF16)
        kaug_ref[2 * g + 1] = jnp.where(low, pltpu.roll(kr, HEAD_DIM, 1), block_tag).astype(BF16)
        ksum = jnp.sum(kr, axis=0, keepdims=True) * (1.0 / tl)
        kmean_ref[0, :, LANES * g:LANES * (g + 1)] = jnp.broadcast_to(ksum, (8, LANES))

    u = z2[:, ATTN_WIDTH:]

    @pl.when(i == 0)
    def _():
        ext_ref[0:HALO, :] = jnp.zeros((HALO, POOL_WIDTH), F32)

    ext_ref[HALO:HALO + tl, :] = u
    row = lax.broadcasted_iota(jnp.int32, (tl, 1), 0) + i * tl
    for g, w in enumerate(POOL_WINDOWS):
        lanes = slice(POOL_CH * g, POOL_CH * (g + 1))
        ug = ext_ref[HALO:HALO + tl, lanes]
        acc = ug
        for j in range(1, w):
            acc = acc + ext_ref[HALO - j:HALO - j + tl, lanes]
        cnt = jnp.minimum(row + 1, w).astype(F32)
        d = acc / cnt - ug
        y = _dot(d.astype(BF16), wpool_ref[g]) * pscale_ref[:, lanes]
        pool_ref[:, lanes] = y.astype(BF16)
    tail = ext_ref[tl:tl + HALO, :]
    ext_ref[0:HALO, :] = tail
    utail_ref[...] = tail


def _prompt_proj(x, g1, wqkvT, wku, cosT, sinT, c_tab, s1_tab, s2_tab, wpool, pscale):
    seq = x.shape[0]
    tl = MOBA_BLOCK
    nb = seq // tl
    bpt = ATTN_TILE // tl
    const2 = lambda i: (0, 0)
    return pl.pallas_call(
        _proj_kernel,
        grid=(nb,),
        in_specs=[
            pl.BlockSpec((tl, D_MODEL), lambda i: (i, 0)),
            pl.BlockSpec((1, D_MODEL), const2),
            pl.BlockSpec((3 * ATTN_WIDTH, D_MODEL), const2),
            pl.BlockSpec((D_MODEL, ATTN_WIDTH + POOL_WIDTH), const2),
            pl.BlockSpec((ROT_HALF, tl), lambda i: (0, i)),
            pl.BlockSpec((ROT_HALF, tl), lambda i: (0, i)),
            pl.BlockSpec((tl, LANES), lambda i: (i, 0)),
            pl.BlockSpec((tl, LANES), lambda i: (i, 0)),
            pl.BlockSpec((tl, LANES), lambda i: (i, 0)),
            pl.BlockSpec((len(POOL_WINDOWS), POOL_CH, POOL_CH), lambda i: (0, 0, 0)),
            pl.BlockSpec((1, POOL_WIDTH), const2),
        ],
        out_specs=[
            pl.BlockSpec((ATTN_WIDTH, tl), lambda i: (0, i)),
            pl.BlockSpec((ATTN_WIDTH, tl), lambda i: (0, i)),
            pl.BlockSpec((ATTN_WIDTH, tl), lambda i: (0, i)),
            pl.BlockSpec((1, N_HEADS * V_ROWS, tl), lambda i: (i // bpt, 0, i % bpt)),
            pl.BlockSpec((N_HEADS, tl, LANES), lambda i: (0, i, 0)),
            pl.BlockSpec((1, 8, ATTN_WIDTH), lambda i: (i, 0, 0)),
            pl.BlockSpec((tl, POOL_WIDTH), lambda i: (i, 0)),
            pl.BlockSpec((HALO, POOL_WIDTH), const2),
        ],
        out_shape=[
            jax.ShapeDtypeStruct((ATTN_WIDTH, seq), F32),
            jax.ShapeDtypeStruct((ATTN_WIDTH, seq), F32),
            jax.ShapeDtypeStruct((ATTN_WIDTH, seq), F32),
            jax.ShapeDtypeStruct((nb // bpt, N_HEADS * V_ROWS, bpt * tl), BF16),
            jax.ShapeDtypeStruct((N_HEADS, seq, LANES), BF16),
            jax.ShapeDtypeStruct((nb, 8, ATTN_WIDTH), F32),
            jax.ShapeDtypeStruct((seq, POOL_WIDTH), BF16),
            jax.ShapeDtypeStruct((HALO, POOL_WIDTH), F32),
        ],
        scratch_shapes=[pltpu.VMEM((HALO + tl, POOL_WIDTH), F32)],
        compiler_params=pltpu.CompilerParams(dimension_semantics=("arbitrary",),
                                             vmem_limit_bytes=VMEM_LIMIT),
        name="prompt_proj",
    )(x, g1, wqkvT, wku, cosT, sinT, c_tab, s1_tab, s2_tab, wpool, pscale)


def _attn_kernel(q_ref, k_ref, v_ref, km_ref, o_ref, s_ref, cmax_ref):
    tile_id = pl.program_id(1)
    nblk = km_ref.shape[1]
    tile = q_ref.shape[1]
    blk_idx = lax.broadcasted_iota(jnp.int32, (nblk, tile), 0)
    q_blk = (tile_id * (tile // MOBA_BLOCK)
             + lax.broadcasted_iota(jnp.int32, (nblk, tile), 1) // MOBA_BLOCK)
    key_idx = lax.broadcasted_iota(jnp.int32, (tile, tile), 0)
    qry_idx = lax.broadcasted_iota(jnp.int32, (tile, tile), 1)

    q_aug = []
    for hh in range(2):
        qT = q_ref[HEAD_DIM * hh:HEAD_DIM * (hh + 1), :]
        gate = _dot3(km_ref[hh], qT)
        sel = _top_k_mask(gate, blk_idx < q_blk, blk_idx, 0)
        bias = jnp.where(jnp.logical_or(sel, blk_idx == q_blk), 0.0, NEG)
        q_aug.append(jnp.concatenate([qT.astype(BF16), bias.astype(BF16)], axis=0))

    def scores(hh, c):
        start = pl.multiple_of(c * tile, tile)
        return _dot(k_ref[hh, pl.ds(start, tile), :], q_aug[hh])

    def values(hh, c):
        return v_ref[c, V_ROWS * hh:V_ROWS * (hh + 1), :]

    def produce(slot, c, causal):
        for hh in range(2):
            s = scores(hh, c)
            if causal:
                s = jnp.where(key_idx <= qry_idx, s, NEG)
            s_ref[slot, hh] = s
            cmax_ref[slot, hh] = jnp.broadcast_to(jnp.max(s, axis=0, keepdims=True), (8, tile))

    def consume(slot, c, carry):
        out = []
        for hh in range(2):
            m, acc = carry[2 * hh:2 * hh + 2]
            m_new = jnp.maximum(m, cmax_ref[slot, hh, 0:1, :])
            alpha = jnp.exp(m - m_new)
            p = jnp.exp(s_ref[slot, hh] - m_new).astype(BF16)
            out += [m_new, acc * alpha + _dot(values(hh, c), p)]
        return tuple(out)

    produce(0, tile_id, True)
    init = (jnp.full((1, tile), NEG, F32), jnp.zeros((V_ROWS, tile), F32))

    def pair(i, carry):
        produce(1, 2 * i, False)
        carry = consume(0, jnp.where(i == 0, tile_id, 2 * i - 1), carry)
        produce(0, 2 * i + 1, False)
        return consume(1, 2 * i, carry)

    state = lax.fori_loop(0, tile_id // 2, pair, init + init)
    last = tile_id - 1

    def odd_tail(carry):
        produce(1, last, False)
        carry = consume(0, jnp.where(last == 0, tile_id, last - 1), carry)
        return consume(1, last, carry)

    def even_tail(carry):
        return consume(0, jnp.where(tile_id == 0, tile_id, last), carry)

    state = lax.cond(tile_id % 2 == 1, odd_tail, even_tail, state)
    oT = jnp.concatenate([acc[:HEAD_DIM] / acc[HEAD_DIM:HEAD_DIM + 1] for acc in (state[1], state[3])],
                         axis=0)
    o_ref[...] = oT.T.astype(BF16)


def _prompt_attn(qT, k_aug, v_tiles, kmean_hm):
    seq = qT.shape[1]
    nt, _, tile = v_tiles.shape
    nblk = kmean_hm.shape[1]
    return pl.pallas_call(
        _attn_kernel,
        grid=(N_HEADS // 2, nt),
        in_specs=[
            pl.BlockSpec((2 * HEAD_DIM, tile), lambda p, t: (p, t)),
            pl.BlockSpec((2, seq, LANES), lambda p, t: (p, 0, 0)),
            pl.BlockSpec((nt, 2 * V_ROWS, tile), lambda p, t: (0, p, 0)),
            pl.BlockSpec((2, nblk, HEAD_DIM), lambda p, t: (p, 0, 0)),
        ],
        out_specs=pl.BlockSpec((tile, 2 * HEAD_DIM), lambda p, t: (t, p)),
        out_shape=jax.ShapeDtypeStruct((seq, ATTN_WIDTH), BF16),
        scratch_shapes=[pltpu.VMEM((2, 2, tile, tile), F32), pltpu.VMEM((2, 2, 8, tile), F32)],
        compiler_params=pltpu.CompilerParams(dimension_semantics=("arbitrary", "arbitrary"),
                                             vmem_limit_bytes=VMEM_LIMIT),
        name="prompt_attn",
    )(qT, k_aug, v_tiles, kmean_hm)


def _mlp_kernel(x_ref, attn_ref, pool_ref, wo_ref, g2_ref, wup_ref, wdown_ref, gf_ref, y_ref,
                x1_ref, hn_ref, acc_ref):
    j = pl.program_id(1)

    @pl.when(j == 0)
    def _():
        x1 = (x_ref[...] + _dot(attn_ref[...], wo_ref[0:ATTN_WIDTH, :])
              + _dot(pool_ref[...], wo_ref[ATTN_WIDTH:, :]))
        x1_ref[...] = x1
        hn_ref[...] = _rmsnorm(x1, g2_ref[...]).astype(BF16)
        acc_ref[...] = jnp.zeros(acc_ref.shape, F32)

    hid = jnp.maximum(_dot(hn_ref[...], wup_ref[...]), 0.0)
    acc_ref[...] += _dot((hid * hid).astype(BF16), wdown_ref[...])

    @pl.when(j == pl.num_programs(1) - 1)
    def _():
        y_ref[...] = _rmsnorm(x1_ref[...] + acc_ref[...], gf_ref[...])


def _out_mlp(x, attn, pool, wo, g2, wup, wdown, gf, tl, tf):
    rows = x.shape[0]
    const2 = lambda i, j: (0, 0)
    return pl.pallas_call(
        _mlp_kernel,
        grid=(rows // tl, D_FF // tf),
        in_specs=[
            pl.BlockSpec((tl, D_MODEL), lambda i, j: (i, 0)),
            pl.BlockSpec((tl, ATTN_WIDTH), lambda i, j: (i, 0)),
            pl.BlockSpec((tl, POOL_WIDTH), lambda i, j: (i, 0)),
            pl.BlockSpec((D_MODEL, D_MODEL), const2),
            pl.BlockSpec((1, D_MODEL), const2),
            pl.BlockSpec((D_MODEL, tf), lambda i, j: (0, j)),
            pl.BlockSpec((tf, D_MODEL), lambda i, j: (j, 0)),
            pl.BlockSpec((1, D_MODEL), const2),
        ],
        out_specs=pl.BlockSpec((tl, D_MODEL), lambda i, j: (i, 0)),
        out_shape=jax.ShapeDtypeStruct((rows, D_MODEL), F32),
        scratch_shapes=[pltpu.VMEM((tl, D_MODEL), F32), pltpu.VMEM((tl, D_MODEL), BF16),
                        pltpu.VMEM((tl, D_MODEL), F32)],
        compiler_params=pltpu.CompilerParams(dimension_semantics=("arbitrary", "arbitrary"),
                                             vmem_limit_bytes=VMEM_LIMIT),
        name="out_mlp",
    )(x, attn, pool, wo, g2, wup, wdown, gf)


def _sproj_kernel(x_ref, g_ref, win_ref, c_ref, s1_ref, s2_ref, state_ref, wpool_ref, pscale_ref,
                  q_ref, k_ref, v_ref, u_ref, pool_ref, ext_ref):
    rows = x_ref.shape[0]
    nseq, dec = state_ref.shape[0], rows // state_ref.shape[0]
    hb = _rmsnorm(x_ref[...], g_ref[...]).astype(BF16)
    z = _dot(hb, win_ref[...])
    for g in range(2 * ATTN_WIDTH // LANES):
        xg = z[:, LANES * g:LANES * (g + 1)]
        r = (xg * c_ref[...] + pltpu.roll(xg, ROT_HALF, 1) * s2_ref[...]
             + pltpu.roll(xg, LANES - ROT_HALF, 1) * s1_ref[...])
        if LANES * g < ATTN_WIDTH:
            q_ref[:, LANES * g:LANES * (g + 1)] = r * ATTN_SCALE
        else:
            k_ref[:, LANES * g - ATTN_WIDTH:LANES * (g + 1) - ATTN_WIDTH] = r
    v_ref[...] = z[:, 2 * ATTN_WIDTH:3 * ATTN_WIDTH]
    u = z[:, 3 * ATTN_WIDTH:]
    u_ref[...] = u

    ext_ref[:, 0:HALO, :] = state_ref[...]
    ext_ref[:, HALO:HALO + dec, :] = u.reshape(nseq, dec, POOL_WIDTH)
    for g, w in enumerate(POOL_WINDOWS):
        lanes = slice(POOL_CH * g, POOL_CH * (g + 1))
        ug = ext_ref[:, HALO:HALO + dec, lanes]
        acc = ug
        for j in range(1, w):
            acc = acc + ext_ref[:, HALO - j:HALO - j + dec, lanes]
        d = (acc / float(w) - ug).reshape(rows, POOL_CH)
        y = _dot(d.astype(BF16), wpool_ref[g]) * pscale_ref[:, lanes]
        pool_ref[:, lanes] = y.astype(BF16)


def _sample_proj(x, g1, win, c_tab, s1_tab, s2_tab, state_pad, wpool, pscale):
    rows = x.shape[0]
    nseq = state_pad.shape[0]
    shapes = [jax.ShapeDtypeStruct((rows, ATTN_WIDTH), F32)] * 3 + [
        jax.ShapeDtypeStruct((rows, POOL_WIDTH), F32), jax.ShapeDtypeStruct((rows, POOL_WIDTH), BF16)]
    return pl.pallas_call(
        _sproj_kernel,
        out_shape=shapes,
        scratch_shapes=[pltpu.VMEM((nseq, HALO + rows // nseq, POOL_WIDTH), F32)],
        compiler_params=pltpu.CompilerParams(vmem_limit_bytes=VMEM_LIMIT),
        name="sample_proj",
    )(x, g1, win, c_tab, s1_tab, s2_tab, state_pad, wpool, pscale)


GROUP = 8
N_BUF = 4 * GROUP


def _dec_kernel(pt_ref, q_ref, kn_ref, vn_ref, ck_hbm, cv_hbm, o_ref,
                kbuf, vbuf, ksem, vsem, s_scr, p_scr, bias_scr):
    b = pl.program_id(0)
    n_pages = pt_ref.shape[1]
    page = kbuf.shape[-1]
    ppb = MOBA_BLOCK // page
    nblk = n_pages // ppb
    n_groups = n_pages // GROUP
    dec = q_ref.shape[1]
    rows = N_HEADS * dec

    def slot_of(g, e):
        return (g % (N_BUF // GROUP)) * GROUP + e

    def k_copy(j, slot, seq=b):
        return pltpu.make_async_copy(ck_hbm.at[pt_ref[seq, j]], kbuf.at[slot], ksem.at[slot])

    def v_copy(j, slot):
        return pltpu.make_async_copy(cv_hbm.at[pt_ref[b, j]], vbuf.at[slot], vsem.at[slot])

    @pl.when(b == 0)
    def _():
        for s in range(N_BUF):
            k_copy(s, s).start()

    for s in range(N_BUF):
        v_copy(s, s).start()

    r_idx = lax.broadcasted_iota(jnp.int32, (rows, ATTN_WIDTH), 0)
    c_idx = lax.broadcasted_iota(jnp.int32, (rows, ATTN_WIDTH), 1)
    head_mask = (c_idx // HEAD_DIM) == (r_idx // dec)
    q_bd = jnp.where(head_mask, jnp.concatenate([q_ref[0]] * N_HEADS, axis=0), 0.0)
    q_hi, q_lo = _split_bf16(q_bd)
    lane = lax.broadcasted_iota(jnp.int32, (rows, LANES), 1)

    def refill(copy, g):
        @pl.when((g + N_BUF // GROUP) * GROUP < n_pages)
        def _():
            for e in range(GROUP):
                copy(g * GROUP + e + N_BUF, slot_of(g, e)).start()

    def k_group(g, gate):
        for e in range(GROUP):
            k_copy(g * GROUP + e, slot_of(g, e)).wait()
        for e in range(GROUP):
            j = g * GROUP + e
            kt = kbuf[slot_of(g, e)].reshape(ATTN_WIDTH, page).astype(BF16)
            s = _dot(q_hi, kt) + _dot(q_lo, kt)
            s_scr[j] = s
            blk = s if e % ppb == 0 else blk + s
            if e % ppb == ppb - 1:
                col = jnp.sum(blk, axis=1, keepdims=True)
                gate = jnp.where(lane == j // ppb, col, gate)
        refill(k_copy, g)
        return gate

    gate = lax.fori_loop(0, n_groups, k_group, jnp.zeros((rows, LANES), F32)) * (1.0 / MOBA_BLOCK)

    @pl.when(b + 1 < pl.num_programs(0))
    def _():
        for s in range(N_BUF):
            k_copy(s, s, b + 1).start()

    sel = _top_k_mask(gate, lane < nblk, lane, 1)
    bias = jnp.where(sel, 0.0, NEG)

    def bias_group(g, carry):
        for e in range(GROUP):
            n = g * GROUP + e
            col = jnp.sum(jnp.where(lane == n, bias, 0.0), axis=1, keepdims=True)
            bias_scr[n] = jnp.broadcast_to(col, (rows, LANES))
        return carry

    lax.fori_loop(0, nblk // GROUP, bias_group, 0)

    q_pos = lax.broadcasted_iota(jnp.int32, (rows, LANES), 0) % dec
    s_own = jnp.where(lane <= q_pos, _dot_nt(q_hi, kn_ref[0]), NEG)

    def max_group(g, mv):
        for e in range(GROUP):
            j = g * GROUP + e
            mv = jnp.maximum(mv, s_scr[j] + bias_scr[j // ppb])
        return mv

    m = jnp.max(lax.fori_loop(0, n_groups, max_group, s_own), axis=1, keepdims=True)
    p_own = jnp.exp(s_own - m)

    def exp_group(g, lv):
        for e in range(GROUP):
            j = g * GROUP + e
            p = jnp.exp(s_scr[j] + (bias_scr[j // ppb] - m))
            p_scr[j] = p.astype(BF16)
            lv = lv + p
        return lv

    l = jnp.sum(lax.fori_loop(0, n_groups, exp_group, p_own), axis=1, keepdims=True)

    def v_group(g, acc):
        for e in range(GROUP):
            v_copy(g * GROUP + e, slot_of(g, e)).wait()
        for e in range(GROUP):
            vt = vbuf[slot_of(g, e)].reshape(ATTN_WIDTH, page).astype(BF16)
            acc = acc + _dot_nt(p_scr[g * GROUP + e], vt)
        refill(v_copy, g)
        return acc

    acc = lax.fori_loop(0, n_groups, v_group, _dot(p_own.astype(BF16), vn_ref[0]))
    o = jnp.where(head_mask, acc / l, 0.0)
    out = o[0:dec]
    for h in range(1, N_HEADS):
        out = out + o[h * dec:(h + 1) * dec]
    o_ref[0] = out


def _sample_attn(page_table, q_s, kn_pad, vn_pad, cache_kT, cache_vT):
    nseq, n_pages = page_table.shape
    dec = q_s.shape[1]
    page = cache_kT.shape[-1]
    grid_spec = pltpu.PrefetchScalarGridSpec(
        num_scalar_prefetch=1,
        grid=(nseq,),
        in_specs=[
            pl.BlockSpec((1, dec, ATTN_WIDTH), lambda b, pt: (b, 0, 0)),
            pl.BlockSpec((1, LANES, ATTN_WIDTH), lambda b, pt: (b, 0, 0)),
            pl.BlockSpec((1, LANES, ATTN_WIDTH), lambda b, pt: (b, 0, 0)),
            pl.BlockSpec(memory_space=pl.ANY),
            pl.BlockSpec(memory_space=pl.ANY),
        ],
        out_specs=pl.BlockSpec((1, dec, ATTN_WIDTH), lambda b, pt: (b, 0, 0)),
        scratch_shapes=[
            pltpu.VMEM((N_BUF, N_HEADS, HEAD_DIM, page), F32),
            pltpu.VMEM((N_BUF, N_HEADS, HEAD_DIM, page), F32),
            pltpu.SemaphoreType.DMA((N_BUF,)),
            pltpu.SemaphoreType.DMA((N_BUF,)),
            pltpu.VMEM((n_pages, N_HEADS * dec, page), F32),
            pltpu.VMEM((n_pages, N_HEADS * dec, page), BF16),
            pltpu.VMEM((n_pages * page // MOBA_BLOCK, N_HEADS * dec, LANES), F32),
        ],
    )
    return pl.pallas_call(
        _dec_kernel,
        grid_spec=grid_spec,
        out_shape=jax.ShapeDtypeStruct((nseq, dec, ATTN_WIDTH), F32),
        compiler_params=pltpu.CompilerParams(dimension_semantics=("arbitrary",),
                                             vmem_limit_bytes=VMEM_LIMIT),
        name="sample_attn",
    )(page_table, q_s, kn_pad, vn_pad, cache_kT, cache_vT)


def _rope_tables(pos):
    inv = jnp.power(ROPE_THETA, -jnp.arange(ROT_HALF, dtype=F32) * (2.0 / ROT_DIM))
    ang = pos.astype(F32)[:, None] * inv[None, :]
    cos, sin = jnp.cos(ang), jnp.sin(ang)
    lane = jnp.arange(LANES) % HEAD_DIM
    f = lane % ROT_HALF
    c_tab = jnp.where(lane[None, :] < ROT_DIM, cos[:, f], 1.0)
    s1_tab = jnp.where(lane[None, :] < ROT_HALF, -sin[:, f], 0.0)
    s2_tab = jnp.where((lane[None, :] >= ROT_HALF) & (lane[None, :] < ROT_DIM), sin[:, f], 0.0)
    return cos.T, sin.T, c_tab, s1_tab, s2_tab


def _layer(l, xp, xs, cache_k, cache_v, state_pool, page_table, norm1_g, w_in, w_pool, pool_scale,
           w_o, norm2_g, w_up, w_down, final_g):
    seq = xp.shape[0]
    nseq, dec, _ = xs.shape
    past_len = page_table.shape[1] * cache_k.shape[2]

    g1 = norm1_g[l][None, :]
    g2 = norm2_g[l][None, :]
    win = w_in[l].astype(BF16)
    wqkvT = win[:, :3 * ATTN_WIDTH].T
    wku = jnp.concatenate([win[:, ATTN_WIDTH:2 * ATTN_WIDTH], win[:, 3 * ATTN_WIDTH:]], axis=1)
    wpool = w_pool[l].astype(BF16)
    pscale = pool_scale[l][None, :]
    wo = w_o[l].astype(BF16)
    wup = w_up[l].astype(BF16)
    wdown = w_down[l].astype(BF16)
    gf = final_g[None, :]

    cosT, sinT, c_tab, s1_tab, s2_tab = _rope_tables(jnp.arange(seq, dtype=jnp.int32))
    qT, kT, vT, v_tiles, k_aug, kmean, pool_p, u_tail = _prompt_proj(
        xp, g1, wqkvT, wku, cosT, sinT, c_tab, s1_tab, s2_tab, wpool, pscale)
    nb = seq // MOBA_BLOCK
    kmean_hm = kmean[:, 0, :].reshape(nb, N_HEADS, HEAD_DIM).transpose(1, 0, 2)
    attn_p = _prompt_attn(qT, k_aug, v_tiles, kmean_hm)
    yp = _out_mlp(xp, attn_p, pool_p, wo, g2, wup, wdown, gf, tl=512, tf=1024)
    k_prompt = kT.reshape(N_HEADS, HEAD_DIM, seq).transpose(2, 0, 1)[None]
    v_prompt = vT.reshape(N_HEADS, HEAD_DIM, seq).transpose(2, 0, 1)[None]
    pool_prompt = u_tail[HALO - POOL_STATE:][None]

    rows = nseq * dec
    pos_s = past_len + jnp.arange(dec, dtype=jnp.int32)
    _, _, c_s, s1_s, s2_s = _rope_tables(jnp.tile(pos_s, nseq))
    state = state_pool[l].astype(F32)
    state_pad = jnp.pad(state, ((0, 0), (HALO - POOL_STATE, 0), (0, 0)))
    q_s, k_s, v_s, u_s, pool_s = _sample_proj(
        xs.reshape(rows, D_MODEL), g1, win, c_s, s1_s, s2_s, state_pad, wpool, pscale)
    pad = ((0, 0), (0, LANES - dec), (0, 0))
    kn_pad = jnp.pad(k_s.reshape(nseq, dec, ATTN_WIDTH), pad).astype(BF16)
    vn_pad = jnp.pad(v_s.reshape(nseq, dec, ATTN_WIDTH), pad).astype(BF16)
    cache_kT = jnp.transpose(cache_k[l], (0, 2, 3, 1))
    cache_vT = jnp.transpose(cache_v[l], (0, 2, 3, 1))
    attn_s = _sample_attn(page_table, q_s.reshape(nseq, dec, ATTN_WIDTH), kn_pad, vn_pad,
                          cache_kT, cache_vT)
    ys = _out_mlp(xs.reshape(rows, D_MODEL), attn_s.reshape(rows, ATTN_WIDTH).astype(BF16), pool_s,
                  wo, g2, wup, wdown, gf, tl=rows, tf=1024)
    k_sample = k_s.reshape(nseq, dec, N_HEADS, HEAD_DIM)
    v_sample = v_s.reshape(nseq, dec, N_HEADS, HEAD_DIM)
    u_ext = jnp.concatenate([state, u_s.reshape(nseq, dec, POOL_WIDTH)], axis=1)
    pool_sample = u_ext[:, u_ext.shape[1] - POOL_STATE:]
    return (yp, ys.reshape(nseq, dec, D_MODEL), k_prompt, v_prompt, pool_prompt,
            k_sample, v_sample, pool_sample)


def kernel(x_prompt, x_sample, cache_k, cache_v, state_pool, page_table, norm1_g, w_in, w_pool,
           pool_scale, w_o, norm2_g, w_up, w_down, final_g):
    depth = w_in.shape[0]
    assert depth == 1 and x_prompt.shape[0] == 1, "single layer, single prompt sequence"
    (yp, ys, k_p, v_p, pool_p, k_s, v_s, pool_s) = _layer(
        0, x_prompt[0], x_sample, cache_k, cache_v, state_pool, page_table, norm1_g, w_in, w_pool,
        pool_scale, w_o, norm2_g, w_up, w_down, final_g)
    return (yp[None], ys, k_p[None], v_p[None], pool_p[None], k_s[None], v_s[None], pool_s[None])
```

```python
import functools

import jax
import jax.numpy as jnp
from jax import lax
from jax.experimental import pallas as pl
from jax.experimental.pallas import tpu as pltpu

F32 = jnp.float32
BF16 = jnp.bfloat16

D_MODEL = 1024
HEAD_DIM = 64
N_HEADS = 8
ATTN_WIDTH = N_HEADS * HEAD_DIM
POOL_WINDOWS = (2, 4, 8, 16)
POOL_CH = 128
POOL_WIDTH = POOL_CH * len(POOL_WINDOWS)
POOL_STATE = max(POOL_WINDOWS) - 1
HALO = 16
D_FF = 4 * D_MODEL
MOBA_BLOCK = 256
MOBA_TOP_K = 3
ROPE_THETA = 500000.0
ROT_DIM = HEAD_DIM // 4
ROT_HALF = ROT_DIM // 2
ATTN_SCALE = HEAD_DIM ** -0.5
RMS_EPS = 1e-6
NEG = -1e30
LANES = 128
ATTN_TILE = 512
V_ROWS = HEAD_DIM + 16
VMEM_LIMIT = 56 * 1024 * 1024

_NT = (((1,), (1,)), ((), ()))


def _dot(a, b):
    return jnp.dot(a, b, preferred_element_type=F32)


def _dot_nt(a, b):
    return lax.dot_general(a, b, _NT, preferred_element_type=F32)


def _split_bf16(x):
    hi = x.astype(BF16)
    lo = (x - hi.astype(F32)).astype(BF16)
    return hi, lo


def _dot3(a, b):
    a_hi, a_lo = _split_bf16(a)
    b_hi, b_lo = _split_bf16(b)
    return _dot(a_hi, b_hi) + _dot(a_hi, b_lo) + _dot(a_lo, b_hi)


def _rmsnorm(x, g):
    ms = jnp.mean(x * x, axis=-1, keepdims=True)
    return x * lax.rsqrt(ms + RMS_EPS) * g


def _top_k_mask(gate, valid, index, axis):
    limit = gate.shape[axis]
    g = jnp.where(valid, gate, -jnp.inf)
    sel = jnp.zeros(gate.shape, jnp.bool_)
    for _ in range(MOBA_TOP_K):
        mx = jnp.max(g, axis=axis, keepdims=True)
        first = jnp.min(jnp.where(g == mx, index, limit), axis=axis, keepdims=True)
        pick = jnp.logical_and(index == first, mx > -jnp.inf)
        sel = jnp.logical_or(sel, pick)
        g = jnp.where(pick, -jnp.inf, g)
    return sel


def _proj_kernel(x_ref, g_ref, wqkvT_ref, wu_ref, cosT_ref, sinT_ref, wpool_ref, pscale_ref,
                 qT_ref, kT_ref, vT_ref, vblk_ref, kaug_ref, kmean_ref, pool_ref, utail_ref,
                 ext_ref):
    i = pl.program_id(0)
    tl = x_ref.shape[0]
    hb = _rmsnorm(x_ref[...], g_ref[...]).astype(BF16)

    zT = _dot_nt(wqkvT_ref[...], hb)
    cosT = cosT_ref[...]
    sinT = sinT_ref[...]
    k_rows = []
    for hd in range(2 * N_HEADS):
        base = hd * HEAD_DIM
        x1 = zT[base:base + ROT_HALF]
        x2 = zT[base + ROT_HALF:base + ROT_DIM]
        rest = zT[base + ROT_DIM:base + HEAD_DIM]
        r1 = x1 * cosT - x2 * sinT
        r2 = x2 * cosT + x1 * sinT
        if hd < N_HEADS:
            qT_ref[base:base + ROT_HALF, :] = r1 * ATTN_SCALE
            qT_ref[base + ROT_HALF:base + ROT_DIM, :] = r2 * ATTN_SCALE
            qT_ref[base + ROT_DIM:base + HEAD_DIM, :] = rest * ATTN_SCALE
        else:
            k_rows += [r1, r2, rest]
    kT = jnp.concatenate(k_rows, axis=0)
    kT_ref[...] = kT
    vT = zT[2 * ATTN_WIDTH:3 * ATTN_WIDTH]
    vT_ref[...] = vT
    ones_rows = jnp.ones((V_ROWS - HEAD_DIM, tl), BF16)
    vT16 = vT.astype(BF16)
    vblk_ref[0] = jnp.concatenate(
        [piece for h in range(N_HEADS)
         for piece in (vT16[HEAD_DIM * h:HEAD_DIM * (h + 1)], ones_rows)], axis=0)

    k_rm = kT.T
    lane = lax.broadcasted_iota(jnp.int32, (tl, LANES), 1)
    block_tag = jnp.where(lane - HEAD_DIM == i, 1.0, 0.0)
    low = lane < HEAD_DIM
    for g in range(ATTN_WIDTH // LANES):
        kr = k_rm[:, LANES * g:LANES * (g + 1)]
        kaug_ref[2 * g] = jnp.where(low, kr, block_tag).astype(BF16)
        kaug_ref[2 * g + 1] = jnp.where(low, pltpu.roll(kr, HEAD_DIM, 1), block_tag).astype(BF16)
        ksum = jnp.sum(kr, axis=0, keepdims=True) * (1.0 / tl)
        kmean_ref[0, :, LANES * g:LANES * (g + 1)] = jnp.broadcast_to(ksum, (8, LANES))

    u = _dot(hb, wu_ref[...])

    @pl.when(i == 0)
    def _():
        ext_ref[0:HALO, :] = jnp.zeros((HALO, POOL_WIDTH), F32)

    ext_ref[HALO:HALO + tl, :] = u
    row = lax.broadcasted_iota(jnp.int32, (tl, 1), 0) + i * tl
    for g, w in enumerate(POOL_WINDOWS):
        lanes = slice(POOL_CH * g, POOL_CH * (g + 1))
        ug = ext_ref[HALO:HALO + tl, lanes]
        acc = ug
        for j in range(1, w):
            acc = acc + ext_ref[HALO - j:HALO - j + tl, lanes]
        cnt = jnp.minimum(row + 1, w).astype(F32)
        d = acc / cnt - ug
        y = _dot(d.astype(BF16), wpool_ref[g]) * pscale_ref[:, lanes]
        pool_ref[:, lanes] = y.astype(BF16)
    tail = ext_ref[tl:tl + HALO, :]
    ext_ref[0:HALO, :] = tail
    utail_ref[...] = tail


def _prompt_proj(x, g1, wqkvT, wu, cosT, sinT, wpool, pscale):
    seq = x.shape[0]
    tl = MOBA_BLOCK
    nb = seq // tl
    bpt = ATTN_TILE // tl
    const2 = lambda i: (0, 0)
    return pl.pallas_call(
        _proj_kernel,
        grid=(nb,),
        in_specs=[
            pl.BlockSpec((tl, D_MODEL), lambda i: (i, 0)),
            pl.BlockSpec((1, D_MODEL), const2),
            pl.BlockSpec((3 * ATTN_WIDTH, D_MODEL), const2),
            pl.BlockSpec((D_MODEL, POOL_WIDTH), const2),
            pl.BlockSpec((ROT_HALF, tl), lambda i: (0, i)),
            pl.BlockSpec((ROT_HALF, tl), lambda i: (0, i)),
            pl.BlockSpec((len(POOL_WINDOWS), POOL_CH, POOL_CH), lambda i: (0, 0, 0)),
            pl.BlockSpec((1, POOL_WIDTH), const2),
        ],
        out_specs=[
            pl.BlockSpec((ATTN_WIDTH, tl), lambda i: (0, i)),
            pl.BlockSpec((ATTN_WIDTH, tl), lambda i: (0, i)),
            pl.BlockSpec((ATTN_WIDTH, tl), lambda i: (0, i)),
            pl.BlockSpec((1, N_HEADS * V_ROWS, tl), lambda i: (i // bpt, 0, i % bpt)),
            pl.BlockSpec((N_HEADS, tl, LANES), lambda i: (0, i, 0)),
            pl.BlockSpec((1, 8, ATTN_WIDTH), lambda i: (i, 0, 0)),
            pl.BlockSpec((tl, POOL_WIDTH), lambda i: (i, 0)),
            pl.BlockSpec((HALO, POOL_WIDTH), const2),
        ],
        out_shape=[
            jax.ShapeDtypeStruct((ATTN_WIDTH, seq), F32),
            jax.ShapeDtypeStruct((ATTN_WIDTH, seq), F32),
            jax.ShapeDtypeStruct((ATTN_WIDTH, seq), F32),
            jax.ShapeDtypeStruct((nb // bpt, N_HEADS * V_ROWS, bpt * tl), BF16),
            jax.ShapeDtypeStruct((N_HEADS, seq, LANES), BF16),
            jax.ShapeDtypeStruct((nb, 8, ATTN_WIDTH), F32),
            jax.ShapeDtypeStruct((seq, POOL_WIDTH), BF16),
            jax.ShapeDtypeStruct((HALO, POOL_WIDTH), F32),
        ],
        scratch_shapes=[pltpu.VMEM((HALO + tl, POOL_WIDTH), F32)],
        compiler_params=pltpu.CompilerParams(dimension_semantics=("arbitrary",),
                                             vmem_limit_bytes=VMEM_LIMIT),
        name="prompt_proj",
    )(x, g1, wqkvT, wu, cosT, sinT, wpool, pscale)


def _attn_kernel(q_ref, k_ref, v_ref, km_ref, o_ref, s_ref, cmax_ref):
    tile_id = pl.program_id(1)
    nblk = km_ref.shape[1]
    tile = q_ref.shape[1]
    blk_idx = lax.broadcasted_iota(jnp.int32, (nblk, tile), 0)
    q_blk = (tile_id * (tile // MOBA_BLOCK)
             + lax.broadcasted_iota(jnp.int32, (nblk, tile), 1) // MOBA_BLOCK)
    key_idx = lax.broadcasted_iota(jnp.int32, (tile, tile), 0)
    qry_idx = lax.broadcasted_iota(jnp.int32, (tile, tile), 1)

    q_aug = []
    for hh in range(2):
        qT = q_ref[HEAD_DIM * hh:HEAD_DIM * (hh + 1), :]
        gate = _dot3(km_ref[hh], qT)
        sel = _top_k_mask(gate, blk_idx < q_blk, blk_idx, 0)
        bias = jnp.where(jnp.logical_or(sel, blk_idx == q_blk), 0.0, NEG)
        q_aug.append(jnp.concatenate([qT.astype(BF16), bias.astype(BF16)], axis=0))

    def scores(hh, c):
        start = pl.multiple_of(c * tile, tile)
        return _dot(k_ref[hh, pl.ds(start, tile), :], q_aug[hh])

    def values(hh, c):
        return v_ref[c, V_ROWS * hh:V_ROWS * (hh + 1), :]

    def produce(slot, c, causal):
        for hh in range(2):
            s = scores(hh, c)
            if causal:
                s = jnp.where(key_idx <= qry_idx, s, NEG)
            s_ref[slot, hh] = s
            cmax_ref[slot, hh] = jnp.broadcast_to(jnp.max(s, axis=0, keepdims=True), (8, tile))

    def consume(slot, c, carry):
        out = []
        for hh in range(2):
            m, acc = carry[2 * hh:2 * hh + 2]
            m_new = jnp.maximum(m, cmax_ref[slot, hh, 0:1, :])
            alpha = jnp.exp(m - m_new)
            p = jnp.exp(s_ref[slot, hh] - m_new).astype(BF16)
            out += [m_new, acc * alpha + _dot(values(hh, c), p)]
        return tuple(out)

    produce(0, tile_id, True)
    init = (jnp.full((1, tile), NEG, F32), jnp.zeros((V_ROWS, tile), F32))

    def pair(i, carry):
        produce(1, 2 * i, False)
        carry = consume(0, jnp.where(i == 0, tile_id, 2 * i - 1), carry)
        produce(0, 2 * i + 1, False)
        return consume(1, 2 * i, carry)

    state = lax.fori_loop(0, tile_id // 2, pair, init + init)
    last = tile_id - 1

    def odd_tail(carry):
        produce(1, last, False)
        carry = consume(0, jnp.where(last == 0, tile_id, last - 1), carry)
        return consume(1, last, carry)

    def even_tail(carry):
        return consume(0, jnp.where(tile_id == 0, tile_id, last), carry)

    state = lax.cond(tile_id % 2 == 1, odd_tail, even_tail, state)
    oT = jnp.concatenate([acc[:HEAD_DIM] / acc[HEAD_DIM:HEAD_DIM + 1] for acc in (state[1], state[3])],
                         axis=0)
    o_ref[...] = oT.T.astype(BF16)


def _prompt_attn(qT, k_aug, v_tiles, kmean_hm):
    seq = qT.shape[1]
    nt, _, tile = v_tiles.shape
    nblk = kmean_hm.shape[1]
    return pl.pallas_call(
        _attn_kernel,
        grid=(N_HEADS // 2, nt),
        in_specs=[
            pl.BlockSpec((2 * HEAD_DIM, tile), lambda p, t: (p, t)),
            pl.BlockSpec((2, seq, LANES), lambda p, t: (p, 0, 0)),
            pl.BlockSpec((nt, 2 * V_ROWS, tile), lambda p, t: (0, p, 0)),
            pl.BlockSpec((2, nblk, HEAD_DIM), lambda p, t: (p, 0, 0)),
        ],
        out_specs=pl.BlockSpec((tile, 2 * HEAD_DIM), lambda p, t: (t, p)),
        out_shape=jax.ShapeDtypeStruct((seq, ATTN_WIDTH), BF16),
        scratch_shapes=[pltpu.VMEM((2, 2, tile, tile), F32), pltpu.VMEM((2, 2, 8, tile), F32)],
        compiler_params=pltpu.CompilerParams(dimension_semantics=("arbitrary", "arbitrary"),
                                             vmem_limit_bytes=VMEM_LIMIT),
        name="prompt_attn",
    )(qT, k_aug, v_tiles, kmean_hm)


def _mlp_kernel(x_ref, attn_ref, pool_ref, wo_ref, g2_ref, wup_ref, wdown_ref, gf_ref, y_ref,
                x1_ref, hn_ref, acc_ref):
    j = pl.program_id(1)

    @pl.when(j == 0)
    def _():
        x1 = (x_ref[...] + _dot(attn_ref[...], wo_ref[0:ATTN_WIDTH, :])
              + _dot(pool_ref[...], wo_ref[ATTN_WIDTH:, :]))
        x1_ref[...] = x1
        hn_ref[...] = _rmsnorm(x1, g2_ref[...]).astype(BF16)
        acc_ref[...] = jnp.zeros(acc_ref.shape, F32)

    hid = jnp.maximum(_dot(hn_ref[...], wup_ref[...]), 0.0)
    acc_ref[...] += _dot((hid * hid).astype(BF16), wdown_ref[...])

    @pl.when(j == pl.num_programs(1) - 1)
    def _():
        y_ref[...] = _rmsnorm(x1_ref[...] + acc_ref[...], gf_ref[...])


def _out_mlp(x, attn, pool, wo, g2, wup, wdown, gf, tl, tf):
    rows = x.shape[0]
    const2 = lambda i, j: (0, 0)
    return pl.pallas_call(
        _mlp_kernel,
        grid=(rows // tl, D_FF // tf),
        in_specs=[
            pl.BlockSpec((tl, D_MODEL), lambda i, j: (i, 0)),
            pl.BlockSpec((tl, ATTN_WIDTH), lambda i, j: (i, 0)),
            pl.BlockSpec((tl, POOL_WIDTH), lambda i, j: (i, 0)),
            pl.BlockSpec((D_MODEL, D_MODEL), const2),
            pl.BlockSpec((1, D_MODEL), const2),
            pl.BlockSpec((D_MODEL, tf), lambda i, j: (0, j)),
            pl.BlockSpec((tf, D_MODEL), lambda i, j: (j, 0)),
            pl.BlockSpec((1, D_MODEL), const2),
        ],
        out_specs=pl.BlockSpec((tl, D_MODEL), lambda i, j: (i, 0)),
        out_shape=jax.ShapeDtypeStruct((rows, D_MODEL), F32),
        scratch_shapes=[pltpu.VMEM((tl, D_MODEL), F32), pltpu.VMEM((tl, D_MODEL), BF16),
                        pltpu.VMEM((tl, D_MODEL), F32)],
        compiler_params=pltpu.CompilerParams(dimension_semantics=("arbitrary", "arbitrary"),
                                             vmem_limit_bytes=VMEM_LIMIT),
        name="out_mlp",
    )(x, attn, pool, wo, g2, wup, wdown, gf)


def _sproj_kernel(x_ref, g_ref, win_ref, c_ref, s1_ref, s2_ref, state_ref, wpool_ref, pscale_ref,
                  q_ref, k_ref, v_ref, u_ref, pool_ref, ext_ref):
    rows = x_ref.shape[0]
    nseq, dec = state_ref.shape[0], rows // state_ref.shape[0]
    hb = _rmsnorm(x_ref[...], g_ref[...]).astype(BF16)
    z = _dot(hb, win_ref[...])
    for g in range(2 * ATTN_WIDTH // LANES):
        xg = z[:, LANES * g:LANES * (g + 1)]
        r = (xg * c_ref[...] + pltpu.roll(xg, ROT_HALF, 1) * s2_ref[...]
             + pltpu.roll(xg, LANES - ROT_HALF, 1) * s1_ref[...])
        if LANES * g < ATTN_WIDTH:
            q_ref[:, LANES * g:LANES * (g + 1)] = r * ATTN_SCALE
        else:
            k_ref[:, LANES * g - ATTN_WIDTH:LANES * (g + 1) - ATTN_WIDTH] = r
    v_ref[...] = z[:, 2 * ATTN_WIDTH:3 * ATTN_WIDTH]
    u = z[:, 3 * ATTN_WIDTH:]
    u_ref[...] = u

    ext_ref[:, 0:HALO, :] = state_ref[...]
    ext_ref[:, HALO:HALO + dec, :] = u.reshape(nseq, dec, POOL_WIDTH)
    for g, w in enumerate(POOL_WINDOWS):
        lanes = slice(POOL_CH * g, POOL_CH * (g + 1))
        ug = ext_ref[:, HALO:HALO + dec, lanes]
        acc = ug
        for j in range(1, w):
            acc = acc + ext_ref[:, HALO - j:HALO - j + dec, lanes]
        d = (acc / float(w) - ug).reshape(rows, POOL_CH)
        y = _dot(d.astype(BF16), wpool_ref[g]) * pscale_ref[:, lanes]
        pool_ref[:, lanes] = y.astype(BF16)


def _sample_proj(x, g1, win, c_tab, s1_tab, s2_tab, state_pad, wpool, pscale):
    rows = x.shape[0]
    nseq = state_pad.shape[0]
    shapes = [jax.ShapeDtypeStruct((rows, ATTN_WIDTH), F32)] * 3 + [
        jax.ShapeDtypeStruct((rows, POOL_WIDTH), F32), jax.ShapeDtypeStruct((rows, POOL_WIDTH), BF16)]
    return pl.pallas_call(
        _sproj_kernel,
        out_shape=shapes,
        scratch_shapes=[pltpu.VMEM((nseq, HALO + rows // nseq, POOL_WIDTH), F32)],
        compiler_params=pltpu.CompilerParams(vmem_limit_bytes=VMEM_LIMIT),
        name="sample_proj",
    )(x, g1, win, c_tab, s1_tab, s2_tab, state_pad, wpool, pscale)


GROUP = 8
K_BUF = 4 * GROUP
V_BUF = 8 * GROUP


def _dec_kernel(pt_ref, q_ref, kn_ref, vn_ref, ck_hbm, cv_hbm, o_ref,
                kbuf, vbuf, ksem, vsem, s_scr, p_scr, bias_scr):
    b = pl.program_id(0)
    n_pages = pt_ref.shape[1]
    page = kbuf.shape[-1]
    ppb = MOBA_BLOCK // page
    nblk = n_pages // ppb
    n_groups = n_pages // GROUP
    dec = q_ref.shape[1]
    rows = N_HEADS * dec

    def slot_of(g, e, n_buf):
        return (g % (n_buf // GROUP)) * GROUP + e

    def k_copy(j, slot, seq=b):
        return pltpu.make_async_copy(ck_hbm.at[pt_ref[seq, j]], kbuf.at[slot], ksem.at[slot])

    def v_copy(j, slot):
        return pltpu.make_async_copy(cv_hbm.at[pt_ref[b, j]], vbuf.at[slot], vsem.at[slot])

    @pl.when(b == 0)
    def _():
        for s in range(K_BUF):
            k_copy(s, s).start()

    for s in range(V_BUF):
        v_copy(s, s).start()

    r_idx = lax.broadcasted_iota(jnp.int32, (rows, ATTN_WIDTH), 0)
    c_idx = lax.broadcasted_iota(jnp.int32, (rows, ATTN_WIDTH), 1)
    head_mask = (c_idx // HEAD_DIM) == (r_idx // dec)
    q_bd = jnp.where(head_mask, jnp.concatenate([q_ref[0]] * N_HEADS, axis=0), 0.0)
    q_hi, q_lo = _split_bf16(q_bd)
    q_hilo = jnp.concatenate([q_hi, q_lo], axis=0)
    lane = lax.broadcasted_iota(jnp.int32, (rows, LANES), 1)

    def refill(copy, g, n_buf):
        @pl.when((g + n_buf // GROUP) * GROUP < n_pages)
        def _():
            for e in range(GROUP):
                copy(g * GROUP + e + n_buf, slot_of(g, e, n_buf)).start()

    def block_of(buf, g, e, n_buf):
        return jnp.concatenate(
            [buf[slot_of(g, e + i, n_buf)].reshape(ATTN_WIDTH, page).astype(BF16) for i in range(ppb)],
            axis=1)

    def k_group(g, gate):
        for e in range(GROUP):
            k_copy(g * GROUP + e, slot_of(g, e, K_BUF)).wait()
        for e in range(0, GROUP, ppb):
            n = (g * GROUP + e) // ppb
            kt = block_of(kbuf, g, e, K_BUF)
            s2 = _dot(q_hilo, kt)
            s = s2[:rows] + s2[rows:]
            s_scr[n] = s
            col = jnp.sum(s, axis=1, keepdims=True)
            gate = jnp.where(lane == n, col, gate)
        refill(k_copy, g, K_BUF)
        return gate

    gate = lax.fori_loop(0, n_groups, k_group, jnp.zeros((rows, LANES), F32)) * (1.0 / MOBA_BLOCK)

    @pl.when(b + 1 < pl.num_programs(0))
    def _():
        for s in range(K_BUF):
            k_copy(s, s, b + 1).start()

    sel = _top_k_mask(gate, lane < nblk, lane, 1)
    bias = jnp.where(sel, 0.0, NEG)

    def bias_group(g, carry):
        for e in range(GROUP):
            n = g * GROUP + e
            col = jnp.sum(jnp.where(lane == n, bias, 0.0), axis=1, keepdims=True)
            bias_scr[n] = jnp.broadcast_to(col, (rows, MOBA_BLOCK))
        return carry

    lax.fori_loop(0, nblk // GROUP, bias_group, 0)

    lane_b = lax.broadcasted_iota(jnp.int32, (rows, MOBA_BLOCK), 1)
    q_pos = lax.broadcasted_iota(jnp.int32, (rows, MOBA_BLOCK), 0) % dec
    s_own = jnp.where(lane_b <= q_pos, _dot_nt(q_hi, kn_ref[0]), NEG)

    def max_group(g, mv):
        for e in range(GROUP):
            n = g * GROUP + e
            mv = jnp.maximum(mv, s_scr[n] + bias_scr[n])
        return mv

    m = jnp.max(lax.fori_loop(0, nblk // GROUP, max_group, s_own), axis=1, keepdims=True)
    p_own = jnp.exp(s_own - m)

    def exp_group(g, lv):
        for e in range(GROUP):
            n = g * GROUP + e
            p = jnp.exp(s_scr[n] + (bias_scr[n] - m))
            p_scr[n] = p.astype(BF16)
            lv = lv + p
        return lv

    l = jnp.sum(lax.fori_loop(0, nblk // GROUP, exp_group, p_own), axis=1, keepdims=True)

    def v_group(g, acc):
        for e in range(GROUP):
            v_copy(g * GROUP + e, slot_of(g, e, V_BUF)).wait()
        for e in range(0, GROUP, ppb):
            acc = acc + _dot_nt(p_scr[(g * GROUP + e) // ppb], block_of(vbuf, g, e, V_BUF))
        refill(v_copy, g, V_BUF)
        return acc

    acc = lax.fori_loop(0, n_groups, v_group, _dot(p_own.astype(BF16), vn_ref[0]))
    o = jnp.where(head_mask, acc / l, 0.0)
    out = o[0:dec]
    for h in range(1, N_HEADS):
        out = out + o[h * dec:(h + 1) * dec]
    o_ref[0] = out


def _sample_attn(page_table, q_s, kn_pad, vn_pad, cache_kT, cache_vT):
    nseq, n_pages = page_table.shape
    dec = q_s.shape[1]
    page = cache_kT.shape[-1]
    nblk = n_pages * page // MOBA_BLOCK
    grid_spec = pltpu.PrefetchScalarGridSpec(
        num_scalar_prefetch=1,
        grid=(nseq,),
        in_specs=[
            pl.BlockSpec((1, dec, ATTN_WIDTH), lambda b, pt: (b, 0, 0)),
            pl.BlockSpec((1, MOBA_BLOCK, ATTN_WIDTH), lambda b, pt: (b, 0, 0)),
            pl.BlockSpec((1, MOBA_BLOCK, ATTN_WIDTH), lambda b, pt: (b, 0, 0)),
            pl.BlockSpec(memory_space=pl.ANY),
            pl.BlockSpec(memory_space=pl.ANY),
        ],
        out_specs=pl.BlockSpec((1, dec, ATTN_WIDTH), lambda b, pt: (b, 0, 0)),
        scratch_shapes=[
            pltpu.VMEM((K_BUF, N_HEADS, HEAD_DIM, page), F32),
            pltpu.VMEM((V_BUF, N_HEADS, HEAD_DIM, page), F32),
            pltpu.SemaphoreType.DMA((K_BUF,)),
            pltpu.SemaphoreType.DMA((V_BUF,)),
            pltpu.VMEM((nblk, N_HEADS * dec, MOBA_BLOCK), F32),
            pltpu.VMEM((nblk, N_HEADS * dec, MOBA_BLOCK), BF16),
            pltpu.VMEM((nblk, N_HEADS * dec, MOBA_BLOCK), F32),
        ],
    )
    return pl.pallas_call(
        _dec_kernel,
        grid_spec=grid_spec,
        out_shape=jax.ShapeDtypeStruct((nseq, dec, ATTN_WIDTH), F32),
        compiler_params=pltpu.CompilerParams(dimension_semantics=("arbitrary",),
                                             vmem_limit_bytes=VMEM_LIMIT),
        name="sample_attn",
    )(page_table, q_s, kn_pad, vn_pad, cache_kT, cache_vT)


def _rope_angles(pos):
    inv = jnp.power(ROPE_THETA, -jnp.arange(ROT_HALF, dtype=F32) * (2.0 / ROT_DIM))
    ang = pos.astype(F32)[:, None] * inv[None, :]
    return jnp.cos(ang), jnp.sin(ang)


def _rope_tables(pos):
    cos, sin = _rope_angles(pos)
    lane = jnp.arange(LANES) % HEAD_DIM
    f = lane % ROT_HALF
    c_tab = jnp.where(lane[None, :] < ROT_DIM, cos[:, f], 1.0)
    s1_tab = jnp.where(lane[None, :] < ROT_HALF, -sin[:, f], 0.0)
    s2_tab = jnp.where((lane[None, :] >= ROT_HALF) & (lane[None, :] < ROT_DIM), sin[:, f], 0.0)
    return c_tab, s1_tab, s2_tab


def _layer(l, xp, xs, cache_k, cache_v, state_pool, page_table, norm1_g, w_in, w_pool, pool_scale,
           w_o, norm2_g, w_up, w_down, final_g):
    seq = xp.shape[0]
    nseq, dec, _ = xs.shape
    past_len = page_table.shape[1] * cache_k.shape[2]

    g1 = norm1_g[l][None, :]
    g2 = norm2_g[l][None, :]
    win = w_in[l].astype(BF16)
    wqkvT = win[:, :3 * ATTN_WIDTH].T
    wu = win[:, 3 * ATTN_WIDTH:]
    wpool = w_pool[l].astype(BF16)
    pscale = pool_scale[l][None, :]
    wo = w_o[l].astype(BF16)
    wup = w_up[l].astype(BF16)
    wdown = w_down[l].astype(BF16)
    gf = final_g[None, :]

    cos, sin = _rope_angles(jnp.arange(seq, dtype=jnp.int32))
    qT, kT, vT, v_tiles, k_aug, kmean, pool_p, u_tail = _prompt_proj(
        xp, g1, wqkvT, wu, cos.T, sin.T, wpool, pscale)
    nb = seq // MOBA_BLOCK
    kmean_hm = kmean[:, 0, :].reshape(nb, N_HEADS, HEAD_DIM).transpose(1, 0, 2)
    attn_p = _prompt_attn(qT, k_aug, v_tiles, kmean_hm)
    yp = _out_mlp(xp, attn_p, pool_p, wo, g2, wup, wdown, gf, tl=512, tf=1024)
    k_prompt = kT.reshape(N_HEADS, HEAD_DIM, seq).transpose(2, 0, 1)[None]
    v_prompt = vT.reshape(N_HEADS, HEAD_DIM, seq).transpose(2, 0, 1)[None]
    pool_prompt = u_tail[HALO - POOL_STATE:][None]

    rows = nseq * dec
    pos_s = past_len + jnp.arange(dec, dtype=jnp.int32)
    c_s, s1_s, s2_s = _rope_tables(jnp.tile(pos_s, nseq))
    state = state_pool[l].astype(F32)
    state_pad = jnp.pad(state, ((0, 0), (HALO - POOL_STATE, 0), (0, 0)))
    q_s, k_s, v_s, u_s, pool_s = _sample_proj(
        xs.reshape(rows, D_MODEL), g1, win, c_s, s1_s, s2_s, state_pad, wpool, pscale)
    pad = ((0, 0), (0, MOBA_BLOCK - dec), (0, 0))
    kn_pad = jnp.pad(k_s.reshape(nseq, dec, ATTN_WIDTH), pad).astype(BF16)
    vn_pad = jnp.pad(v_s.reshape(nseq, dec, ATTN_WIDTH), pad).astype(BF16)
    cache_kT = jnp.transpose(cache_k[l], (0, 2, 3, 1))
    cache_vT = jnp.transpose(cache_v[l], (0, 2, 3, 1))
    attn_s = _sample_attn(page_table, q_s.reshape(nseq, dec, ATTN_WIDTH), kn_pad, vn_pad,
                          cache_kT, cache_vT)
    ys = _out_mlp(xs.reshape(rows, D_MODEL), attn_s.reshape(rows, ATTN_WIDTH).astype(BF16), pool_s,
                  wo, g2, wup, wdown, gf, tl=rows, tf=1024)
    k_sample = k_s.reshape(nseq, dec, N_HEADS, HEAD_DIM)
    v_sample = v_s.reshape(nseq, dec, N_HEADS, HEAD_DIM)
    u_ext = jnp.concatenate([state, u_s.reshape(nseq, dec, POOL_WIDTH)], axis=1)
    pool_sample = u_ext[:, u_ext.shape[1] - POOL_STATE:]
    return (yp, ys.reshape(nseq, dec, D_MODEL), k_prompt, v_prompt, pool_prompt,
            k_sample, v_sample, pool_sample)


def kernel(x_prompt, x_sample, cache_k, cache_v, state_pool, page_table, norm1_g, w_in, w_pool,
           pool_scale, w_o, norm2_g, w_up, w_down, final_g):
    depth = w_in.shape[0]
    assert depth == 1 and x_prompt.shape[0] == 1, "single layer, single prompt sequence"
    (yp, ys, k_p, v_p, pool_p, k_s, v_s, pool_s) = _layer(
        0, x_prompt[0], x_sample, cache_k, cache_v, state_pool, page_table, norm1_g, w_in, w_pool,
        pool_scale, w_o, norm2_g, w_up, w_down, final_g)
    return (yp[None], ys, k_p[None], v_p[None], pool_p[None], k_s[None], v_s[None], pool_s[None])
```

```python
import functools

import jax
import jax.numpy as jnp
from jax import lax
from jax.experimental import pallas as pl
from jax.experimental.pallas import tpu as pltpu

F32 = jnp.float32
BF16 = jnp.bfloat16

D_MODEL = 1024
HEAD_DIM = 64
N_HEADS = 8
ATTN_WIDTH = N_HEADS * HEAD_DIM
POOL_WINDOWS = (2, 4, 8, 16)
POOL_CH = 128
POOL_WIDTH = POOL_CH * len(POOL_WINDOWS)
POOL_STATE = max(POOL_WINDOWS) - 1
HALO = 16
D_FF = 4 * D_MODEL
MOBA_BLOCK = 256
MOBA_TOP_K = 3
ROPE_THETA = 500000.0
ROT_DIM = HEAD_DIM // 4
ROT_HALF = ROT_DIM // 2
ATTN_SCALE = HEAD_DIM ** -0.5
RMS_EPS = 1e-6
NEG = -1e30
LANES = 128
ATTN_TILE = 512
V_ROWS = HEAD_DIM + 16
VMEM_LIMIT = 56 * 1024 * 1024

_NT = (((1,), (1,)), ((), ()))


def _dot(a, b):
    return jnp.dot(a, b, preferred_element_type=F32)


def _dot_nt(a, b):
    return lax.dot_general(a, b, _NT, preferred_element_type=F32)


def _split_bf16(x):
    hi = x.astype(BF16)
    lo = (x - hi.astype(F32)).astype(BF16)
    return hi, lo


def _dot3(a, b):
    a_hi, a_lo = _split_bf16(a)
    b_hi, b_lo = _split_bf16(b)
    return _dot(a_hi, b_hi) + _dot(a_hi, b_lo) + _dot(a_lo, b_hi)


def _rmsnorm(x, g):
    ms = jnp.mean(x * x, axis=-1, keepdims=True)
    return x * lax.rsqrt(ms + RMS_EPS) * g


def _top_k_mask(gate, valid, index, axis):
    limit = gate.shape[axis]
    g = jnp.where(valid, gate, -jnp.inf)
    sel = jnp.zeros(gate.shape, jnp.bool_)
    for _ in range(MOBA_TOP_K):
        mx = jnp.max(g, axis=axis, keepdims=True)
        first = jnp.min(jnp.where(g == mx, index, limit), axis=axis, keepdims=True)
        pick = jnp.logical_and(index == first, mx > -jnp.inf)
        sel = jnp.logical_or(sel, pick)
        g = jnp.where(pick, -jnp.inf, g)
    return sel


def _proj_kernel(x_ref, g_ref, wqkvT_ref, wu_ref, cosT_ref, sinT_ref, wpool_ref, pscale_ref,
                 qT_ref, kT_ref, vT_ref, vblk_ref, kaug_ref, kmean_ref, pool_ref, utail_ref,
                 ext_ref):
    i = pl.program_id(0)
    tl = x_ref.shape[0]
    hb = _rmsnorm(x_ref[...], g_ref[...]).astype(BF16)

    zT = _dot_nt(wqkvT_ref[...], hb)
    cosT = cosT_ref[...]
    sinT = sinT_ref[...]
    k_rows = []
    for hd in range(2 * N_HEADS):
        base = hd * HEAD_DIM
        x1 = zT[base:base + ROT_HALF]
        x2 = zT[base + ROT_HALF:base + ROT_DIM]
        rest = zT[base + ROT_DIM:base + HEAD_DIM]
        r1 = x1 * cosT - x2 * sinT
        r2 = x2 * cosT + x1 * sinT
        if hd < N_HEADS:
            qT_ref[base:base + ROT_HALF, :] = r1 * ATTN_SCALE
            qT_ref[base + ROT_HALF:base + ROT_DIM, :] = r2 * ATTN_SCALE
            qT_ref[base + ROT_DIM:base + HEAD_DIM, :] = rest * ATTN_SCALE
        else:
            k_rows += [r1, r2, rest]
    kT = jnp.concatenate(k_rows, axis=0)
    kT_ref[...] = kT
    vT = zT[2 * ATTN_WIDTH:3 * ATTN_WIDTH]
    vT_ref[...] = vT
    ones_rows = jnp.ones((V_ROWS - HEAD_DIM, tl), BF16)
    vT16 = vT.astype(BF16)
    vblk_ref[0] = jnp.concatenate(
        [piece for h in range(N_HEADS)
         for piece in (vT16[HEAD_DIM * h:HEAD_DIM * (h + 1)], ones_rows)], axis=0)

    k_rm = kT.T
    lane = lax.broadcasted_iota(jnp.int32, (tl, LANES), 1)
    block_tag = jnp.where(lane - HEAD_DIM == i, 1.0, 0.0)
    low = lane < HEAD_DIM
    for g in range(ATTN_WIDTH // LANES):
        kr = k_rm[:, LANES * g:LANES * (g + 1)]
        kaug_ref[2 * g] = jnp.where(low, kr, block_tag).astype(BF16)
        kaug_ref[2 * g + 1] = jnp.where(low, pltpu.roll(kr, HEAD_DIM, 1), block_tag).astype(BF16)
        ksum = jnp.sum(kr, axis=0, keepdims=True) * (1.0 / tl)
        kmean_ref[0, :, LANES * g:LANES * (g + 1)] = jnp.broadcast_to(ksum, (8, LANES))

    u = _dot(hb, wu_ref[...])

    @pl.when(i == 0)
    def _():
        ext_ref[0:HALO, :] = jnp.zeros((HALO, POOL_WIDTH), F32)

    ext_ref[HALO:HALO + tl, :] = u
    row = lax.broadcasted_iota(jnp.int32, (tl, 1), 0) + i * tl
    for g, w in enumerate(POOL_WINDOWS):
        lanes = slice(POOL_CH * g, POOL_CH * (g + 1))
        ug = ext_ref[HALO:HALO + tl, lanes]
        acc = ug
        for j in range(1, w):
            acc = acc + ext_ref[HALO - j:HALO - j + tl, lanes]
        cnt = jnp.minimum(row + 1, w).astype(F32)
        d = acc / cnt - ug
        y = _dot(d.astype(BF16), wpool_ref[g]) * pscale_ref[:, lanes]
        pool_ref[:, lanes] = y.astype(BF16)
    tail = ext_ref[tl:tl + HALO, :]
    ext_ref[0:HALO, :] = tail
    utail_ref[...] = tail


def _prompt_proj(x, g1, wqkvT, wu, cosT, sinT, wpool, pscale):
    seq = x.shape[0]
    tl = MOBA_BLOCK
    nb = seq // tl
    bpt = ATTN_TILE // tl
    const2 = lambda i: (0, 0)
    return pl.pallas_call(
        _proj_kernel,
        grid=(nb,),
        in_specs=[
            pl.BlockSpec((tl, D_MODEL), lambda i: (i, 0)),
            pl.BlockSpec((1, D_MODEL), const2),
            pl.BlockSpec((3 * ATTN_WIDTH, D_MODEL), const2),
            pl.BlockSpec((D_MODEL, POOL_WIDTH), const2),
            pl.BlockSpec((ROT_HALF, tl), lambda i: (0, i)),
            pl.BlockSpec((ROT_HALF, tl), lambda i: (0, i)),
            pl.BlockSpec((len(POOL_WINDOWS), POOL_CH, POOL_CH), lambda i: (0, 0, 0)),
            pl.BlockSpec((1, POOL_WIDTH), const2),
        ],
        out_specs=[
            pl.BlockSpec((ATTN_WIDTH, tl), lambda i: (0, i)),
            pl.BlockSpec((ATTN_WIDTH, tl), lambda i: (0, i)),
            pl.BlockSpec((ATTN_WIDTH, tl), lambda i: (0, i)),
            pl.BlockSpec((1, N_HEADS * V_ROWS, tl), lambda i: (i // bpt, 0, i % bpt)),
            pl.BlockSpec((N_HEADS, tl, LANES), lambda i: (0, i, 0)),
            pl.BlockSpec((1, 8, ATTN_WIDTH), lambda i: (i, 0, 0)),
            pl.BlockSpec((tl, POOL_WIDTH), lambda i: (i, 0)),
            pl.BlockSpec((HALO, POOL_WIDTH), const2),
        ],
        out_shape=[
            jax.ShapeDtypeStruct((ATTN_WIDTH, seq), F32),
            jax.ShapeDtypeStruct((ATTN_WIDTH, seq), F32),
            jax.ShapeDtypeStruct((ATTN_WIDTH, seq), F32),
            jax.ShapeDtypeStruct((nb // bpt, N_HEADS * V_ROWS, bpt * tl), BF16),
            jax.ShapeDtypeStruct((N_HEADS, seq, LANES), BF16),
            jax.ShapeDtypeStruct((nb, 8, ATTN_WIDTH), F32),
            jax.ShapeDtypeStruct((seq, POOL_WIDTH), BF16),
            jax.ShapeDtypeStruct((HALO, POOL_WIDTH), F32),
        ],
        scratch_shapes=[pltpu.VMEM((HALO + tl, POOL_WIDTH), F32)],
        compiler_params=pltpu.CompilerParams(dimension_semantics=("arbitrary",),
                                             vmem_limit_bytes=VMEM_LIMIT),
        name="prompt_proj",
    )(x, g1, wqkvT, wu, cosT, sinT, wpool, pscale)


def _attn_kernel(q_ref, k_ref, v_ref, km_ref, o_ref, s_ref, cmax_ref):
    tile_id = pl.program_id(1)
    nblk = km_ref.shape[1]
    tile = q_ref.shape[1]
    blk_idx = lax.broadcasted_iota(jnp.int32, (nblk, tile), 0)
    q_blk = (tile_id * (tile // MOBA_BLOCK)
             + lax.broadcasted_iota(jnp.int32, (nblk, tile), 1) // MOBA_BLOCK)
    key_idx = lax.broadcasted_iota(jnp.int32, (tile, tile), 0)
    qry_idx = lax.broadcasted_iota(jnp.int32, (tile, tile), 1)

    q_aug = []
    for hh in range(2):
        qT = q_ref[HEAD_DIM * hh:HEAD_DIM * (hh + 1), :]
        gate = _dot3(km_ref[hh], qT)
        sel = _top_k_mask(gate, blk_idx < q_blk, blk_idx, 0)
        bias = jnp.where(jnp.logical_or(sel, blk_idx == q_blk), 0.0, NEG)
        q_aug.append(jnp.concatenate([qT.astype(BF16), bias.astype(BF16)], axis=0))

    def scores(hh, c):
        start = pl.multiple_of(c * tile, tile)
        return _dot(k_ref[hh, pl.ds(start, tile), :], q_aug[hh])

    def values(hh, c):
        return v_ref[c, V_ROWS * hh:V_ROWS * (hh + 1), :]

    def produce(slot, c, causal):
        for hh in range(2):
            s = scores(hh, c)
            if causal:
                s = jnp.where(key_idx <= qry_idx, s, NEG)
            s_ref[slot, hh] = s
            cmax_ref[slot, hh] = jnp.broadcast_to(jnp.max(s, axis=0, keepdims=True), (8, tile))

    def consume(slot, c, carry):
        out = []
        for hh in range(2):
            m, acc = carry[2 * hh:2 * hh + 2]
            m_new = jnp.maximum(m, cmax_ref[slot, hh, 0:1, :])
            alpha = jnp.exp(m - m_new)
            p = jnp.exp(s_ref[slot, hh] - m_new).astype(BF16)
            out += [m_new, acc * alpha + _dot(values(hh, c), p)]
        return tuple(out)

    produce(0, tile_id, True)
    init = (jnp.full((1, tile), NEG, F32), jnp.zeros((V_ROWS, tile), F32))

    def pair(i, carry):
        produce(1, 2 * i, False)
        carry = consume(0, jnp.where(i == 0, tile_id, 2 * i - 1), carry)
        produce(0, 2 * i + 1, False)
        return consume(1, 2 * i, carry)

    state = lax.fori_loop(0, tile_id // 2, pair, init + init)
    last = tile_id - 1

    def odd_tail(carry):
        produce(1, last, False)
        carry = consume(0, jnp.where(last == 0, tile_id, last - 1), carry)
        return consume(1, last, carry)

    def even_tail(carry):
        return consume(0, jnp.where(tile_id == 0, tile_id, last), carry)

    state = lax.cond(tile_id % 2 == 1, odd_tail, even_tail, state)
    oT = jnp.concatenate([acc[:HEAD_DIM] / acc[HEAD_DIM:HEAD_DIM + 1] for acc in (state[1], state[3])],
                         axis=0)
    o_ref[...] = oT.T.astype(BF16)


def _prompt_attn(qT, k_aug, v_tiles, kmean_hm):
    seq = qT.shape[1]
    nt, _, tile = v_tiles.shape
    nblk = kmean_hm.shape[1]
    return pl.pallas_call(
        _attn_kernel,
        grid=(N_HEADS // 2, nt),
        in_specs=[
            pl.BlockSpec((2 * HEAD_DIM, tile), lambda p, t: (p, t)),
            pl.BlockSpec((2, seq, LANES), lambda p, t: (p, 0, 0)),
            pl.BlockSpec((nt, 2 * V_ROWS, tile), lambda p, t: (0, p, 0)),
            pl.BlockSpec((2, nblk, HEAD_DIM), lambda p, t: (p, 0, 0)),
        ],
        out_specs=pl.BlockSpec((tile, 2 * HEAD_DIM), lambda p, t: (t, p)),
        out_shape=jax.ShapeDtypeStruct((seq, ATTN_WIDTH), BF16),
        scratch_shapes=[pltpu.VMEM((2, 2, tile, tile), F32), pltpu.VMEM((2, 2, 8, tile), F32)],
        compiler_params=pltpu.CompilerParams(dimension_semantics=("arbitrary", "arbitrary"),
                                             vmem_limit_bytes=VMEM_LIMIT),
        name="prompt_attn",
    )(qT, k_aug, v_tiles, kmean_hm)


def _mlp_kernel(x_ref, attn_ref, pool_ref, wo_ref, g2_ref, wup_ref, wdown_ref, gf_ref, y_ref,
                x1_ref, hn_ref, acc_ref):
    j = pl.program_id(1)

    @pl.when(j == 0)
    def _():
        x1 = (x_ref[...] + _dot(attn_ref[...], wo_ref[0:ATTN_WIDTH, :])
              + _dot(pool_ref[...], wo_ref[ATTN_WIDTH:, :]))
        x1_ref[...] = x1
        hn_ref[...] = _rmsnorm(x1, g2_ref[...]).astype(BF16)
        acc_ref[...] = jnp.zeros(acc_ref.shape, F32)

    hid = jnp.maximum(_dot(hn_ref[...], wup_ref[...]), 0.0)
    acc_ref[...] += _dot((hid * hid).astype(BF16), wdown_ref[...])

    @pl.when(j == pl.num_programs(1) - 1)
    def _():
        y_ref[...] = _rmsnorm(x1_ref[...] + acc_ref[...], gf_ref[...])


def _out_mlp(x, attn, pool, wo, g2, wup, wdown, gf, tl, tf):
    rows = x.shape[0]
    const2 = lambda i, j: (0, 0)
    return pl.pallas_call(
        _mlp_kernel,
        grid=(rows // tl, D_FF // tf),
        in_specs=[
            pl.BlockSpec((tl, D_MODEL), lambda i, j: (i, 0)),
            pl.BlockSpec((tl, ATTN_WIDTH), lambda i, j: (i, 0)),
            pl.BlockSpec((tl, POOL_WIDTH), lambda i, j: (i, 0)),
            pl.BlockSpec((D_MODEL, D_MODEL), const2),
            pl.BlockSpec((1, D_MODEL), const2),
            pl.BlockSpec((D_MODEL, tf), lambda i, j: (0, j)),
            pl.BlockSpec((tf, D_MODEL), lambda i, j: (j, 0)),
            pl.BlockSpec((1, D_MODEL), const2),
        ],
        out_specs=pl.BlockSpec((tl, D_MODEL), lambda i, j: (i, 0)),
        out_shape=jax.ShapeDtypeStruct((rows, D_MODEL), F32),
        scratch_shapes=[pltpu.VMEM((tl, D_MODEL), F32), pltpu.VMEM((tl, D_MODEL), BF16),
                        pltpu.VMEM((tl, D_MODEL), F32)],
        compiler_params=pltpu.CompilerParams(dimension_semantics=("arbitrary", "arbitrary"),
                                             vmem_limit_bytes=VMEM_LIMIT),
        name="out_mlp",
    )(x, attn, pool, wo, g2, wup, wdown, gf)


def _sproj_kernel(x_ref, g_ref, win_ref, c_ref, s1_ref, s2_ref, state_ref, wpool_ref, pscale_ref,
                  q_ref, k_ref, v_ref, u_ref, pool_ref, ext_ref):
    rows = x_ref.shape[0]
    nseq, dec = state_ref.shape[0], rows // state_ref.shape[0]
    hb = _rmsnorm(x_ref[...], g_ref[...]).astype(BF16)
    z = _dot(hb, win_ref[...])
    for g in range(2 * ATTN_WIDTH // LANES):
        xg = z[:, LANES * g:LANES * (g + 1)]
        r = (xg * c_ref[...] + pltpu.roll(xg, ROT_HALF, 1) * s2_ref[...]
             + pltpu.roll(xg, LANES - ROT_HALF, 1) * s1_ref[...])
        if LANES * g < ATTN_WIDTH:
            q_ref[:, LANES * g:LANES * (g + 1)] = r * ATTN_SCALE
        else:
            k_ref[:, LANES * g - ATTN_WIDTH:LANES * (g + 1) - ATTN_WIDTH] = r
    v_ref[...] = z[:, 2 * ATTN_WIDTH:3 * ATTN_WIDTH]
    u = z[:, 3 * ATTN_WIDTH:]
    u_ref[...] = u

    ext_ref[:, 0:HALO, :] = state_ref[...]
    ext_ref[:, HALO:HALO + dec, :] = u.reshape(nseq, dec, POOL_WIDTH)
    for g, w in enumerate(POOL_WINDOWS):
        lanes = slice(POOL_CH * g, POOL_CH * (g + 1))
        ug = ext_ref[:, HALO:HALO + dec, lanes]
        acc = ug
        for j in range(1, w):
            acc = acc + ext_ref[:, HALO - j:HALO - j + dec, lanes]
        d = (acc / float(w) - ug).reshape(rows, POOL_CH)
        y = _dot(d.astype(BF16), wpool_ref[g]) * pscale_ref[:, lanes]
        pool_ref[:, lanes] = y.astype(BF16)


def _sample_proj(x, g1, win, c_tab, s1_tab, s2_tab, state_pad, wpool, pscale):
    rows = x.shape[0]
    nseq = state_pad.shape[0]
    shapes = [jax.ShapeDtypeStruct((rows, ATTN_WIDTH), F32)] * 3 + [
        jax.ShapeDtypeStruct((rows, POOL_WIDTH), F32), jax.ShapeDtypeStruct((rows, POOL_WIDTH), BF16)]
    return pl.pallas_call(
        _sproj_kernel,
        out_shape=shapes,
        scratch_shapes=[pltpu.VMEM((nseq, HALO + rows // nseq, POOL_WIDTH), F32)],
        compiler_params=pltpu.CompilerParams(vmem_limit_bytes=VMEM_LIMIT),
        name="sample_proj",
    )(x, g1, win, c_tab, s1_tab, s2_tab, state_pad, wpool, pscale)


GROUP = 8
K_BUF = 4 * GROUP
V_BUF = 8 * GROUP
DMA_THREADS = 2


def _dec_kernel(pt_ref, q_ref, kn_ref, vn_ref, ck_hbm, cv_hbm, o_ref,
                kbuf, vbuf, ksem, vsem, s_scr, p_scr, bias_scr):
    b = pl.program_id(0)
    n_pages = pt_ref.shape[1]
    page = kbuf.shape[-1]
    ppb = MOBA_BLOCK // page
    nblk = n_pages // ppb
    n_groups = n_pages // GROUP
    dec = q_ref.shape[1]
    rows = N_HEADS * dec

    def slot_of(g, e, n_buf):
        return (g % (n_buf // GROUP)) * GROUP + e

    def k_copy(j, slot, seq=b):
        return pltpu.make_async_copy(ck_hbm.at[pt_ref[seq, j]], kbuf.at[slot], ksem.at[slot])

    def v_copy(j, slot):
        return pltpu.make_async_copy(cv_hbm.at[pt_ref[b, j]], vbuf.at[slot], vsem.at[slot])

    @pl.when(b == 0)
    def _():
        for s in range(K_BUF):
            k_copy(s, s).start(priority=s % DMA_THREADS)

    for s in range(V_BUF):
        v_copy(s, s).start(priority=s % DMA_THREADS)

    r_idx = lax.broadcasted_iota(jnp.int32, (rows, ATTN_WIDTH), 0)
    c_idx = lax.broadcasted_iota(jnp.int32, (rows, ATTN_WIDTH), 1)
    head_mask = (c_idx // HEAD_DIM) == (r_idx // dec)
    q_bd = jnp.where(head_mask, jnp.concatenate([q_ref[0]] * N_HEADS, axis=0), 0.0)
    q_hi, q_lo = _split_bf16(q_bd)
    q_hilo = jnp.concatenate([q_hi, q_lo], axis=0)
    lane = lax.broadcasted_iota(jnp.int32, (rows, LANES), 1)

    def refill(copy, g, n_buf):
        @pl.when((g + n_buf // GROUP) * GROUP < n_pages)
        def _():
            for e in range(GROUP):
                copy(g * GROUP + e + n_buf, slot_of(g, e, n_buf)).start(priority=e % DMA_THREADS)

    def block_of(buf, g, e, n_buf):
        return jnp.concatenate(
            [buf[slot_of(g, e + i, n_buf)].reshape(ATTN_WIDTH, page).astype(BF16) for i in range(ppb)],
            axis=1)

    def k_group(g, gate):
        for e in range(GROUP):
            k_copy(g * GROUP + e, slot_of(g, e, K_BUF)).wait()
        for e in range(0, GROUP, ppb):
            n = (g * GROUP + e) // ppb
            kt = block_of(kbuf, g, e, K_BUF)
            s2 = _dot(q_hilo, kt)
            s = s2[:rows] + s2[rows:]
            s_scr[n] = s
            col = jnp.sum(s, axis=1, keepdims=True)
            gate = jnp.where(lane == n, col, gate)
        refill(k_copy, g, K_BUF)
        return gate

    gate = lax.fori_loop(0, n_groups, k_group, jnp.zeros((rows, LANES), F32)) * (1.0 / MOBA_BLOCK)

    @pl.when(b + 1 < pl.num_programs(0))
    def _():
        for s in range(K_BUF):
            k_copy(s, s, b + 1).start(priority=s % DMA_THREADS)

    sel = _top_k_mask(gate, lane < nblk, lane, 1)
    bias = jnp.where(sel, 0.0, NEG)

    def bias_group(g, carry):
        for e in range(GROUP):
            n = g * GROUP + e
            col = jnp.sum(jnp.where(lane == n, bias, 0.0), axis=1, keepdims=True)
            bias_scr[n] = jnp.broadcast_to(col, (rows, MOBA_BLOCK))
        return carry

    lax.fori_loop(0, nblk // GROUP, bias_group, 0)

    lane_b = lax.broadcasted_iota(jnp.int32, (rows, MOBA_BLOCK), 1)
    q_pos = lax.broadcasted_iota(jnp.int32, (rows, MOBA_BLOCK), 0) % dec
    s_own = jnp.where(lane_b <= q_pos, _dot_nt(q_hi, kn_ref[0]), NEG)

    def max_group(g, mv):
        for e in range(GROUP):
            n = g * GROUP + e
            mv = jnp.maximum(mv, s_scr[n] + bias_scr[n])
        return mv

    m = jnp.max(lax.fori_loop(0, nblk // GROUP, max_group, s_own), axis=1, keepdims=True)
    p_own = jnp.exp(s_own - m)

    def exp_group(g, lv):
        for e in range(GROUP):
            n = g * GROUP + e
            p = jnp.exp(s_scr[n] + (bias_scr[n] - m))
            p_scr[n] = p.astype(BF16)
            lv = lv + p
        return lv

    l = jnp.sum(lax.fori_loop(0, nblk // GROUP, exp_group, p_own), axis=1, keepdims=True)

    def v_group(g, acc):
        for e in range(GROUP):
            v_copy(g * GROUP + e, slot_of(g, e, V_BUF)).wait()
        for e in range(0, GROUP, ppb):
            acc = acc + _dot_nt(p_scr[(g * GROUP + e) // ppb], block_of(vbuf, g, e, V_BUF))
        refill(v_copy, g, V_BUF)
        return acc

    acc = lax.fori_loop(0, n_groups, v_group, _dot(p_own.astype(BF16), vn_ref[0]))
    o = jnp.where(head_mask, acc / l, 0.0)
    out = o[0:dec]
    for h in range(1, N_HEADS):
        out = out + o[h * dec:(h + 1) * dec]
    o_ref[0] = out


def _sample_attn(page_table, q_s, kn_pad, vn_pad, cache_kT, cache_vT):
    nseq, n_pages = page_table.shape
    dec = q_s.shape[1]
    page = cache_kT.shape[-1]
    nblk = n_pages * page // MOBA_BLOCK
    grid_spec = pltpu.PrefetchScalarGridSpec(
        num_scalar_prefetch=1,
        grid=(nseq,),
        in_specs=[
            pl.BlockSpec((1, dec, ATTN_WIDTH), lambda b, pt: (b, 0, 0)),
            pl.BlockSpec((1, MOBA_BLOCK, ATTN_WIDTH), lambda b, pt: (b, 0, 0)),
            pl.BlockSpec((1, MOBA_BLOCK, ATTN_WIDTH), lambda b, pt: (b, 0, 0)),
            pl.BlockSpec(memory_space=pl.ANY),
            pl.BlockSpec(memory_space=pl.ANY),
        ],
        out_specs=pl.BlockSpec((1, dec, ATTN_WIDTH), lambda b, pt: (b, 0, 0)),
        scratch_shapes=[
            pltpu.VMEM((K_BUF, N_HEADS, HEAD_DIM, page), F32),
            pltpu.VMEM((V_BUF, N_HEADS, HEAD_DIM, page), F32),
            pltpu.SemaphoreType.DMA((K_BUF,)),
            pltpu.SemaphoreType.DMA((V_BUF,)),
            pltpu.VMEM((nblk, N_HEADS * dec, MOBA_BLOCK), F32),
            pltpu.VMEM((nblk, N_HEADS * dec, MOBA_BLOCK), BF16),
            pltpu.VMEM((nblk, N_HEADS * dec, MOBA_BLOCK), F32),
        ],
    )
    return pl.pallas_call(
        _dec_kernel,
        grid_spec=grid_spec,
        out_shape=jax.ShapeDtypeStruct((nseq, dec, ATTN_WIDTH), F32),
        compiler_params=pltpu.CompilerParams(dimension_semantics=("arbitrary",),
                                             vmem_limit_bytes=VMEM_LIMIT),
        name="sample_attn",
    )(page_table, q_s, kn_pad, vn_pad, cache_kT, cache_vT)


def _rope_angles(pos):
    inv = jnp.power(ROPE_THETA, -jnp.arange(ROT_HALF, dtype=F32) * (2.0 / ROT_DIM))
    ang = pos.astype(F32)[:, None] * inv[None, :]
    return jnp.cos(ang), jnp.sin(ang)


def _rope_tables(pos):
    cos, sin = _rope_angles(pos)
    lane = jnp.arange(LANES) % HEAD_DIM
    f = lane % ROT_HALF
    c_tab = jnp.where(lane[None, :] < ROT_DIM, cos[:, f], 1.0)
    s1_tab = jnp.where(lane[None, :] < ROT_HALF, -sin[:, f], 0.0)
    s2_tab = jnp.where((lane[None, :] >= ROT_HALF) & (lane[None, :] < ROT_DIM), sin[:, f], 0.0)
    return c_tab, s1_tab, s2_tab


def _layer(l, xp, xs, cache_k, cache_v, state_pool, page_table, norm1_g, w_in, w_pool, pool_scale,
           w_o, norm2_g, w_up, w_down, final_g):
    seq = xp.shape[0]
    nseq, dec, _ = xs.shape
    past_len = page_table.shape[1] * cache_k.shape[2]

    g1 = norm1_g[l][None, :]
    g2 = norm2_g[l][None, :]
    win = w_in[l].astype(BF16)
    wqkvT = win[:, :3 * ATTN_WIDTH].T
    wu = win[:, 3 * ATTN_WIDTH:]
    wpool = w_pool[l].astype(BF16)
    pscale = pool_scale[l][None, :]
    wo = w_o[l].astype(BF16)
    wup = w_up[l].astype(BF16)
    wdown = w_down[l].astype(BF16)
    gf = final_g[None, :]

    cos, sin = _rope_angles(jnp.arange(seq, dtype=jnp.int32))
    qT, kT, vT, v_tiles, k_aug, kmean, pool_p, u_tail = _prompt_proj(
        xp, g1, wqkvT, wu, cos.T, sin.T, wpool, pscale)
    nb = seq // MOBA_BLOCK
    kmean_hm = kmean[:, 0, :].reshape(nb, N_HEADS, HEAD_DIM).transpose(1, 0, 2)
    attn_p = _prompt_attn(qT, k_aug, v_tiles, kmean_hm)
    yp = _out_mlp(xp, attn_p, pool_p, wo, g2, wup, wdown, gf, tl=512, tf=1024)
    k_prompt = kT.reshape(N_HEADS, HEAD_DIM, seq).transpose(2, 0, 1)[None]
    v_prompt = vT.reshape(N_HEADS, HEAD_DIM, seq).transpose(2, 0, 1)[None]
    pool_prompt = u_tail[HALO - POOL_STATE:][None]

    rows = nseq * dec
    pos_s = past_len + jnp.arange(dec, dtype=jnp.int32)
    c_s, s1_s, s2_s = _rope_tables(jnp.tile(pos_s, nseq))
    state = state_pool[l].astype(F32)
    state_pad = jnp.pad(state, ((0, 0), (HALO - POOL_STATE, 0), (0, 0)))
    q_s, k_s, v_s, u_s, pool_s = _sample_proj(
        xs.reshape(rows, D_MODEL), g1, win, c_s, s1_s, s2_s, state_pad, wpool, pscale)
    pad = ((0, 0), (0, MOBA_BLOCK - dec), (0, 0))
    kn_pad = jnp.pad(k_s.reshape(nseq, dec, ATTN_WIDTH), pad).astype(BF16)
    vn_pad = jnp.pad(v_s.reshape(nseq, dec, ATTN_WIDTH), pad).astype(BF16)
    cache_kT = jnp.transpose(cache_k[l], (0, 2, 3, 1))
    cache_vT = jnp.transpose(cache_v[l], (0, 2, 3, 1))
    attn_s = _sample_attn(page_table, q_s.reshape(nseq, dec, ATTN_WIDTH), kn_pad, vn_pad,
                          cache_kT, cache_vT)
    ys = _out_mlp(xs.reshape(rows, D_MODEL), attn_s.reshape(rows, ATTN_WIDTH).astype(BF16), pool_s,
                  wo, g2, wup, wdown, gf, tl=rows, tf=1024)
    k_sample = k_s.reshape(nseq, dec, N_HEADS, HEAD_DIM)
    v_sample = v_s.reshape(nseq, dec, N_HEADS, HEAD_DIM)
    u_ext = jnp.concatenate([state, u_s.reshape(nseq, dec, POOL_WIDTH)], axis=1)
    pool_sample = u_ext[:, u_ext.shape[1] - POOL_STATE:]
    return (yp, ys.reshape(nseq, dec, D_MODEL), k_prompt, v_prompt, pool_prompt,
            k_sample, v_sample, pool_sample)


def kernel(x_prompt, x_sample, cache_k, cache_v, state_pool, page_table, norm1_g, w_in, w_pool,
           pool_scale, w_o, norm2_g, w_up, w_down, final_g):
    depth = w_in.shape[0]
    assert depth == 1 and x_prompt.shape[0] == 1, "single layer, single prompt sequence"
    (yp, ys, k_p, v_p, pool_p, k_s, v_s, pool_s) = _layer(
        0, x_prompt[0], x_sample, cache_k, cache_v, state_pool, page_table, norm1_g, w_in, w_pool,
        pool_scale, w_o, norm2_g, w_up, w_down, final_g)
    return (yp[None], ys, k_p[None], v_p[None], pool_p[None], k_s[None], v_s[None], pool_s[None])
```

```python
import functools

import jax
import jax.numpy as jnp
from jax import lax
from jax.experimental import pallas as pl
from jax.experimental.pallas import tpu as pltpu

F32 = jnp.float32
BF16 = jnp.bfloat16

D_MODEL = 1024
HEAD_DIM = 64
N_HEADS = 8
ATTN_WIDTH = N_HEADS * HEAD_DIM
POOL_WINDOWS = (2, 4, 8, 16)
POOL_CH = 128
POOL_WIDTH = POOL_CH * len(POOL_WINDOWS)
POOL_STATE = max(POOL_WINDOWS) - 1
HALO = 16
D_FF = 4 * D_MODEL
MOBA_BLOCK = 256
MOBA_TOP_K = 3
ROPE_THETA = 500000.0
ROT_DIM = HEAD_DIM // 4
ROT_HALF = ROT_DIM // 2
ATTN_SCALE = HEAD_DIM ** -0.5
RMS_EPS = 1e-6
NEG = -1e30
LANES = 128
ATTN_TILE = 512
V_ROWS = HEAD_DIM + 16
VMEM_LIMIT = 56 * 1024 * 1024

_NT = (((1,), (1,)), ((), ()))


def _dot(a, b):
    return jnp.dot(a, b, preferred_element_type=F32)


def _dot_nt(a, b):
    return lax.dot_general(a, b, _NT, preferred_element_type=F32)


def _split_bf16(x):
    hi = x.astype(BF16)
    lo = (x - hi.astype(F32)).astype(BF16)
    return hi, lo


def _dot3(a, b):
    a_hi, a_lo = _split_bf16(a)
    b_hi, b_lo = _split_bf16(b)
    return _dot(a_hi, b_hi) + _dot(a_hi, b_lo) + _dot(a_lo, b_hi)


def _rmsnorm(x, g):
    ms = jnp.mean(x * x, axis=-1, keepdims=True)
    return x * lax.rsqrt(ms + RMS_EPS) * g


def _top_k_mask(gate, valid, index, axis):
    limit = gate.shape[axis]
    g = jnp.where(valid, gate, -jnp.inf)
    sel = jnp.zeros(gate.shape, jnp.bool_)
    for _ in range(MOBA_TOP_K):
        mx = jnp.max(g, axis=axis, keepdims=True)
        first = jnp.min(jnp.where(g == mx, index, limit), axis=axis, keepdims=True)
        pick = jnp.logical_and(index == first, mx > -jnp.inf)
        sel = jnp.logical_or(sel, pick)
        g = jnp.where(pick, -jnp.inf, g)
    return sel


def _proj_kernel(x_ref, g_ref, wqkvT_ref, wu_ref, cosT_ref, sinT_ref, wpool_ref, pscale_ref,
                 qT_ref, kT_ref, vT_ref, vblk_ref, kaug_ref, kmean_ref, pool_ref, utail_ref,
                 ext_ref):
    i = pl.program_id(0)
    tl = x_ref.shape[0]
    hb = _rmsnorm(x_ref[...], g_ref[...]).astype(BF16)

    zT = _dot_nt(wqkvT_ref[...], hb)
    cosT = cosT_ref[...]
    sinT = sinT_ref[...]
    k_rows = []
    for hd in range(2 * N_HEADS):
        base = hd * HEAD_DIM
        x1 = zT[base:base + ROT_HALF]
        x2 = zT[base + ROT_HALF:base + ROT_DIM]
        rest = zT[base + ROT_DIM:base + HEAD_DIM]
        r1 = x1 * cosT - x2 * sinT
        r2 = x2 * cosT + x1 * sinT
        if hd < N_HEADS:
            qT_ref[base:base + ROT_HALF, :] = r1 * ATTN_SCALE
            qT_ref[base + ROT_HALF:base + ROT_DIM, :] = r2 * ATTN_SCALE
            qT_ref[base + ROT_DIM:base + HEAD_DIM, :] = rest * ATTN_SCALE
        else:
            k_rows += [r1, r2, rest]
    kT = jnp.concatenate(k_rows, axis=0)
    kT_ref[...] = kT
    vT = zT[2 * ATTN_WIDTH:3 * ATTN_WIDTH]
    vT_ref[...] = vT
    ones_rows = jnp.ones((V_ROWS - HEAD_DIM, tl), BF16)
    vT16 = vT.astype(BF16)
    vblk_ref[0] = jnp.concatenate(
        [piece for h in range(N_HEADS)
         for piece in (vT16[HEAD_DIM * h:HEAD_DIM * (h + 1)], ones_rows)], axis=0)

    k_rm = kT.T
    lane = lax.broadcasted_iota(jnp.int32, (tl, LANES), 1)
    block_tag = jnp.where(lane - HEAD_DIM == i, 1.0, 0.0)
    low = lane < HEAD_DIM
    for g in range(ATTN_WIDTH // LANES):
        kr = k_rm[:, LANES * g:LANES * (g + 1)]
        kaug_ref[2 * g] = jnp.where(low, kr, block_tag).astype(BF16)
        kaug_ref[2 * g + 1] = jnp.where(low, pltpu.roll(kr, HEAD_DIM, 1), block_tag).astype(BF16)
        ksum = jnp.sum(kr, axis=0, keepdims=True) * (1.0 / tl)
        kmean_ref[0, :, LANES * g:LANES * (g + 1)] = jnp.broadcast_to(ksum, (8, LANES))

    u = _dot(hb, wu_ref[...])

    @pl.when(i == 0)
    def _():
        ext_ref[0:HALO, :] = jnp.zeros((HALO, POOL_WIDTH), F32)

    ext_ref[HALO:HALO + tl, :] = u
    row = lax.broadcasted_iota(jnp.int32, (tl, 1), 0) + i * tl
    for g, w in enumerate(POOL_WINDOWS):
        lanes = slice(POOL_CH * g, POOL_CH * (g + 1))
        ug = ext_ref[HALO:HALO + tl, lanes]
        acc = ug
        for j in range(1, w):
            acc = acc + ext_ref[HALO - j:HALO - j + tl, lanes]
        cnt = jnp.minimum(row + 1, w).astype(F32)
        d = acc / cnt - ug
        y = _dot(d.astype(BF16), wpool_ref[g]) * pscale_ref[:, lanes]
        pool_ref[:, lanes] = y.astype(BF16)
    tail = ext_ref[tl:tl + HALO, :]
    ext_ref[0:HALO, :] = tail
    utail_ref[...] = tail


def _prompt_proj(x, g1, wqkvT, wu, cosT, sinT, wpool, pscale):
    seq = x.shape[0]
    tl = MOBA_BLOCK
    nb = seq // tl
    bpt = ATTN_TILE // tl
    const2 = lambda i: (0, 0)
    return pl.pallas_call(
        _proj_kernel,
        grid=(nb,),
        in_specs=[
            pl.BlockSpec((tl, D_MODEL), lambda i: (i, 0)),
            pl.BlockSpec((1, D_MODEL), const2),
            pl.BlockSpec((3 * ATTN_WIDTH, D_MODEL), const2),
            pl.BlockSpec((D_MODEL, POOL_WIDTH), const2),
            pl.BlockSpec((ROT_HALF, tl), lambda i: (0, i)),
            pl.BlockSpec((ROT_HALF, tl), lambda i: (0, i)),
            pl.BlockSpec((len(POOL_WINDOWS), POOL_CH, POOL_CH), lambda i: (0, 0, 0)),
            pl.BlockSpec((1, POOL_WIDTH), const2),
        ],
        out_specs=[
            pl.BlockSpec((ATTN_WIDTH, tl), lambda i: (0, i)),
            pl.BlockSpec((ATTN_WIDTH, tl), lambda i: (0, i)),
            pl.BlockSpec((ATTN_WIDTH, tl), lambda i: (0, i)),
            pl.BlockSpec((1, N_HEADS * V_ROWS, tl), lambda i: (i // bpt, 0, i % bpt)),
            pl.BlockSpec((N_HEADS, tl, LANES), lambda i: (0, i, 0)),
            pl.BlockSpec((1, 8, ATTN_WIDTH), lambda i: (i, 0, 0)),
            pl.BlockSpec((tl, POOL_WIDTH), lambda i: (i, 0)),
            pl.BlockSpec((HALO, POOL_WIDTH), const2),
        ],
        out_shape=[
            jax.ShapeDtypeStruct((ATTN_WIDTH, seq), F32),
            jax.ShapeDtypeStruct((ATTN_WIDTH, seq), F32),
            jax.ShapeDtypeStruct((ATTN_WIDTH, seq), F32),
            jax.ShapeDtypeStruct((nb // bpt, N_HEADS * V_ROWS, bpt * tl), BF16),
            jax.ShapeDtypeStruct((N_HEADS, seq, LANES), BF16),
            jax.ShapeDtypeStruct((nb, 8, ATTN_WIDTH), F32),
            jax.ShapeDtypeStruct((seq, POOL_WIDTH), BF16),
            jax.ShapeDtypeStruct((HALO, POOL_WIDTH), F32),
        ],
        scratch_shapes=[pltpu.VMEM((HALO + tl, POOL_WIDTH), F32)],
        compiler_params=pltpu.CompilerParams(dimension_semantics=("arbitrary",),
                                             vmem_limit_bytes=VMEM_LIMIT),
        name="prompt_proj",
    )(x, g1, wqkvT, wu, cosT, sinT, wpool, pscale)


def _attn_kernel(q_ref, k_ref, v_ref, km_ref, o_ref, s_ref, cmax_ref):
    tile_id = pl.program_id(1)
    nblk = km_ref.shape[1]
    tile = q_ref.shape[1]
    blk_idx = lax.broadcasted_iota(jnp.int32, (nblk, tile), 0)
    q_blk = (tile_id * (tile // MOBA_BLOCK)
             + lax.broadcasted_iota(jnp.int32, (nblk, tile), 1) // MOBA_BLOCK)
    key_idx = lax.broadcasted_iota(jnp.int32, (tile, tile), 0)
    qry_idx = lax.broadcasted_iota(jnp.int32, (tile, tile), 1)

    q_aug = []
    for hh in range(2):
        qT = q_ref[HEAD_DIM * hh:HEAD_DIM * (hh + 1), :]
        gate = _dot3(km_ref[hh], qT)
        sel = _top_k_mask(gate, blk_idx < q_blk, blk_idx, 0)
        bias = jnp.where(jnp.logical_or(sel, blk_idx == q_blk), 0.0, NEG)
        q_aug.append(jnp.concatenate([qT.astype(BF16), bias.astype(BF16)], axis=0))

    def scores(hh, c):
        start = pl.multiple_of(c * tile, tile)
        return _dot(k_ref[hh, pl.ds(start, tile), :], q_aug[hh])

    def values(hh, c):
        return v_ref[c, V_ROWS * hh:V_ROWS * (hh + 1), :]

    def produce(slot, c, causal):
        for hh in range(2):
            s = scores(hh, c)
            if causal:
                s = jnp.where(key_idx <= qry_idx, s, NEG)
            s_ref[slot, hh, :, 0:tile] = s
            cmax_ref[slot, hh] = jnp.broadcast_to(jnp.max(s, axis=0, keepdims=True), (8, tile))

    def consume(slot, c, carry):
        out = []
        for hh in range(2):
            m, acc = carry[2 * hh:2 * hh + 2]
            m_new = jnp.maximum(m, cmax_ref[slot, hh, 0:1, :])
            alpha = jnp.exp(m - m_new)
            p = jnp.exp(s_ref[slot, hh, :, 0:tile] - m_new).astype(BF16)
            out += [m_new, acc * alpha + _dot(values(hh, c), p)]
        return tuple(out)

    produce(0, tile_id, True)
    init = (jnp.full((1, tile), NEG, F32), jnp.zeros((V_ROWS, tile), F32))

    def pair(i, carry):
        produce(1, 2 * i, False)
        carry = consume(0, jnp.where(i == 0, tile_id, 2 * i - 1), carry)
        produce(0, 2 * i + 1, False)
        return consume(1, 2 * i, carry)

    state = lax.fori_loop(0, tile_id // 2, pair, init + init)
    last = tile_id - 1

    def odd_tail(carry):
        produce(1, last, False)
        carry = consume(0, jnp.where(last == 0, tile_id, last - 1), carry)
        return consume(1, last, carry)

    def even_tail(carry):
        return consume(0, jnp.where(tile_id == 0, tile_id, last), carry)

    state = lax.cond(tile_id % 2 == 1, odd_tail, even_tail, state)
    oT = jnp.concatenate([acc[:HEAD_DIM] / acc[HEAD_DIM:HEAD_DIM + 1] for acc in (state[1], state[3])],
                         axis=0)
    o_ref[...] = oT.T.astype(BF16)


def _prompt_attn(qT, k_aug, v_tiles, kmean_hm):
    seq = qT.shape[1]
    nt, _, tile = v_tiles.shape
    nblk = kmean_hm.shape[1]
    return pl.pallas_call(
        _attn_kernel,
        grid=(N_HEADS // 2, nt),
        in_specs=[
            pl.BlockSpec((2 * HEAD_DIM, tile), lambda p, t: (p, t)),
            pl.BlockSpec((2, seq, LANES), lambda p, t: (p, 0, 0)),
            pl.BlockSpec((nt, 2 * V_ROWS, tile), lambda p, t: (0, p, 0)),
            pl.BlockSpec((2, nblk, HEAD_DIM), lambda p, t: (p, 0, 0)),
        ],
        out_specs=pl.BlockSpec((tile, 2 * HEAD_DIM), lambda p, t: (t, p)),
        out_shape=jax.ShapeDtypeStruct((seq, ATTN_WIDTH), BF16),
        scratch_shapes=[pltpu.VMEM((2, 2, tile, tile + LANES), F32), pltpu.VMEM((2, 2, 8, tile), F32)],
        compiler_params=pltpu.CompilerParams(dimension_semantics=("arbitrary", "arbitrary"),
                                             vmem_limit_bytes=VMEM_LIMIT),
        name="prompt_attn",
    )(qT, k_aug, v_tiles, kmean_hm)


def _mlp_kernel(x_ref, attn_ref, pool_ref, wo_ref, g2_ref, wup_ref, wdown_ref, gf_ref, y_ref,
                x1_ref, hn_ref, acc_ref):
    j = pl.program_id(1)

    @pl.when(j == 0)
    def _():
        x1 = (x_ref[...] + _dot(attn_ref[...], wo_ref[0:ATTN_WIDTH, :])
              + _dot(pool_ref[...], wo_ref[ATTN_WIDTH:, :]))
        x1_ref[...] = x1
        hn_ref[...] = _rmsnorm(x1, g2_ref[...]).astype(BF16)
        acc_ref[...] = jnp.zeros(acc_ref.shape, F32)

    hid = jnp.maximum(_dot(hn_ref[...], wup_ref[...]), 0.0)
    acc_ref[...] += _dot((hid * hid).astype(BF16), wdown_ref[...])

    @pl.when(j == pl.num_programs(1) - 1)
    def _():
        y_ref[...] = _rmsnorm(x1_ref[...] + acc_ref[...], gf_ref[...])


def _out_mlp(x, attn, pool, wo, g2, wup, wdown, gf, tl, tf):
    rows = x.shape[0]
    const2 = lambda i, j: (0, 0)
    return pl.pallas_call(
        _mlp_kernel,
        grid=(rows // tl, D_FF // tf),
        in_specs=[
            pl.BlockSpec((tl, D_MODEL), lambda i, j: (i, 0)),
            pl.BlockSpec((tl, ATTN_WIDTH), lambda i, j: (i, 0)),
            pl.BlockSpec((tl, POOL_WIDTH), lambda i, j: (i, 0)),
            pl.BlockSpec((D_MODEL, D_MODEL), const2),
            pl.BlockSpec((1, D_MODEL), const2),
            pl.BlockSpec((D_MODEL, tf), lambda i, j: (0, j)),
            pl.BlockSpec((tf, D_MODEL), lambda i, j: (j, 0)),
            pl.BlockSpec((1, D_MODEL), const2),
        ],
        out_specs=pl.BlockSpec((tl, D_MODEL), lambda i, j: (i, 0)),
        out_shape=jax.ShapeDtypeStruct((rows, D_MODEL), F32),
        scratch_shapes=[pltpu.VMEM((tl, D_MODEL), F32), pltpu.VMEM((tl, D_MODEL), BF16),
                        pltpu.VMEM((tl, D_MODEL), F32)],
        compiler_params=pltpu.CompilerParams(dimension_semantics=("arbitrary", "arbitrary"),
                                             vmem_limit_bytes=VMEM_LIMIT),
        name="out_mlp",
    )(x, attn, pool, wo, g2, wup, wdown, gf)


def _sproj_kernel(x_ref, g_ref, win_ref, c_ref, s1_ref, s2_ref, state_ref, wpool_ref, pscale_ref,
                  q_ref, k_ref, v_ref, u_ref, pool_ref, ext_ref):
    rows = x_ref.shape[0]
    nseq, dec = state_ref.shape[0], rows // state_ref.shape[0]
    hb = _rmsnorm(x_ref[...], g_ref[...]).astype(BF16)
    z = _dot(hb, win_ref[...])
    for g in range(2 * ATTN_WIDTH // LANES):
        xg = z[:, LANES * g:LANES * (g + 1)]
        r = (xg * c_ref[...] + pltpu.roll(xg, ROT_HALF, 1) * s2_ref[...]
             + pltpu.roll(xg, LANES - ROT_HALF, 1) * s1_ref[...])
        if LANES * g < ATTN_WIDTH:
            q_ref[:, LANES * g:LANES * (g + 1)] = r * ATTN_SCALE
        else:
            k_ref[:, LANES * g - ATTN_WIDTH:LANES * (g + 1) - ATTN_WIDTH] = r
    v_ref[...] = z[:, 2 * ATTN_WIDTH:3 * ATTN_WIDTH]
    u = z[:, 3 * ATTN_WIDTH:]
    u_ref[...] = u

    ext_ref[:, 0:HALO, :] = state_ref[...]
    ext_ref[:, HALO:HALO + dec, :] = u.reshape(nseq, dec, POOL_WIDTH)
    for g, w in enumerate(POOL_WINDOWS):
        lanes = slice(POOL_CH * g, POOL_CH * (g + 1))
        ug = ext_ref[:, HALO:HALO + dec, lanes]
        acc = ug
        for j in range(1, w):
            acc = acc + ext_ref[:, HALO - j:HALO - j + dec, lanes]
        d = (acc / float(w) - ug).reshape(rows, POOL_CH)
        y = _dot(d.astype(BF16), wpool_ref[g]) * pscale_ref[:, lanes]
        pool_ref[:, lanes] = y.astype(BF16)


def _sample_proj(x, g1, win, c_tab, s1_tab, s2_tab, state_pad, wpool, pscale):
    rows = x.shape[0]
    nseq = state_pad.shape[0]
    shapes = [jax.ShapeDtypeStruct((rows, ATTN_WIDTH), F32)] * 3 + [
        jax.ShapeDtypeStruct((rows, POOL_WIDTH), F32), jax.ShapeDtypeStruct((rows, POOL_WIDTH), BF16)]
    return pl.pallas_call(
        _sproj_kernel,
        out_shape=shapes,
        scratch_shapes=[pltpu.VMEM((nseq, HALO + rows // nseq, POOL_WIDTH), F32)],
        compiler_params=pltpu.CompilerParams(vmem_limit_bytes=VMEM_LIMIT),
        name="sample_proj",
    )(x, g1, win, c_tab, s1_tab, s2_tab, state_pad, wpool, pscale)


GROUP = 8
K_BUF = 4 * GROUP
V_BUF = 8 * GROUP
K_PRIORITY = 0
V_PRIORITY = 1


def _dec_kernel(pt_ref, q_ref, kn_ref, vn_ref, ck_hbm, cv_hbm, o_ref,
                kbuf, vbuf, ksem, vsem, s_scr, p_scr, bias_scr):
    b = pl.program_id(0)
    n_pages = pt_ref.shape[1]
    page = kbuf.shape[-1]
    ppb = MOBA_BLOCK // page
    nblk = n_pages // ppb
    n_groups = n_pages // GROUP
    dec = q_ref.shape[1]
    rows = N_HEADS * dec

    def slot_of(g, e, n_buf):
        return (g % (n_buf // GROUP)) * GROUP + e

    def k_copy(j, slot, seq=b):
        return pltpu.make_async_copy(ck_hbm.at[pt_ref[seq, j]], kbuf.at[slot], ksem.at[slot])

    def v_copy(j, slot):
        return pltpu.make_async_copy(cv_hbm.at[pt_ref[b, j]], vbuf.at[slot], vsem.at[slot])

    @pl.when(b == 0)
    def _():
        for s in range(K_BUF):
            k_copy(s, s).start(priority=K_PRIORITY)

    for s in range(V_BUF):
        v_copy(s, s).start(priority=V_PRIORITY)

    r_idx = lax.broadcasted_iota(jnp.int32, (rows, ATTN_WIDTH), 0)
    c_idx = lax.broadcasted_iota(jnp.int32, (rows, ATTN_WIDTH), 1)
    head_mask = (c_idx // HEAD_DIM) == (r_idx // dec)
    q_bd = jnp.where(head_mask, jnp.concatenate([q_ref[0]] * N_HEADS, axis=0), 0.0)
    q_hi, q_lo = _split_bf16(q_bd)
    q_hilo = jnp.concatenate([q_hi, q_lo], axis=0)
    lane = lax.broadcasted_iota(jnp.int32, (rows, LANES), 1)

    def refill(copy, g, n_buf, priority):
        @pl.when((g + n_buf // GROUP) * GROUP < n_pages)
        def _():
            for e in range(GROUP):
                copy(g * GROUP + e + n_buf, slot_of(g, e, n_buf)).start(priority=priority)

    def block_of(buf, g, e, n_buf):
        return jnp.concatenate(
            [buf[slot_of(g, e + i, n_buf)].reshape(ATTN_WIDTH, page).astype(BF16) for i in range(ppb)],
            axis=1)

    def k_group(g, gate):
        for e in range(GROUP):
            k_copy(g * GROUP + e, slot_of(g, e, K_BUF)).wait()
        for e in range(0, GROUP, ppb):
            n = (g * GROUP + e) // ppb
            kt = block_of(kbuf, g, e, K_BUF)
            s2 = _dot(q_hilo, kt)
            s = s2[:rows] + s2[rows:]
            s_scr[n] = s
            col = jnp.sum(s, axis=1, keepdims=True)
            gate = jnp.where(lane == n, col, gate)
        refill(k_copy, g, K_BUF, K_PRIORITY)
        return gate

    gate = lax.fori_loop(0, n_groups, k_group, jnp.zeros((rows, LANES), F32)) * (1.0 / MOBA_BLOCK)

    @pl.when(b + 1 < pl.num_programs(0))
    def _():
        for s in range(K_BUF):
            k_copy(s, s, b + 1).start(priority=K_PRIORITY)

    sel = _top_k_mask(gate, lane < nblk, lane, 1)
    bias = jnp.where(sel, 0.0, NEG)

    def bias_group(g, carry):
        for e in range(GROUP):
            n = g * GROUP + e
            col = jnp.sum(jnp.where(lane == n, bias, 0.0), axis=1, keepdims=True)
            bias_scr[n] = jnp.broadcast_to(col, (rows, MOBA_BLOCK))
        return carry

    lax.fori_loop(0, nblk // GROUP, bias_group, 0)

    lane_b = lax.broadcasted_iota(jnp.int32, (rows, MOBA_BLOCK), 1)
    q_pos = lax.broadcasted_iota(jnp.int32, (rows, MOBA_BLOCK), 0) % dec
    s_own = jnp.where(lane_b <= q_pos, _dot_nt(q_hi, kn_ref[0]), NEG)

    def max_group(g, mv):
        for e in range(GROUP):
            n = g * GROUP + e
            mv = jnp.maximum(mv, s_scr[n] + bias_scr[n])
        return mv

    m = jnp.max(lax.fori_loop(0, nblk // GROUP, max_group, s_own), axis=1, keepdims=True)
    p_own = jnp.exp(s_own - m)

    def exp_group(g, lv):
        for e in range(GROUP):
            n = g * GROUP + e
            p = jnp.exp(s_scr[n] + (bias_scr[n] - m))
            p_scr[n] = p.astype(BF16)
            lv = lv + p
        return lv

    l = jnp.sum(lax.fori_loop(0, nblk // GROUP, exp_group, p_own), axis=1, keepdims=True)

    def v_group(g, acc):
        for e in range(GROUP):
            v_copy(g * GROUP + e, slot_of(g, e, V_BUF)).wait()
        for e in range(0, GROUP, ppb):
            acc = acc + _dot_nt(p_scr[(g * GROUP + e) // ppb], block_of(vbuf, g, e, V_BUF))
        refill(v_copy, g, V_BUF, V_PRIORITY)
        return acc

    acc = lax.fori_loop(0, n_groups, v_group, _dot(p_own.astype(BF16), vn_ref[0]))
    o = jnp.where(head_mask, acc / l, 0.0)
    out = o[0:dec]
    for h in range(1, N_HEADS):
        out = out + o[h * dec:(h + 1) * dec]
    o_ref[0] = out


def _sample_attn(page_table, q_s, kn_pad, vn_pad, cache_kT, cache_vT):
    nseq, n_pages = page_table.shape
    dec = q_s.shape[1]
    page = cache_kT.shape[-1]
    nblk = n_pages * page // MOBA_BLOCK
    grid_spec = pltpu.PrefetchScalarGridSpec(
        num_scalar_prefetch=1,
        grid=(nseq,),
        in_specs=[
            pl.BlockSpec((1, dec, ATTN_WIDTH), lambda b, pt: (b, 0, 0)),
            pl.BlockSpec((1, MOBA_BLOCK, ATTN_WIDTH), lambda b, pt: (b, 0, 0)),
            pl.BlockSpec((1, MOBA_BLOCK, ATTN_WIDTH), lambda b, pt: (b, 0, 0)),
            pl.BlockSpec(memory_space=pl.ANY),
            pl.BlockSpec(memory_space=pl.ANY),
        ],
        out_specs=pl.BlockSpec((1, dec, ATTN_WIDTH), lambda b, pt: (b, 0, 0)),
        scratch_shapes=[
            pltpu.VMEM((K_BUF, N_HEADS, HEAD_DIM, page), F32),
            pltpu.VMEM((V_BUF, N_HEADS, HEAD_DIM, page), F32),
            pltpu.SemaphoreType.DMA((K_BUF,)),
            pltpu.SemaphoreType.DMA((V_BUF,)),
            pltpu.VMEM((nblk, N_HEADS * dec, MOBA_BLOCK), F32),
            pltpu.VMEM((nblk, N_HEADS * dec, MOBA_BLOCK), BF16),
            pltpu.VMEM((nblk, N_HEADS * dec, MOBA_BLOCK), F32),
        ],
    )
    return pl.pallas_call(
        _dec_kernel,
        grid_spec=grid_spec,
        out_shape=jax.ShapeDtypeStruct((nseq, dec, ATTN_WIDTH), F32),
        compiler_params=pltpu.CompilerParams(dimension_semantics=("arbitrary",),
                                             vmem_limit_bytes=VMEM_LIMIT),
        name="sample_attn",
    )(page_table, q_s, kn_pad, vn_pad, cache_kT, cache_vT)


def _rope_angles(pos):
    inv = jnp.power(ROPE_THETA, -jnp.arange(ROT_HALF, dtype=F32) * (2.0 / ROT_DIM))
    ang = pos.astype(F32)[:, None] * inv[None, :]
    return jnp.cos(ang), jnp.sin(ang)


def _rope_tables(pos):
    cos, sin = _rope_angles(pos)
    lane = jnp.arange(LANES) % HEAD_DIM
    f = lane % ROT_HALF
    c_tab = jnp.where(lane[None, :] < ROT_DIM, cos[:, f], 1.0)
    s1_tab = jnp.where(lane[None, :] < ROT_HALF, -sin[:, f], 0.0)
    s2_tab = jnp.where((lane[None, :] >= ROT_HALF) & (lane[None, :] < ROT_DIM), sin[:, f], 0.0)
    return c_tab, s1_tab, s2_tab


def _layer(l, xp, xs, cache_k, cache_v, state_pool, page_table, norm1_g, w_in, w_pool, pool_scale,
           w_o, norm2_g, w_up, w_down, final_g):
    seq = xp.shape[0]
    nseq, dec, _ = xs.shape
    past_len = page_table.shape[1] * cache_k.shape[2]

    g1 = norm1_g[l][None, :]
    g2 = norm2_g[l][None, :]
    win = w_in[l].astype(BF16)
    wqkvT = win[:, :3 * ATTN_WIDTH].T
    wu = win[:, 3 * ATTN_WIDTH:]
    wpool = w_pool[l].astype(BF16)
    pscale = pool_scale[l][None, :]
    wo = w_o[l].astype(BF16)
    wup = w_up[l].astype(BF16)
    wdown = w_down[l].astype(BF16)
    gf = final_g[None, :]

    cos, sin = _rope_angles(jnp.arange(seq, dtype=jnp.int32))
    qT, kT, vT, v_tiles, k_aug, kmean, pool_p, u_tail = _prompt_proj(
        xp, g1, wqkvT, wu, cos.T, sin.T, wpool, pscale)
    nb = seq // MOBA_BLOCK
    kmean_hm = kmean[:, 0, :].reshape(nb, N_HEADS, HEAD_DIM).transpose(1, 0, 2)
    attn_p = _prompt_attn(qT, k_aug, v_tiles, kmean_hm)
    yp = _out_mlp(xp, attn_p, pool_p, wo, g2, wup, wdown, gf, tl=512, tf=1024)
    k_prompt = kT.reshape(N_HEADS, HEAD_DIM, seq).transpose(2, 0, 1)[None]
    v_prompt = vT.reshape(N_HEADS, HEAD_DIM, seq).transpose(2, 0, 1)[None]
    pool_prompt = u_tail[HALO - POOL_STATE:][None]

    rows = nseq * dec
    pos_s = past_len + jnp.arange(dec, dtype=jnp.int32)
    c_s, s1_s, s2_s = _rope_tables(jnp.tile(pos_s, nseq))
    state = state_pool[l].astype(F32)
    state_pad = jnp.pad(state, ((0, 0), (HALO - POOL_STATE, 0), (0, 0)))
    q_s, k_s, v_s, u_s, pool_s = _sample_proj(
        xs.reshape(rows, D_MODEL), g1, win, c_s, s1_s, s2_s, state_pad, wpool, pscale)
    pad = ((0, 0), (0, MOBA_BLOCK - dec), (0, 0))
    kn_pad = jnp.pad(k_s.reshape(nseq, dec, ATTN_WIDTH), pad).astype(BF16)
    vn_pad = jnp.pad(v_s.reshape(nseq, dec, ATTN_WIDTH), pad).astype(BF16)
    cache_kT = jnp.transpose(cache_k[l], (0, 2, 3, 1))
    cache_vT = jnp.transpose(cache_v[l], (0, 2, 3, 1))
    attn_s = _sample_attn(page_table, q_s.reshape(nseq, dec, ATTN_WIDTH), kn_pad, vn_pad,
                          cache_kT, cache_vT)
    ys = _out_mlp(xs.reshape(rows, D_MODEL), attn_s.reshape(rows, ATTN_WIDTH).astype(BF16), pool_s,
                  wo, g2, wup, wdown, gf, tl=rows, tf=1024)
    k_sample = k_s.reshape(nseq, dec, N_HEADS, HEAD_DIM)
    v_sample = v_s.reshape(nseq, dec, N_HEADS, HEAD_DIM)
    u_ext = jnp.concatenate([state, u_s.reshape(nseq, dec, POOL_WIDTH)], axis=1)
    pool_sample = u_ext[:, u_ext.shape[1] - POOL_STATE:]
    return (yp, ys.reshape(nseq, dec, D_MODEL), k_prompt, v_prompt, pool_prompt,
            k_sample, v_sample, pool_sample)


def kernel(x_prompt, x_sample, cache_k, cache_v, state_pool, page_table, norm1_g, w_in, w_pool,
           pool_scale, w_o, norm2_g, w_up, w_down, final_g):
    depth = w_in.shape[0]
    assert depth == 1 and x_prompt.shape[0] == 1, "single layer, single prompt sequence"
    (yp, ys, k_p, v_p, pool_p, k_s, v_s, pool_s) = _layer(
        0, x_prompt[0], x_sample, cache_k, cache_v, state_pool, page_table, norm1_g, w_in, w_pool,
        pool_scale, w_o, norm2_g, w_up, w_down, final_g)
    return (yp[None], ys, k_p[None], v_p[None], pool_p[None], k_s[None], v_s[None], pool_s[None])
```

```python
import functools

import jax
import jax.numpy as jnp
from jax import lax
from jax.experimental import pallas as pl
from jax.experimental.pallas import tpu as pltpu

F32 = jnp.float32
BF16 = jnp.bfloat16

D_MODEL = 1024
HEAD_DIM = 64
N_HEADS = 8
ATTN_WIDTH = N_HEADS * HEAD_DIM
POOL_WINDOWS = (2, 4, 8, 16)
POOL_CH = 128
POOL_WIDTH = POOL_CH * len(POOL_WINDOWS)
POOL_STATE = max(POOL_WINDOWS) - 1
HALO = 16
D_FF = 4 * D_MODEL
MOBA_BLOCK = 256
MOBA_TOP_K = 3
ROPE_THETA = 500000.0
ROT_DIM = HEAD_DIM // 4
ROT_HALF = ROT_DIM // 2
ATTN_SCALE = HEAD_DIM ** -0.5
RMS_EPS = 1e-6
NEG = -1e30
LANES = 128
ATTN_TILE = 512
Q_TILE = 2 * ATTN_TILE
V_ROWS = HEAD_DIM + 16
VMEM_LIMIT = 56 * 1024 * 1024

_NT = (((1,), (1,)), ((), ()))


def _dot(a, b):
    return jnp.dot(a, b, preferred_element_type=F32)


def _dot_nt(a, b):
    return lax.dot_general(a, b, _NT, preferred_element_type=F32)


def _split_bf16(x):
    hi = x.astype(BF16)
    lo = (x - hi.astype(F32)).astype(BF16)
    return hi, lo


def _dot3(a, b):
    a_hi, a_lo = _split_bf16(a)
    b_hi, b_lo = _split_bf16(b)
    return _dot(a_hi, b_hi) + _dot(a_hi, b_lo) + _dot(a_lo, b_hi)


def _rmsnorm(x, g):
    ms = jnp.mean(x * x, axis=-1, keepdims=True)
    return x * lax.rsqrt(ms + RMS_EPS) * g


def _top_k_mask(gate, valid, index, axis):
    limit = gate.shape[axis]
    g = jnp.where(valid, gate, -jnp.inf)
    sel = jnp.zeros(gate.shape, jnp.bool_)
    for _ in range(MOBA_TOP_K):
        mx = jnp.max(g, axis=axis, keepdims=True)
        first = jnp.min(jnp.where(g == mx, index, limit), axis=axis, keepdims=True)
        pick = jnp.logical_and(index == first, mx > -jnp.inf)
        sel = jnp.logical_or(sel, pick)
        g = jnp.where(pick, -jnp.inf, g)
    return sel


def _proj_kernel(x_ref, g_ref, wqkvT_ref, wu_ref, cosT_ref, sinT_ref, wpool_ref, pscale_ref,
                 qT_ref, kT_ref, vT_ref, vblk_ref, kaug_ref, kmean_ref, pool_ref, utail_ref,
                 ext_ref):
    i = pl.program_id(0)
    tl = x_ref.shape[0]
    hb = _rmsnorm(x_ref[...], g_ref[...]).astype(BF16)

    zT = _dot_nt(wqkvT_ref[...], hb)
    cosT = cosT_ref[...]
    sinT = sinT_ref[...]
    k_rows = []
    for hd in range(2 * N_HEADS):
        base = hd * HEAD_DIM
        x1 = zT[base:base + ROT_HALF]
        x2 = zT[base + ROT_HALF:base + ROT_DIM]
        rest = zT[base + ROT_DIM:base + HEAD_DIM]
        r1 = x1 * cosT - x2 * sinT
        r2 = x2 * cosT + x1 * sinT
        if hd < N_HEADS:
            qT_ref[base:base + ROT_HALF, :] = r1 * ATTN_SCALE
            qT_ref[base + ROT_HALF:base + ROT_DIM, :] = r2 * ATTN_SCALE
            qT_ref[base + ROT_DIM:base + HEAD_DIM, :] = rest * ATTN_SCALE
        else:
            k_rows += [r1, r2, rest]
    kT = jnp.concatenate(k_rows, axis=0)
    kT_ref[...] = kT
    vT = zT[2 * ATTN_WIDTH:3 * ATTN_WIDTH]
    vT_ref[...] = vT
    ones_rows = jnp.ones((V_ROWS - HEAD_DIM, tl), BF16)
    vT16 = vT.astype(BF16)
    vblk_ref[0] = jnp.concatenate(
        [piece for h in range(N_HEADS)
         for piece in (vT16[HEAD_DIM * h:HEAD_DIM * (h + 1)], ones_rows)], axis=0)

    k_rm = kT.T
    lane = lax.broadcasted_iota(jnp.int32, (tl, LANES), 1)
    block_tag = jnp.where(lane - HEAD_DIM == i, 1.0, 0.0)
    low = lane < HEAD_DIM
    for g in range(ATTN_WIDTH // LANES):
        kr = k_rm[:, LANES * g:LANES * (g + 1)]
        kaug_ref[2 * g] = jnp.where(low, kr, block_tag).astype(BF16)
        kaug_ref[2 * g + 1] = jnp.where(low, pltpu.roll(kr, HEAD_DIM, 1), block_tag).astype(BF16)
        ksum = jnp.sum(kr, axis=0, keepdims=True) * (1.0 / tl)
        kmean_ref[0, :, LANES * g:LANES * (g + 1)] = jnp.broadcast_to(ksum, (8, LANES))

    u = _dot(hb, wu_ref[...])

    @pl.when(i == 0)
    def _():
        ext_ref[0:HALO, :] = jnp.zeros((HALO, POOL_WIDTH), F32)

    ext_ref[HALO:HALO + tl, :] = u
    row = lax.broadcasted_iota(jnp.int32, (tl, 1), 0) + i * tl
    for g, w in enumerate(POOL_WINDOWS):
        lanes = slice(POOL_CH * g, POOL_CH * (g + 1))
        ug = ext_ref[HALO:HALO + tl, lanes]
        acc = ug
        for j in range(1, w):
            acc = acc + ext_ref[HALO - j:HALO - j + tl, lanes]
        cnt = jnp.minimum(row + 1, w).astype(F32)
        d = acc / cnt - ug
        y = _dot(d.astype(BF16), wpool_ref[g]) * pscale_ref[:, lanes]
        pool_ref[:, lanes] = y.astype(BF16)
    tail = ext_ref[tl:tl + HALO, :]
    ext_ref[0:HALO, :] = tail
    utail_ref[...] = tail


def _prompt_proj(x, g1, wqkvT, wu, cosT, sinT, wpool, pscale):
    seq = x.shape[0]
    tl = MOBA_BLOCK
    nb = seq // tl
    bpt = ATTN_TILE // tl
    const2 = lambda i: (0, 0)
    return pl.pallas_call(
        _proj_kernel,
        grid=(nb,),
        in_specs=[
            pl.BlockSpec((tl, D_MODEL), lambda i: (i, 0)),
            pl.BlockSpec((1, D_MODEL), const2),
            pl.BlockSpec((3 * ATTN_WIDTH, D_MODEL), const2),
            pl.BlockSpec((D_MODEL, POOL_WIDTH), const2),
            pl.BlockSpec((ROT_HALF, tl), lambda i: (0, i)),
            pl.BlockSpec((ROT_HALF, tl), lambda i: (0, i)),
            pl.BlockSpec((len(POOL_WINDOWS), POOL_CH, POOL_CH), lambda i: (0, 0, 0)),
            pl.BlockSpec((1, POOL_WIDTH), const2),
        ],
        out_specs=[
            pl.BlockSpec((ATTN_WIDTH, tl), lambda i: (0, i)),
            pl.BlockSpec((ATTN_WIDTH, tl), lambda i: (0, i)),
            pl.BlockSpec((ATTN_WIDTH, tl), lambda i: (0, i)),
            pl.BlockSpec((1, N_HEADS * V_ROWS, tl), lambda i: (i // bpt, 0, i % bpt)),
            pl.BlockSpec((N_HEADS, tl, LANES), lambda i: (0, i, 0)),
            pl.BlockSpec((1, 8, ATTN_WIDTH), lambda i: (i, 0, 0)),
            pl.BlockSpec((tl, POOL_WIDTH), lambda i: (i, 0)),
            pl.BlockSpec((HALO, POOL_WIDTH), const2),
        ],
        out_shape=[
            jax.ShapeDtypeStruct((ATTN_WIDTH, seq), F32),
            jax.ShapeDtypeStruct((ATTN_WIDTH, seq), F32),
            jax.ShapeDtypeStruct((ATTN_WIDTH, seq), F32),
            jax.ShapeDtypeStruct((nb // bpt, N_HEADS * V_ROWS, bpt * tl), BF16),
            jax.ShapeDtypeStruct((N_HEADS, seq, LANES), BF16),
            jax.ShapeDtypeStruct((nb, 8, ATTN_WIDTH), F32),
            jax.ShapeDtypeStruct((seq, POOL_WIDTH), BF16),
            jax.ShapeDtypeStruct((HALO, POOL_WIDTH), F32),
        ],
        scratch_shapes=[pltpu.VMEM((HALO + tl, POOL_WIDTH), F32)],
        compiler_params=pltpu.CompilerParams(dimension_semantics=("arbitrary",),
                                             vmem_limit_bytes=VMEM_LIMIT),
        name="prompt_proj",
    )(x, g1, wqkvT, wu, cosT, sinT, wpool, pscale)


def _attn_kernel(q_ref, k_ref, v_ref, km_ref, o_ref, s_ref, cmax_ref):
    tile_id = pl.program_id(1)
    nblk = km_ref.shape[1]
    tile = q_ref.shape[1]
    chunk = v_ref.shape[2]
    cpt = tile // chunk
    blk_idx = lax.broadcasted_iota(jnp.int32, (nblk, tile), 0)
    q_blk = (tile_id * (tile // MOBA_BLOCK)
             + lax.broadcasted_iota(jnp.int32, (nblk, tile), 1) // MOBA_BLOCK)
    key_idx = lax.broadcasted_iota(jnp.int32, (chunk, tile), 0)
    qry_idx = lax.broadcasted_iota(jnp.int32, (chunk, tile), 1)

    q_aug = []
    for hh in range(2):
        qT = q_ref[HEAD_DIM * hh:HEAD_DIM * (hh + 1), :]
        gate = _dot3(km_ref[hh], qT)
        sel = _top_k_mask(gate, blk_idx < q_blk, blk_idx, 0)
        bias = jnp.where(jnp.logical_or(sel, blk_idx == q_blk), 0.0, NEG)
        q_aug.append(jnp.concatenate([qT.astype(BF16), bias.astype(BF16)], axis=0))

    def scores(hh, c):
        start = pl.multiple_of(c * chunk, chunk)
        return _dot(k_ref[hh, pl.ds(start, chunk), :], q_aug[hh])

    def values(hh, c):
        return v_ref[c, V_ROWS * hh:V_ROWS * (hh + 1), :]

    def produce(slot, c, diag_offset=None, heads=(0, 1)):
        for hh in heads:
            s = scores(hh, c)
            if diag_offset is not None:
                s = jnp.where(key_idx + diag_offset <= qry_idx, s, NEG)
            s_ref[slot, hh] = s
            cmax_ref[slot, hh] = jnp.broadcast_to(jnp.max(s, axis=0, keepdims=True), (8, tile))

    def consume(slot, c, carry, heads=(0, 1)):
        out = []
        for hh in heads:
            m, acc = carry[2 * hh:2 * hh + 2]
            m_new = jnp.maximum(m, cmax_ref[slot, hh, 0:1, :])
            alpha = jnp.exp(m - m_new)
            p = jnp.exp(s_ref[slot, hh] - m_new).astype(BF16)
            out += [m_new, acc * alpha + _dot(values(hh, c), p)]
        return tuple(out)

    assert cpt == 2
    diag0 = tile_id * cpt
    init = (jnp.full((1, tile), NEG, F32), jnp.zeros((V_ROWS, tile), F32))
    produce(0, diag0, 0)

    def stage(p_slot, p_chunk, c_slot, c_chunk, carry, diag_offset=None):
        half = chunk // 2
        out = []
        for hh in range(2):
            m, acc = carry[2 * hh:2 * hh + 2]
            m_new = jnp.maximum(m, cmax_ref[c_slot, hh, 0:1, :])
            acc = acc * jnp.exp(m - m_new)
            cmax = None
            for kh in range(2):
                start = pl.multiple_of(p_chunk * chunk + kh * half, half)
                s = _dot(k_ref[hh, pl.ds(start, half), :], q_aug[hh])
                if diag_offset is not None:
                    k_pos = lax.broadcasted_iota(jnp.int32, (half, tile), 0) + (diag_offset + kh * half)
                    s = jnp.where(k_pos <= lax.broadcasted_iota(jnp.int32, (half, tile), 1), s, NEG)
                s_ref[p_slot, hh, kh * half:(kh + 1) * half, :] = s
                part = jnp.max(s, axis=0, keepdims=True)
                cmax = part if cmax is None else jnp.maximum(cmax, part)
                p = jnp.exp(s_ref[c_slot, hh, kh * half:(kh + 1) * half, :] - m_new).astype(BF16)
                acc = acc + _dot(v_ref[c_chunk, V_ROWS * hh:V_ROWS * (hh + 1), kh * half:(kh + 1) * half], p)
            cmax_ref[p_slot, hh] = jnp.broadcast_to(cmax, (8, tile))
            out += [m_new, acc]
        return tuple(out)

    def pair(i, carry):
        carry = stage(0, 2 * i, 1, jnp.where(i == 0, diag0 + 1, 2 * i - 1), carry)
        return stage(1, 2 * i + 1, 0, 2 * i, carry)

    state = stage(1, diag0 + 1, 0, diag0, init + init, diag_offset=chunk)
    state = lax.fori_loop(0, tile_id, pair, state)
    state = consume(1, jnp.where(tile_id == 0, diag0 + 1, diag0 - 1), state)
    oT = jnp.concatenate([acc[:HEAD_DIM] / acc[HEAD_DIM:HEAD_DIM + 1] for acc in (state[1], state[3])],
                         axis=0)
    o_ref[...] = oT.T.astype(BF16)


def _prompt_attn(qT, k_aug, v_tiles, kmean_hm):
    seq = qT.shape[1]
    nc, _, chunk = v_tiles.shape
    tile = Q_TILE
    nblk = kmean_hm.shape[1]
    return pl.pallas_call(
        _attn_kernel,
        grid=(N_HEADS // 2, seq // tile),
        in_specs=[
            pl.BlockSpec((2 * HEAD_DIM, tile), lambda p, t: (p, t)),
            pl.BlockSpec((2, seq, LANES), lambda p, t: (p, 0, 0)),
            pl.BlockSpec((nc, 2 * V_ROWS, chunk), lambda p, t: (0, p, 0)),
            pl.BlockSpec((2, nblk, HEAD_DIM), lambda p, t: (p, 0, 0)),
        ],
        out_specs=pl.BlockSpec((tile, 2 * HEAD_DIM), lambda p, t: (t, p)),
        out_shape=jax.ShapeDtypeStruct((seq, ATTN_WIDTH), BF16),
        scratch_shapes=[pltpu.VMEM((2, 2, chunk, tile), F32), pltpu.VMEM((2, 2, 8, tile), F32)],
        compiler_params=pltpu.CompilerParams(dimension_semantics=("arbitrary", "arbitrary"),
                                             vmem_limit_bytes=VMEM_LIMIT),
        name="prompt_attn",
    )(qT, k_aug, v_tiles, kmean_hm)


def _mlp_kernel(x_ref, attn_ref, pool_ref, wo_ref, g2_ref, wup_ref, wdown_ref, gf_ref, y_ref,
                x1_ref, hn_ref, acc_ref):
    j = pl.program_id(1)

    @pl.when(j == 0)
    def _():
        x1 = (x_ref[...] + _dot(attn_ref[...], wo_ref[0:ATTN_WIDTH, :])
              + _dot(pool_ref[...], wo_ref[ATTN_WIDTH:, :]))
        x1_ref[...] = x1
        hn_ref[...] = _rmsnorm(x1, g2_ref[...]).astype(BF16)
        acc_ref[...] = jnp.zeros(acc_ref.shape, F32)

    hid = jnp.maximum(_dot(hn_ref[...], wup_ref[...]), 0.0)
    acc_ref[...] += _dot((hid * hid).astype(BF16), wdown_ref[...])

    @pl.when(j == pl.num_programs(1) - 1)
    def _():
        y_ref[...] = _rmsnorm(x1_ref[...] + acc_ref[...], gf_ref[...])


def _out_mlp(x, attn, pool, wo, g2, wup, wdown, gf, tl, tf):
    rows = x.shape[0]
    const2 = lambda i, j: (0, 0)
    return pl.pallas_call(
        _mlp_kernel,
        grid=(rows // tl, D_FF // tf),
        in_specs=[
            pl.BlockSpec((tl, D_MODEL), lambda i, j: (i, 0)),
            pl.BlockSpec((tl, ATTN_WIDTH), lambda i, j: (i, 0)),
            pl.BlockSpec((tl, POOL_WIDTH), lambda i, j: (i, 0)),
            pl.BlockSpec((D_MODEL, D_MODEL), const2),
            pl.BlockSpec((1, D_MODEL), const2),
            pl.BlockSpec((D_MODEL, tf), lambda i, j: (0, j)),
            pl.BlockSpec((tf, D_MODEL), lambda i, j: (j, 0)),
            pl.BlockSpec((1, D_MODEL), const2),
        ],
        out_specs=pl.BlockSpec((tl, D_MODEL), lambda i, j: (i, 0)),
        out_shape=jax.ShapeDtypeStruct((rows, D_MODEL), F32),
        scratch_shapes=[pltpu.VMEM((tl, D_MODEL), F32), pltpu.VMEM((tl, D_MODEL), BF16),
                        pltpu.VMEM((tl, D_MODEL), F32)],
        compiler_params=pltpu.CompilerParams(dimension_semantics=("arbitrary", "arbitrary"),
                                             vmem_limit_bytes=VMEM_LIMIT),
        name="out_mlp",
    )(x, attn, pool, wo, g2, wup, wdown, gf)


def _sproj_kernel(x_ref, g_ref, win_ref, c_ref, s1_ref, s2_ref, state_ref, wpool_ref, pscale_ref,
                  q_ref, k_ref, v_ref, u_ref, pool_ref, ext_ref):
    rows = x_ref.shape[0]
    nseq, dec = state_ref.shape[0], rows // state_ref.shape[0]
    hb = _rmsnorm(x_ref[...], g_ref[...]).astype(BF16)
    z = _dot(hb, win_ref[...])
    for g in range(2 * ATTN_WIDTH // LANES):
        xg = z[:, LANES * g:LANES * (g + 1)]
        r = (xg * c_ref[...] + pltpu.roll(xg, ROT_HALF, 1) * s2_ref[...]
             + pltpu.roll(xg, LANES - ROT_HALF, 1) * s1_ref[...])
        if LANES * g < ATTN_WIDTH:
            q_ref[:, LANES * g:LANES * (g + 1)] = r * ATTN_SCALE
        else:
            k_ref[:, LANES * g - ATTN_WIDTH:LANES * (g + 1) - ATTN_WIDTH] = r
    v_ref[...] = z[:, 2 * ATTN_WIDTH:3 * ATTN_WIDTH]
    u = z[:, 3 * ATTN_WIDTH:]
    u_ref[...] = u

    ext_ref[:, 0:HALO, :] = state_ref[...]
    ext_ref[:, HALO:HALO + dec, :] = u.reshape(nseq, dec, POOL_WIDTH)
    for g, w in enumerate(POOL_WINDOWS):
        lanes = slice(POOL_CH * g, POOL_CH * (g + 1))
        ug = ext_ref[:, HALO:HALO + dec, lanes]
        acc = ug
        for j in range(1, w):
            acc = acc + ext_ref[:, HALO - j:HALO - j + dec, lanes]
        d = (acc / float(w) - ug).reshape(rows, POOL_CH)
        y = _dot(d.astype(BF16), wpool_ref[g]) * pscale_ref[:, lanes]
        pool_ref[:, lanes] = y.astype(BF16)


def _sample_proj(x, g1, win, c_tab, s1_tab, s2_tab, state_pad, wpool, pscale):
    rows = x.shape[0]
    nseq = state_pad.shape[0]
    shapes = [jax.ShapeDtypeStruct((rows, ATTN_WIDTH), F32)] * 3 + [
        jax.ShapeDtypeStruct((rows, POOL_WIDTH), F32), jax.ShapeDtypeStruct((rows, POOL_WIDTH), BF16)]
    return pl.pallas_call(
        _sproj_kernel,
        out_shape=shapes,
        scratch_shapes=[pltpu.VMEM((nseq, HALO + rows // nseq, POOL_WIDTH), F32)],
        compiler_params=pltpu.CompilerParams(vmem_limit_bytes=VMEM_LIMIT),
        name="sample_proj",
    )(x, g1, win, c_tab, s1_tab, s2_tab, state_pad, wpool, pscale)


GROUP = 8
K_BUF = 4 * GROUP
V_BUF = 8 * GROUP
K_PRIORITY = 0
V_PRIORITY = 1


def _dec_kernel(pt_ref, q_ref, kn_ref, vn_ref, ck_hbm, cv_hbm, o_ref,
                kbuf, vbuf, ksem, vsem, s_scr, p_scr, bias_scr):
    b = pl.program_id(0)
    n_pages = pt_ref.shape[1]
    page = kbuf.shape[-1]
    ppb = MOBA_BLOCK // page
    nblk = n_pages // ppb
    n_groups = n_pages // GROUP
    dec = q_ref.shape[1]
    rows = N_HEADS * dec

    def slot_of(g, e, n_buf):
        return (g % (n_buf // GROUP)) * GROUP + e

    def k_copy(j, slot, seq=b):
        return pltpu.make_async_copy(ck_hbm.at[pt_ref[seq, j]], kbuf.at[slot], ksem.at[slot])

    def v_copy(j, slot):
        return pltpu.make_async_copy(cv_hbm.at[pt_ref[b, j]], vbuf.at[slot], vsem.at[slot])

    @pl.when(b == 0)
    def _():
        for s in range(K_BUF):
            k_copy(s, s).start(priority=K_PRIORITY)

    for s in range(V_BUF):
        v_copy(s, s).start(priority=V_PRIORITY)

    r_idx = lax.broadcasted_iota(jnp.int32, (rows, ATTN_WIDTH), 0)
    c_idx = lax.broadcasted_iota(jnp.int32, (rows, ATTN_WIDTH), 1)
    head_mask = (c_idx // HEAD_DIM) == (r_idx // dec)
    q_bd = jnp.where(head_mask, jnp.concatenate([q_ref[0]] * N_HEADS, axis=0), 0.0)
    q_hi, q_lo = _split_bf16(q_bd)
    q_hilo = jnp.concatenate([q_hi, q_lo], axis=0)
    lane = lax.broadcasted_iota(jnp.int32, (rows, LANES), 1)

    def refill(copy, g, n_buf, priority):
        @pl.when((g + n_buf // GROUP) * GROUP < n_pages)
        def _():
            for e in range(GROUP):
                copy(g * GROUP + e + n_buf, slot_of(g, e, n_buf)).start(priority=priority)

    def block_of(buf, g, e, n_buf):
        return jnp.concatenate(
            [buf[slot_of(g, e + i, n_buf)].reshape(ATTN_WIDTH, page).astype(BF16) for i in range(ppb)],
            axis=1)

    def k_group(g, gate):
        for e in range(GROUP):
            k_copy(g * GROUP + e, slot_of(g, e, K_BUF)).wait()
        for e in range(0, GROUP, ppb):
            n = (g * GROUP + e) // ppb
            kt = block_of(kbuf, g, e, K_BUF)
            s2 = _dot(q_hilo, kt)
            s = s2[:rows] + s2[rows:]
            s_scr[n] = s
            col = jnp.sum(s, axis=1, keepdims=True)
            gate = jnp.where(lane == n, col, gate)
        refill(k_copy, g, K_BUF, K_PRIORITY)
        return gate

    gate = lax.fori_loop(0, n_groups, k_group, jnp.zeros((rows, LANES), F32)) * (1.0 / MOBA_BLOCK)

    @pl.when(b + 1 < pl.num_programs(0))
    def _():
        for s in range(K_BUF):
            k_copy(s, s, b + 1).start(priority=K_PRIORITY)

    sel = _top_k_mask(gate, lane < nblk, lane, 1)
    bias = jnp.where(sel, 0.0, NEG)

    def bias_group(g, carry):
        for e in range(GROUP):
            n = g * GROUP + e
            col = jnp.sum(jnp.where(lane == n, bias, 0.0), axis=1, keepdims=True)
            bias_scr[n] = jnp.broadcast_to(col, (rows, MOBA_BLOCK))
        return carry

    lax.fori_loop(0, nblk // GROUP, bias_group, 0)

    lane_b = lax.broadcasted_iota(jnp.int32, (rows, MOBA_BLOCK), 1)
    q_pos = lax.broadcasted_iota(jnp.int32, (rows, MOBA_BLOCK), 0) % dec
    s_own = jnp.where(lane_b <= q_pos, _dot_nt(q_hi, kn_ref[0]), NEG)

    def max_group(g, mv):
        for e in range(GROUP):
            n = g * GROUP + e
            mv = jnp.maximum(mv, s_scr[n] + bias_scr[n])
        return mv

    m = jnp.max(lax.fori_loop(0, nblk // GROUP, max_group, s_own), axis=1, keepdims=True)
    p_own = jnp.exp(s_own - m)

    def exp_group(g, lv):
        for e in range(GROUP):
            n = g * GROUP + e
            p = jnp.exp(s_scr[n] + (bias_scr[n] - m))
            p_scr[n] = p.astype(BF16)
            lv = lv + p
        return lv

    l = jnp.sum(lax.fori_loop(0, nblk // GROUP, exp_group, p_own), axis=1, keepdims=True)

    def v_group(g, acc):
        for e in range(GROUP):
            v_copy(g * GROUP + e, slot_of(g, e, V_BUF)).wait()
        for e in range(0, GROUP, ppb):
            acc = acc + _dot_nt(p_scr[(g * GROUP + e) // ppb], block_of(vbuf, g, e, V_BUF))
        refill(v_copy, g, V_BUF, V_PRIORITY)
        return acc

    acc = lax.fori_loop(0, n_groups, v_group, _dot(p_own.astype(BF16), vn_ref[0]))
    o = jnp.where(head_mask, acc / l, 0.0)
    out = o[0:dec]
    for h in range(1, N_HEADS):
        out = out + o[h * dec:(h + 1) * dec]
    o_ref[0] = out


def _sample_attn(page_table, q_s, kn_pad, vn_pad, cache_kT, cache_vT):
    nseq, n_pages = page_table.shape
    dec = q_s.shape[1]
    page = cache_kT.shape[-1]
    nblk = n_pages * page // MOBA_BLOCK
    grid_spec = pltpu.PrefetchScalarGridSpec(
        num_scalar_prefetch=1,
        grid=(nseq,),
        in_specs=[
            pl.BlockSpec((1, dec, ATTN_WIDTH), lambda b, pt: (b, 0, 0)),
            pl.BlockSpec((1, MOBA_BLOCK, ATTN_WIDTH), lambda b, pt: (b, 0, 0)),
            pl.BlockSpec((1, MOBA_BLOCK, ATTN_WIDTH), lambda b, pt: (b, 0, 0)),
            pl.BlockSpec(memory_space=pl.ANY),
            pl.BlockSpec(memory_space=pl.ANY),
        ],
        out_specs=pl.BlockSpec((1, dec, ATTN_WIDTH), lambda b, pt: (b, 0, 0)),
        scratch_shapes=[
            pltpu.VMEM((K_BUF, N_HEADS, HEAD_DIM, page), F32),
            pltpu.VMEM((V_BUF, N_HEADS, HEAD_DIM, page), F32),
            pltpu.SemaphoreType.DMA((K_BUF,)),
            pltpu.SemaphoreType.DMA((V_BUF,)),
            pltpu.VMEM((nblk, N_HEADS * dec, MOBA_BLOCK), F32),
            pltpu.VMEM((nblk, N_HEADS * dec, MOBA_BLOCK), BF16),
            pltpu.VMEM((nblk, N_HEADS * dec, MOBA_BLOCK), F32),
        ],
    )
    return pl.pallas_call(
        _dec_kernel,
        grid_spec=grid_spec,
        out_shape=jax.ShapeDtypeStruct((nseq, dec, ATTN_WIDTH), F32),
        compiler_params=pltpu.CompilerParams(dimension_semantics=("arbitrary",),
                                             vmem_limit_bytes=VMEM_LIMIT),
        name="sample_attn",
    )(page_table, q_s, kn_pad, vn_pad, cache_kT, cache_vT)


def _rope_angles(pos):
    inv = jnp.power(ROPE_THETA, -jnp.arange(ROT_HALF, dtype=F32) * (2.0 / ROT_DIM))
    ang = pos.astype(F32)[:, None] * inv[None, :]
    return jnp.cos(ang), jnp.sin(ang)


def _rope_tables(pos):
    cos, sin = _rope_angles(pos)
    lane = jnp.arange(LANES) % HEAD_DIM
    f = lane % ROT_HALF
    c_tab = jnp.where(lane[None, :] < ROT_DIM, cos[:, f], 1.0)
    s1_tab = jnp.where(lane[None, :] < ROT_HALF, -sin[:, f], 0.0)
    s2_tab = jnp.where((lane[None, :] >= ROT_HALF) & (lane[None, :] < ROT_DIM), sin[:, f], 0.0)
    return c_tab, s1_tab, s2_tab


def _layer(l, xp, xs, cache_k, cache_v, state_pool, page_table, norm1_g, w_in, w_pool, pool_scale,
           w_o, norm2_g, w_up, w_down, final_g):
    seq = xp.shape[0]
    nseq, dec, _ = xs.shape
    past_len = page_table.shape[1] * cache_k.shape[2]

    g1 = norm1_g[l][None, :]
    g2 = norm2_g[l][None, :]
    win = w_in[l].astype(BF16)
    wqkvT = win[:, :3 * ATTN_WIDTH].T
    wu = win[:, 3 * ATTN_WIDTH:]
    wpool = w_pool[l].astype(BF16)
    pscale = pool_scale[l][None, :]
    wo = w_o[l].astype(BF16)
    wup = w_up[l].astype(BF16)
    wdown = w_down[l].astype(BF16)
    gf = final_g[None, :]

    cos, sin = _rope_angles(jnp.arange(seq, dtype=jnp.int32))
    qT, kT, vT, v_tiles, k_aug, kmean, pool_p, u_tail = _prompt_proj(
        xp, g1, wqkvT, wu, cos.T, sin.T, wpool, pscale)
    nb = seq // MOBA_BLOCK
    kmean_hm = kmean[:, 0, :].reshape(nb, N_HEADS, HEAD_DIM).transpose(1, 0, 2)
    attn_p = _prompt_attn(qT, k_aug, v_tiles, kmean_hm)
    yp = _out_mlp(xp, attn_p, pool_p, wo, g2, wup, wdown, gf, tl=512, tf=1024)
    k_prompt = kT.reshape(N_HEADS, HEAD_DIM, seq).transpose(2, 0, 1)[None]
    v_prompt = vT.reshape(N_HEADS, HEAD_DIM, seq).transpose(2, 0, 1)[None]
    pool_prompt = u_tail[HALO - POOL_STATE:][None]

    rows = nseq * dec
    pos_s = past_len + jnp.arange(dec, dtype=jnp.int32)
    c_s, s1_s, s2_s = _rope_tables(jnp.tile(pos_s, nseq))
    state = state_pool[l].astype(F32)
    state_pad = jnp.pad(state, ((0, 0), (HALO - POOL_STATE, 0), (0, 0)))
    q_s, k_s, v_s, u_s, pool_s = _sample_proj(
        xs.reshape(rows, D_MODEL), g1, win, c_s, s1_s, s2_s, state_pad, wpool, pscale)
    pad = ((0, 0), (0, MOBA_BLOCK - dec), (0, 0))
    kn_pad = jnp.pad(k_s.reshape(nseq, dec, ATTN_WIDTH), pad).astype(BF16)
    vn_pad = jnp.pad(v_s.reshape(nseq, dec, ATTN_WIDTH), pad).astype(BF16)
    cache_kT = jnp.transpose(cache_k[l], (0, 2, 3, 1))
    cache_vT = jnp.transpose(cache_v[l], (0, 2, 3, 1))
    attn_s = _sample_attn(page_table, q_s.reshape(nseq, dec, ATTN_WIDTH), kn_pad, vn_pad,
                          cache_kT, cache_vT)
    ys = _out_mlp(xs.reshape(rows, D_MODEL), attn_s.reshape(rows, ATTN_WIDTH).astype(BF16), pool_s,
                  wo, g2, wup, wdown, gf, tl=rows, tf=1024)
    k_sample = k_s.reshape(nseq, dec, N_HEADS, HEAD_DIM)
    v_sample = v_s.reshape(nseq, dec, N_HEADS, HEAD_DIM)
    u_ext = jnp.concatenate([state, u_s.reshape(nseq, dec, POOL_WIDTH)], axis=1)
    pool_sample = u_ext[:, u_ext.shape[1] - POOL_STATE:]
    return (yp, ys.reshape(nseq, dec, D_MODEL), k_prompt, v_prompt, pool_prompt,
            k_sample, v_sample, pool_sample)


def kernel(x_prompt, x_sample, cache_k, cache_v, state_pool, page_table, norm1_g, w_in, w_pool,
           pool_scale, w_o, norm2_g, w_up, w_down, final_g):
    depth = w_in.shape[0]
    assert depth == 1 and x_prompt.shape[0] == 1, "single layer, single prompt sequence"
    (yp, ys, k_p, v_p, pool_p, k_s, v_s, pool_s) = _layer(
        0, x_prompt[0], x_sample, cache_k, cache_v, state_pool, page_table, norm1_g, w_in, w_pool,
        pool_scale, w_o, norm2_g, w_up, w_down, final_g)
    return (yp[None], ys, k_p[None], v_p[None], pool_p[None], k_s[None], v_s[None], pool_s[None])
```

```python
import functools

import jax
import jax.numpy as jnp
from jax import lax
from jax.experimental import pallas as pl
from jax.experimental.pallas import tpu as pltpu

F32 = jnp.float32
BF16 = jnp.bfloat16

D_MODEL = 1024
HEAD_DIM = 64
N_HEADS = 8
ATTN_WIDTH = N_HEADS * HEAD_DIM
POOL_WINDOWS = (2, 4, 8, 16)
POOL_CH = 128
POOL_WIDTH = POOL_CH * len(POOL_WINDOWS)
POOL_STATE = max(POOL_WINDOWS) - 1
HALO = 16
D_FF = 4 * D_MODEL
MOBA_BLOCK = 256
MOBA_TOP_K = 3
ROPE_THETA = 500000.0
ROT_DIM = HEAD_DIM // 4
ROT_HALF = ROT_DIM // 2
ATTN_SCALE = HEAD_DIM ** -0.5
LOG2E = 1.4426950408889634
PROMPT_Q_SCALE = ATTN_SCALE * LOG2E
RMS_EPS = 1e-6
NEG = -1e30
LANES = 128
ATTN_TILE = 512
Q_TILE = 2 * ATTN_TILE
ATTN_PIECES = 2
V_ROWS = HEAD_DIM + 16
VMEM_LIMIT = 56 * 1024 * 1024

_NT = (((1,), (1,)), ((), ()))


def _dot(a, b):
    return jnp.dot(a, b, preferred_element_type=F32)


def _dot_nt(a, b):
    return lax.dot_general(a, b, _NT, preferred_element_type=F32)


def _split_bf16(x):
    hi = x.astype(BF16)
    lo = (x - hi.astype(F32)).astype(BF16)
    return hi, lo


def _dot3(a, b):
    a_hi, a_lo = _split_bf16(a)
    b_hi, b_lo = _split_bf16(b)
    return _dot(a_hi, b_hi) + _dot(a_hi, b_lo) + _dot(a_lo, b_hi)


def _rmsnorm(x, g):
    ms = jnp.mean(x * x, axis=-1, keepdims=True)
    return x * lax.rsqrt(ms + RMS_EPS) * g


def _top_k_mask(gate, valid, index, axis):
    limit = gate.shape[axis]
    g = jnp.where(valid, gate, -jnp.inf)
    sel = jnp.zeros(gate.shape, jnp.bool_)
    for _ in range(MOBA_TOP_K):
        mx = jnp.max(g, axis=axis, keepdims=True)
        first = jnp.min(jnp.where(g == mx, index, limit), axis=axis, keepdims=True)
        pick = jnp.logical_and(index == first, mx > -jnp.inf)
        sel = jnp.logical_or(sel, pick)
        g = jnp.where(pick, -jnp.inf, g)
    return sel


def _proj_kernel(x_ref, g_ref, wqkvT_ref, wu_ref, cosT_ref, sinT_ref, wpool_ref, pscale_ref,
                 qT_ref, kT_ref, vT_ref, vblk_ref, kaug_ref, kmean_ref, pool_ref, utail_ref,
                 ext_ref):
    i = pl.program_id(0)
    tl = x_ref.shape[0]
    hb = _rmsnorm(x_ref[...], g_ref[...]).astype(BF16)

    zT = _dot_nt(wqkvT_ref[...], hb)
    cosT = cosT_ref[...]
    sinT = sinT_ref[...]
    k_rows = []
    for hd in range(2 * N_HEADS):
        base = hd * HEAD_DIM
        x1 = zT[base:base + ROT_HALF]
        x2 = zT[base + ROT_HALF:base + ROT_DIM]
        rest = zT[base + ROT_DIM:base + HEAD_DIM]
        r1 = x1 * cosT - x2 * sinT
        r2 = x2 * cosT + x1 * sinT
        if hd < N_HEADS:
            qT_ref[base:base + ROT_HALF, :] = r1 * PROMPT_Q_SCALE
            qT_ref[base + ROT_HALF:base + ROT_DIM, :] = r2 * PROMPT_Q_SCALE
            qT_ref[base + ROT_DIM:base + HEAD_DIM, :] = rest * PROMPT_Q_SCALE
        else:
            k_rows += [r1, r2, rest]
    kT = jnp.concatenate(k_rows, axis=0)
    kT_ref[...] = kT
    vT = zT[2 * ATTN_WIDTH:3 * ATTN_WIDTH]
    vT_ref[...] = vT
    ones_rows = jnp.ones((V_ROWS - HEAD_DIM, tl), BF16)
    vT16 = vT.astype(BF16)
    vblk_ref[0] = jnp.concatenate(
        [piece for h in range(N_HEADS)
         for piece in (vT16[HEAD_DIM * h:HEAD_DIM * (h + 1)], ones_rows)], axis=0)

    k_rm = kT.T
    lane = lax.broadcasted_iota(jnp.int32, (tl, LANES), 1)
    block_tag = jnp.where(lane - HEAD_DIM == i, 1.0, 0.0)
    low = lane < HEAD_DIM
    for g in range(ATTN_WIDTH // LANES):
        kr = k_rm[:, LANES * g:LANES * (g + 1)]
        kaug_ref[2 * g] = jnp.where(low, kr, block_tag).astype(BF16)
        kaug_ref[2 * g + 1] = jnp.where(low, pltpu.roll(kr, HEAD_DIM, 1), block_tag).astype(BF16)
        ksum = jnp.sum(kr, axis=0, keepdims=True) * (1.0 / tl)
        kmean_ref[0, :, LANES * g:LANES * (g + 1)] = jnp.broadcast_to(ksum, (8, LANES))

    u = _dot(hb, wu_ref[...])

    @pl.when(i == 0)
    def _():
        ext_ref[0:HALO, :] = jnp.zeros((HALO, POOL_WIDTH), F32)

    ext_ref[HALO:HALO + tl, :] = u
    row = lax.broadcasted_iota(jnp.int32, (tl, 1), 0) + i * tl
    for g, w in enumerate(POOL_WINDOWS):
        lanes = slice(POOL_CH * g, POOL_CH * (g + 1))
        ug = ext_ref[HALO:HALO + tl, lanes]
        acc = ug
        for j in range(1, w):
            acc = acc + ext_ref[HALO - j:HALO - j + tl, lanes]
        cnt = jnp.minimum(row + 1, w).astype(F32)
        d = acc / cnt - ug
        y = _dot(d.astype(BF16), wpool_ref[g]) * pscale_ref[:, lanes]
        pool_ref[:, lanes] = y.astype(BF16)
    tail = ext_ref[tl:tl + HALO, :]
    ext_ref[0:HALO, :] = tail
    utail_ref[...] = tail


def _prompt_proj(x, g1, wqkvT, wu, cosT, sinT, wpool, pscale):
    seq = x.shape[0]
    tl = MOBA_BLOCK
    nb = seq // tl
    bpt = ATTN_TILE // tl
    const2 = lambda i: (0, 0)
    return pl.pallas_call(
        _proj_kernel,
        grid=(nb,),
        in_specs=[
            pl.BlockSpec((tl, D_MODEL), lambda i: (i, 0)),
            pl.BlockSpec((1, D_MODEL), const2),
            pl.BlockSpec((3 * ATTN_WIDTH, D_MODEL), const2),
            pl.BlockSpec((D_MODEL, POOL_WIDTH), const2),
            pl.BlockSpec((ROT_HALF, tl), lambda i: (0, i)),
            pl.BlockSpec((ROT_HALF, tl), lambda i: (0, i)),
            pl.BlockSpec((len(POOL_WINDOWS), POOL_CH, POOL_CH), lambda i: (0, 0, 0)),
            pl.BlockSpec((1, POOL_WIDTH), const2),
        ],
        out_specs=[
            pl.BlockSpec((ATTN_WIDTH, tl), lambda i: (0, i)),
            pl.BlockSpec((ATTN_WIDTH, tl), lambda i: (0, i)),
            pl.BlockSpec((ATTN_WIDTH, tl), lambda i: (0, i)),
            pl.BlockSpec((1, N_HEADS * V_ROWS, tl), lambda i: (i // bpt, 0, i % bpt)),
            pl.BlockSpec((N_HEADS, tl, LANES), lambda i: (0, i, 0)),
            pl.BlockSpec((1, 8, ATTN_WIDTH), lambda i: (i, 0, 0)),
            pl.BlockSpec((tl, POOL_WIDTH), lambda i: (i, 0)),
            pl.BlockSpec((HALO, POOL_WIDTH), const2),
        ],
        out_shape=[
            jax.ShapeDtypeStruct((ATTN_WIDTH, seq), F32),
            jax.ShapeDtypeStruct((ATTN_WIDTH, seq), F32),
            jax.ShapeDtypeStruct((ATTN_WIDTH, seq), F32),
            jax.ShapeDtypeStruct((nb // bpt, N_HEADS * V_ROWS, bpt * tl), BF16),
            jax.ShapeDtypeStruct((N_HEADS, seq, LANES), BF16),
            jax.ShapeDtypeStruct((nb, 8, ATTN_WIDTH), F32),
            jax.ShapeDtypeStruct((seq, POOL_WIDTH), BF16),
            jax.ShapeDtypeStruct((HALO, POOL_WIDTH), F32),
        ],
        scratch_shapes=[pltpu.VMEM((HALO + tl, POOL_WIDTH), F32)],
        compiler_params=pltpu.CompilerParams(dimension_semantics=("arbitrary",),
                                             vmem_limit_bytes=VMEM_LIMIT),
        name="prompt_proj",
    )(x, g1, wqkvT, wu, cosT, sinT, wpool, pscale)


def _attn_kernel(q_ref, k_ref, v_ref, km_ref, o_ref, s_ref, cmax_ref):
    tile_id = pl.program_id(1)
    nblk = km_ref.shape[1]
    tile = q_ref.shape[1]
    chunk = v_ref.shape[2]
    cpt = tile // chunk
    blk_idx = lax.broadcasted_iota(jnp.int32, (nblk, tile), 0)
    q_blk = (tile_id * (tile // MOBA_BLOCK)
             + lax.broadcasted_iota(jnp.int32, (nblk, tile), 1) // MOBA_BLOCK)
    key_idx = lax.broadcasted_iota(jnp.int32, (chunk, tile), 0)
    qry_idx = lax.broadcasted_iota(jnp.int32, (chunk, tile), 1)

    q_aug = []
    for hh in range(2):
        qT = q_ref[HEAD_DIM * hh:HEAD_DIM * (hh + 1), :]
        gate = _dot3(km_ref[hh], qT)
        sel = _top_k_mask(gate, blk_idx < q_blk, blk_idx, 0)
        bias = jnp.where(jnp.logical_or(sel, blk_idx == q_blk), 0.0, NEG)
        q_aug.append(jnp.concatenate([qT.astype(BF16), bias.astype(BF16)], axis=0))

    def scores(hh, c):
        start = pl.multiple_of(c * chunk, chunk)
        return _dot(k_ref[hh, pl.ds(start, chunk), :], q_aug[hh])

    def values(hh, c):
        return v_ref[c, V_ROWS * hh:V_ROWS * (hh + 1), :]

    def produce(slot, c, diag_offset=None, heads=(0, 1)):
        for hh in heads:
            s = scores(hh, c)
            if diag_offset is not None:
                s = jnp.where(key_idx + diag_offset <= qry_idx, s, NEG)
            s_ref[slot, hh] = s
            cmax_ref[slot, hh] = jnp.broadcast_to(jnp.max(s, axis=0, keepdims=True), (8, tile))

    def consume(slot, c, carry, heads=(0, 1)):
        out = []
        for hh in heads:
            m, acc = carry[2 * hh:2 * hh + 2]
            m_new = jnp.maximum(m, cmax_ref[slot, hh, 0:1, :])
            alpha = jnp.exp2(m - m_new)
            p = jnp.exp2(s_ref[slot, hh] - m_new).astype(BF16)
            out += [m_new, acc * alpha + _dot(values(hh, c), p)]
        return tuple(out)

    assert cpt == 2
    diag0 = tile_id * cpt
    init = (jnp.full((1, tile), NEG, F32), jnp.zeros((V_ROWS, tile), F32))
    produce(0, diag0, 0)

    def stage(p_slot, p_chunk, c_slot, c_chunk, carry, diag_offset=None):
        half = chunk // ATTN_PIECES
        m_new, acc, cmax = [], [], [None, None]
        for hh in range(2):
            m, a = carry[2 * hh:2 * hh + 2]
            m_new.append(jnp.maximum(m, cmax_ref[c_slot, hh, 0:1, :]))
            acc.append(a * jnp.exp2(m - m_new[hh]))
        for kh in range(ATTN_PIECES):
            for hh in range(2):
                start = pl.multiple_of(p_chunk * chunk + kh * half, half)
                s = _dot(k_ref[hh, pl.ds(start, half), :], q_aug[hh])
                if diag_offset is not None:
                    k_pos = lax.broadcasted_iota(jnp.int32, (half, tile), 0) + (diag_offset + kh * half)
                    s = jnp.where(k_pos <= lax.broadcasted_iota(jnp.int32, (half, tile), 1), s, NEG)
                s_ref[p_slot, hh, kh * half:(kh + 1) * half, :] = s
                part = jnp.max(s, axis=0, keepdims=True)
                cmax[hh] = part if cmax[hh] is None else jnp.maximum(cmax[hh], part)
                p = jnp.exp2(s_ref[c_slot, hh, kh * half:(kh + 1) * half, :] - m_new[hh]).astype(BF16)
                acc[hh] = acc[hh] + _dot(
                    v_ref[c_chunk, V_ROWS * hh:V_ROWS * (hh + 1), kh * half:(kh + 1) * half], p)
        for hh in range(2):
            cmax_ref[p_slot, hh] = jnp.broadcast_to(cmax[hh], (8, tile))
        return (m_new[0], acc[0], m_new[1], acc[1])

    def pair(i, carry):
        carry = stage(0, 2 * i, 1, jnp.where(i == 0, diag0 + 1, 2 * i - 1), carry)
        return stage(1, 2 * i + 1, 0, 2 * i, carry)

    state = stage(1, diag0 + 1, 0, diag0, init + init, diag_offset=chunk)
    state = lax.fori_loop(0, tile_id, pair, state)
    state = consume(1, jnp.where(tile_id == 0, diag0 + 1, diag0 - 1), state)
    oT = jnp.concatenate([acc[:HEAD_DIM] / acc[HEAD_DIM:HEAD_DIM + 1] for acc in (state[1], state[3])],
                         axis=0)
    o_ref[...] = oT.T.astype(BF16)


def _prompt_attn(qT, k_aug, v_tiles, kmean_hm):
    seq = qT.shape[1]
    nc, _, chunk = v_tiles.shape
    tile = Q_TILE
    nblk = kmean_hm.shape[1]
    return pl.pallas_call(
        _attn_kernel,
        grid=(N_HEADS // 2, seq // tile),
        in_specs=[
            pl.BlockSpec((2 * HEAD_DIM, tile), lambda p, t: (p, t)),
            pl.BlockSpec((2, seq, LANES), lambda p, t: (p, 0, 0)),
            pl.BlockSpec((nc, 2 * V_ROWS, chunk), lambda p, t: (0, p, 0)),
            pl.BlockSpec((2, nblk, HEAD_DIM), lambda p, t: (p, 0, 0)),
        ],
        out_specs=pl.BlockSpec((tile, 2 * HEAD_DIM), lambda p, t: (t, p)),
        out_shape=jax.ShapeDtypeStruct((seq, ATTN_WIDTH), BF16),
        scratch_shapes=[pltpu.VMEM((2, 2, chunk, tile), F32), pltpu.VMEM((2, 2, 8, tile), F32)],
        compiler_params=pltpu.CompilerParams(dimension_semantics=("arbitrary", "arbitrary"),
                                             vmem_limit_bytes=VMEM_LIMIT),
        name="prompt_attn",
    )(qT, k_aug, v_tiles, kmean_hm)


def _mlp_kernel(x_ref, attn_ref, pool_ref, wo_ref, g2_ref, wup_ref, wdown_ref, gf_ref, y_ref,
                x1_ref, hn_ref, acc_ref):
    j = pl.program_id(1)

    @pl.when(j == 0)
    def _():
        x1 = (x_ref[...] + _dot(attn_ref[...], wo_ref[0:ATTN_WIDTH, :])
              + _dot(pool_ref[...], wo_ref[ATTN_WIDTH:, :]))
        x1_ref[...] = x1
        hn_ref[...] = _rmsnorm(x1, g2_ref[...]).astype(BF16)
        acc_ref[...] = jnp.zeros(acc_ref.shape, F32)

    hid = jnp.maximum(_dot(hn_ref[...], wup_ref[...]), 0.0)
    acc_ref[...] += _dot((hid * hid).astype(BF16), wdown_ref[...])

    @pl.when(j == pl.num_programs(1) - 1)
    def _():
        y_ref[...] = _rmsnorm(x1_ref[...] + acc_ref[...], gf_ref[...])


def _out_mlp(x, attn, pool, wo, g2, wup, wdown, gf, tl, tf):
    rows = x.shape[0]
    const2 = lambda i, j: (0, 0)
    return pl.pallas_call(
        _mlp_kernel,
        grid=(rows // tl, D_FF // tf),
        in_specs=[
            pl.BlockSpec((tl, D_MODEL), lambda i, j: (i, 0)),
            pl.BlockSpec((tl, ATTN_WIDTH), lambda i, j: (i, 0)),
            pl.BlockSpec((tl, POOL_WIDTH), lambda i, j: (i, 0)),
            pl.BlockSpec((D_MODEL, D_MODEL), const2),
            pl.BlockSpec((1, D_MODEL), const2),
            pl.BlockSpec((D_MODEL, tf), lambda i, j: (0, j)),
            pl.BlockSpec((tf, D_MODEL), lambda i, j: (j, 0)),
            pl.BlockSpec((1, D_MODEL), const2),
        ],
        out_specs=pl.BlockSpec((tl, D_MODEL), lambda i, j: (i, 0)),
        out_shape=jax.ShapeDtypeStruct((rows, D_MODEL), F32),
        scratch_shapes=[pltpu.VMEM((tl, D_MODEL), F32), pltpu.VMEM((tl, D_MODEL), BF16),
                        pltpu.VMEM((tl, D_MODEL), F32)],
        compiler_params=pltpu.CompilerParams(dimension_semantics=("arbitrary", "arbitrary"),
                                             vmem_limit_bytes=VMEM_LIMIT),
        name="out_mlp",
    )(x, attn, pool, wo, g2, wup, wdown, gf)


def _sproj_kernel(x_ref, g_ref, win_ref, c_ref, s1_ref, s2_ref, state_ref, wpool_ref, pscale_ref,
                  q_ref, k_ref, v_ref, u_ref, pool_ref, ext_ref):
    rows = x_ref.shape[0]
    nseq, dec = state_ref.shape[0], rows // state_ref.shape[0]
    hb = _rmsnorm(x_ref[...], g_ref[...]).astype(BF16)
    z = _dot(hb, win_ref[...])
    for g in range(2 * ATTN_WIDTH // LANES):
        xg = z[:, LANES * g:LANES * (g + 1)]
        r = (xg * c_ref[...] + pltpu.roll(xg, ROT_HALF, 1) * s2_ref[...]
             + pltpu.roll(xg, LANES - ROT_HALF, 1) * s1_ref[...])
        if LANES * g < ATTN_WIDTH:
            q_ref[:, LANES * g:LANES * (g + 1)] = r * ATTN_SCALE
        else:
            k_ref[:, LANES * g - ATTN_WIDTH:LANES * (g + 1) - ATTN_WIDTH] = r
    v_ref[...] = z[:, 2 * ATTN_WIDTH:3 * ATTN_WIDTH]
    u = z[:, 3 * ATTN_WIDTH:]
    u_ref[...] = u

    ext_ref[:, 0:HALO, :] = state_ref[...]
    ext_ref[:, HALO:HALO + dec, :] = u.reshape(nseq, dec, POOL_WIDTH)
    for g, w in enumerate(POOL_WINDOWS):
        lanes = slice(POOL_CH * g, POOL_CH * (g + 1))
        ug = ext_ref[:, HALO:HALO + dec, lanes]
        acc = ug
        for j in range(1, w):
            acc = acc + ext_ref[:, HALO - j:HALO - j + dec, lanes]
        d = (acc / float(w) - ug).reshape(rows, POOL_CH)
        y = _dot(d.astype(BF16), wpool_ref[g]) * pscale_ref[:, lanes]
        pool_ref[:, lanes] = y.astype(BF16)


def _sample_proj(x, g1, win, c_tab, s1_tab, s2_tab, state_pad, wpool, pscale):
    rows = x.shape[0]
    nseq = state_pad.shape[0]
    shapes = [jax.ShapeDtypeStruct((rows, ATTN_WIDTH), F32)] * 3 + [
        jax.ShapeDtypeStruct((rows, POOL_WIDTH), F32), jax.ShapeDtypeStruct((rows, POOL_WIDTH), BF16)]
    return pl.pallas_call(
        _sproj_kernel,
        out_shape=shapes,
        scratch_shapes=[pltpu.VMEM((nseq, HALO + rows // nseq, POOL_WIDTH), F32)],
        compiler_params=pltpu.CompilerParams(vmem_limit_bytes=VMEM_LIMIT),
        name="sample_proj",
    )(x, g1, win, c_tab, s1_tab, s2_tab, state_pad, wpool, pscale)


GROUP = 16
K_BUF = 2 * GROUP
V_BUF = 4 * GROUP
K_PRIORITY = 0
V_PRIORITY = 1


def _dec_kernel(pt_ref, q_ref, kn_ref, vn_ref, ck_hbm, cv_hbm, o_ref,
                kbuf, vbuf, ksem, vsem, s_scr, p_scr, bias_scr):
    b = pl.program_id(0)
    n_pages = pt_ref.shape[1]
    page = kbuf.shape[-1]
    ppb = MOBA_BLOCK // page
    nblk = n_pages // ppb
    n_groups = n_pages // GROUP
    dec = q_ref.shape[1]
    rows = N_HEADS * dec

    def slot_of(g, e, n_buf):
        return (g % (n_buf // GROUP)) * GROUP + e

    def k_copy(j, slot, seq=b):
        return pltpu.make_async_copy(ck_hbm.at[pt_ref[seq, j]], kbuf.at[slot], ksem.at[slot])

    def v_copy(j, slot):
        return pltpu.make_async_copy(cv_hbm.at[pt_ref[b, j]], vbuf.at[slot], vsem.at[slot])

    @pl.when(b == 0)
    def _():
        for s in range(K_BUF):
            k_copy(s, s).start(priority=K_PRIORITY)

    for s in range(V_BUF):
        v_copy(s, s).start(priority=V_PRIORITY)

    r_idx = lax.broadcasted_iota(jnp.int32, (rows, ATTN_WIDTH), 0)
    c_idx = lax.broadcasted_iota(jnp.int32, (rows, ATTN_WIDTH), 1)
    head_mask = (c_idx // HEAD_DIM) == (r_idx // dec)
    q_bd = jnp.where(head_mask, jnp.concatenate([q_ref[0]] * N_HEADS, axis=0), 0.0)
    q_hi, q_lo = _split_bf16(q_bd)
    q_hilo = jnp.concatenate([q_hi, q_lo], axis=0)
    lane = lax.broadcasted_iota(jnp.int32, (rows, LANES), 1)

    def refill(copy, g, n_buf, priority):
        @pl.when((g + n_buf // GROUP) * GROUP < n_pages)
        def _():
            for e in range(GROUP):
                copy(g * GROUP + e + n_buf, slot_of(g, e, n_buf)).start(priority=priority)

    def block_of(buf, g, e, n_buf):
        return jnp.concatenate(
            [buf[slot_of(g, e + i, n_buf)].reshape(ATTN_WIDTH, page).astype(BF16) for i in range(ppb)],
            axis=1)

    def k_group(g, carry):
        for e in range(GROUP):
            k_copy(g * GROUP + e, slot_of(g, e, K_BUF)).wait()
        for e in range(0, GROUP, ppb):
            kt = block_of(kbuf, g, e, K_BUF)
            s2 = _dot(q_hilo, kt)
            s_scr[(g * GROUP + e) // ppb] = s2[:rows] + s2[rows:]
        refill(k_copy, g, K_BUF, K_PRIORITY)
        return carry

    lax.fori_loop(0, n_groups, k_group, 0)

    @pl.when(b + 1 < pl.num_programs(0))
    def _():
        for s in range(K_BUF):
            k_copy(s, s, b + 1).start(priority=K_PRIORITY)

    def gate_group(g, gate):
        for e in range(GROUP):
            n = g * GROUP + e
            gate = jnp.where(lane == n, jnp.sum(s_scr[n], axis=1, keepdims=True), gate)
        return gate

    gate = lax.fori_loop(0, nblk // GROUP, gate_group, jnp.zeros((rows, LANES), F32)) * (1.0 / MOBA_BLOCK)
    sel = _top_k_mask(gate, lane < nblk, lane, 1)
    bias = jnp.where(sel, 0.0, NEG)

    def bias_group(g, carry):
        for e in range(GROUP):
            n = g * GROUP + e
            col = jnp.sum(jnp.where(lane == n, bias, 0.0), axis=1, keepdims=True)
            bias_scr[n] = jnp.broadcast_to(col, (rows, MOBA_BLOCK))
        return carry

    lax.fori_loop(0, nblk // GROUP, bias_group, 0)

    lane_b = lax.broadcasted_iota(jnp.int32, (rows, MOBA_BLOCK), 1)
    q_pos = lax.broadcasted_iota(jnp.int32, (rows, MOBA_BLOCK), 0) % dec
    s_own = jnp.where(lane_b <= q_pos, _dot_nt(q_hi, kn_ref[0]), NEG)

    def max_group(g, mv):
        for e in range(GROUP):
            n = g * GROUP + e
            mv = jnp.maximum(mv, s_scr[n] + bias_scr[n])
        return mv

    m = jnp.max(lax.fori_loop(0, nblk // GROUP, max_group, s_own), axis=1, keepdims=True)
    p_own = jnp.exp(s_own - m)

    def exp_group(g, lv):
        for e in range(GROUP):
            n = g * GROUP + e
            p = jnp.exp(s_scr[n] + (bias_scr[n] - m))
            p_scr[n] = p.astype(BF16)
            lv = lv + p
        return lv

    l = jnp.sum(lax.fori_loop(0, nblk // GROUP, exp_group, p_own), axis=1, keepdims=True)

    def v_group(g, acc):
        for e in range(GROUP):
            v_copy(g * GROUP + e, slot_of(g, e, V_BUF)).wait()
        for e in range(0, GROUP, ppb):
            acc = acc + _dot_nt(p_scr[(g * GROUP + e) // ppb], block_of(vbuf, g, e, V_BUF))
        refill(v_copy, g, V_BUF, V_PRIORITY)
        return acc

    acc = lax.fori_loop(0, n_groups, v_group, _dot(p_own.astype(BF16), vn_ref[0]))
    o = jnp.where(head_mask, acc / l, 0.0)
    out = o[0:dec]
    for h in range(1, N_HEADS):
        out = out + o[h * dec:(h + 1) * dec]
    o_ref[0] = out


def _sample_attn(page_table, q_s, kn_pad, vn_pad, cache_kT, cache_vT):
    nseq, n_pages = page_table.shape
    dec = q_s.shape[1]
    page = cache_kT.shape[-1]
    nblk = n_pages * page // MOBA_BLOCK
    grid_spec = pltpu.PrefetchScalarGridSpec(
        num_scalar_prefetch=1,
        grid=(nseq,),
        in_specs=[
            pl.BlockSpec((1, dec, ATTN_WIDTH), lambda b, pt: (b, 0, 0)),
            pl.BlockSpec((1, MOBA_BLOCK, ATTN_WIDTH), lambda b, pt: (b, 0, 0)),
            pl.BlockSpec((1, MOBA_BLOCK, ATTN_WIDTH), lambda b, pt: (b, 0, 0)),
            pl.BlockSpec(memory_space=pl.ANY),
            pl.BlockSpec(memory_space=pl.ANY),
        ],
        out_specs=pl.BlockSpec((1, dec, ATTN_WIDTH), lambda b, pt: (b, 0, 0)),
        scratch_shapes=[
            pltpu.VMEM((K_BUF, N_HEADS, HEAD_DIM, page), F32),
            pltpu.VMEM((V_BUF, N_HEADS, HEAD_DIM, page), F32),
            pltpu.SemaphoreType.DMA((K_BUF,)),
            pltpu.SemaphoreType.DMA((V_BUF,)),
            pltpu.VMEM((nblk, N_HEADS * dec, MOBA_BLOCK), F32),
            pltpu.VMEM((nblk, N_HEADS * dec, MOBA_BLOCK), BF16),
            pltpu.VMEM((nblk, N_HEADS * dec, MOBA_BLOCK), F32),
        ],
    )
    return pl.pallas_call(
        _dec_kernel,
        grid_spec=grid_spec,
        out_shape=jax.ShapeDtypeStruct((nseq, dec, ATTN_WIDTH), F32),
        compiler_params=pltpu.CompilerParams(dimension_semantics=("arbitrary",),
                                             vmem_limit_bytes=VMEM_LIMIT),
        name="sample_attn",
    )(page_table, q_s, kn_pad, vn_pad, cache_kT, cache_vT)


def _rope_angles(pos):
    inv = jnp.power(ROPE_THETA, -jnp.arange(ROT_HALF, dtype=F32) * (2.0 / ROT_DIM))
    ang = pos.astype(F32)[:, None] * inv[None, :]
    return jnp.cos(ang), jnp.sin(ang)


def _rope_tables(pos):
    cos, sin = _rope_angles(pos)
    lane = jnp.arange(LANES) % HEAD_DIM
    f = lane % ROT_HALF
    c_tab = jnp.where(lane[None, :] < ROT_DIM, cos[:, f], 1.0)
    s1_tab = jnp.where(lane[None, :] < ROT_HALF, -sin[:, f], 0.0)
    s2_tab = jnp.where((lane[None, :] >= ROT_HALF) & (lane[None, :] < ROT_DIM), sin[:, f], 0.0)
    return c_tab, s1_tab, s2_tab


def _layer(l, xp, xs, cache_k, cache_v, state_pool, page_table, norm1_g, w_in, w_pool, pool_scale,
           w_o, norm2_g, w_up, w_down, final_g):
    seq = xp.shape[0]
    nseq, dec, _ = xs.shape
    past_len = page_table.shape[1] * cache_k.shape[2]

    g1 = norm1_g[l][None, :]
    g2 = norm2_g[l][None, :]
    win = w_in[l].astype(BF16)
    wqkvT = win[:, :3 * ATTN_WIDTH].T
    wu = win[:, 3 * ATTN_WIDTH:]
    wpool = w_pool[l].astype(BF16)
    pscale = pool_scale[l][None, :]
    wo = w_o[l].astype(BF16)
    wup = w_up[l].astype(BF16)
    wdown = w_down[l].astype(BF16)
    gf = final_g[None, :]

    cos, sin = _rope_angles(jnp.arange(seq, dtype=jnp.int32))
    qT, kT, vT, v_tiles, k_aug, kmean, pool_p, u_tail = _prompt_proj(
        xp, g1, wqkvT, wu, cos.T, sin.T, wpool, pscale)
    nb = seq // MOBA_BLOCK
    kmean_hm = kmean[:, 0, :].reshape(nb, N_HEADS, HEAD_DIM).transpose(1, 0, 2)
    attn_p = _prompt_attn(qT, k_aug, v_tiles, kmean_hm)
    yp = _out_mlp(xp, attn_p, pool_p, wo, g2, wup, wdown, gf, tl=512, tf=1024)
    k_prompt = kT.reshape(N_HEADS, HEAD_DIM, seq).transpose(2, 0, 1)[None]
    v_prompt = vT.reshape(N_HEADS, HEAD_DIM, seq).transpose(2, 0, 1)[None]
    pool_prompt = u_tail[HALO - POOL_STATE:][None]

    rows = nseq * dec
    pos_s = past_len + jnp.arange(dec, dtype=jnp.int32)
    c_s, s1_s, s2_s = _rope_tables(jnp.tile(pos_s, nseq))
    state = state_pool[l].astype(F32)
    state_pad = jnp.pad(state, ((0, 0), (HALO - POOL_STATE, 0), (0, 0)))
    q_s, k_s, v_s, u_s, pool_s = _sample_proj(
        xs.reshape(rows, D_MODEL), g1, win, c_s, s1_s, s2_s, state_pad, wpool, pscale)
    pad = ((0, 0), (0, MOBA_BLOCK - dec), (0, 0))
    kn_pad = jnp.pad(k_s.reshape(nseq, dec, ATTN_WIDTH), pad).astype(BF16)
    vn_pad = jnp.pad(v_s.reshape(nseq, dec, ATTN_WIDTH), pad).astype(BF16)
    cache_kT = jnp.transpose(cache_k[l], (0, 2, 3, 1))
    cache_vT = jnp.transpose(cache_v[l], (0, 2, 3, 1))
    attn_s = _sample_attn(page_table, q_s.reshape(nseq, dec, ATTN_WIDTH), kn_pad, vn_pad,
                          cache_kT, cache_vT)
    ys = _out_mlp(xs.reshape(rows, D_MODEL), attn_s.reshape(rows, ATTN_WIDTH).astype(BF16), pool_s,
                  wo, g2, wup, wdown, gf, tl=rows, tf=1024)
    k_sample = k_s.reshape(nseq, dec, N_HEADS, HEAD_DIM)
    v_sample = v_s.reshape(nseq, dec, N_HEADS, HEAD_DIM)
    u_ext = jnp.concatenate([state, u_s.reshape(nseq, dec, POOL_WIDTH)], axis=1)
    pool_sample = u_ext[:, u_ext.shape[1] - POOL_STATE:]
    return (yp, ys.reshape(nseq, dec, D_MODEL), k_prompt, v_prompt, pool_prompt,
            k_sample, v_sample, pool_sample)


def kernel(x_prompt, x_sample, cache_k, cache_v, state_pool, page_table, norm1_g, w_in, w_pool,
           pool_scale, w_o, norm2_g, w_up, w_down, final_g):
    depth = w_in.shape[0]
    assert depth == 1 and x_prompt.shape[0] == 1, "single layer, single prompt sequence"
    (yp, ys, k_p, v_p, pool_p, k_s, v_s, pool_s) = _layer(
        0, x_prompt[0], x_sample, cache_k, cache_v, state_pool, page_table, norm1_g, w_in, w_pool,
        pool_scale, w_o, norm2_g, w_up, w_down, final_g)
    return (yp[None], ys, k_p[None], v_p[None], pool_p[None], k_s[None], v_s[None], pool_s[None])
```

```python
import functools

import jax
import jax.numpy as jnp
from jax import lax
from jax.experimental import pallas as pl
from jax.experimental.pallas import tpu as pltpu

F32 = jnp.float32
BF16 = jnp.bfloat16

D_MODEL = 1024
HEAD_DIM = 64
N_HEADS = 8
ATTN_WIDTH = N_HEADS * HEAD_DIM
POOL_WINDOWS = (2, 4, 8, 16)
POOL_CH = 128
POOL_WIDTH = POOL_CH * len(POOL_WINDOWS)
POOL_STATE = max(POOL_WINDOWS) - 1
HALO = 16
D_FF = 4 * D_MODEL
MOBA_BLOCK = 256
MOBA_TOP_K = 3
ROPE_THETA = 500000.0
ROT_DIM = HEAD_DIM // 4
ROT_HALF = ROT_DIM // 2
ATTN_SCALE = HEAD_DIM ** -0.5
LOG2E = 1.4426950408889634
PROMPT_Q_SCALE = ATTN_SCALE * LOG2E
RMS_EPS = 1e-6
NEG = -1e30
LANES = 128
ATTN_TILE = 512
Q_TILE = 2 * ATTN_TILE
ATTN_PIECES = 2
V_ROWS = HEAD_DIM + 16
VMEM_LIMIT = 56 * 1024 * 1024

_NT = (((1,), (1,)), ((), ()))


def _dot(a, b):
    return jnp.dot(a, b, preferred_element_type=F32)


def _dot_nt(a, b):
    return lax.dot_general(a, b, _NT, preferred_element_type=F32)


def _split_bf16(x):
    hi = x.astype(BF16)
    lo = (x - hi.astype(F32)).astype(BF16)
    return hi, lo


def _dot3(a, b):
    a_hi, a_lo = _split_bf16(a)
    b_hi, b_lo = _split_bf16(b)
    return _dot(a_hi, b_hi) + _dot(a_hi, b_lo) + _dot(a_lo, b_hi)


def _rmsnorm(x, g):
    ms = jnp.mean(x * x, axis=-1, keepdims=True)
    return x * lax.rsqrt(ms + RMS_EPS) * g


def _top_k_mask(gate, valid, index, axis):
    limit = gate.shape[axis]
    g = jnp.where(valid, gate, -jnp.inf)
    sel = jnp.zeros(gate.shape, jnp.bool_)
    for _ in range(MOBA_TOP_K):
        mx = jnp.max(g, axis=axis, keepdims=True)
        first = jnp.min(jnp.where(g == mx, index, limit), axis=axis, keepdims=True)
        pick = jnp.logical_and(index == first, mx > -jnp.inf)
        sel = jnp.logical_or(sel, pick)
        g = jnp.where(pick, -jnp.inf, g)
    return sel


def _proj_kernel(x_ref, g_ref, wqkvT_ref, wu_ref, cosT_ref, sinT_ref, wpool_ref, pscale_ref,
                 qT_ref, kT_ref, vT_ref, vblk_ref, kaug_ref, kmean_ref, pool_ref, utail_ref,
                 ext_ref):
    i = pl.program_id(0)
    tl = x_ref.shape[0]
    hb = _rmsnorm(x_ref[...], g_ref[...]).astype(BF16)

    zT = _dot_nt(wqkvT_ref[...], hb)
    cosT = cosT_ref[...]
    sinT = sinT_ref[...]
    k_rows = []
    for hd in range(2 * N_HEADS):
        base = hd * HEAD_DIM
        x1 = zT[base:base + ROT_HALF]
        x2 = zT[base + ROT_HALF:base + ROT_DIM]
        rest = zT[base + ROT_DIM:base + HEAD_DIM]
        r1 = x1 * cosT - x2 * sinT
        r2 = x2 * cosT + x1 * sinT
        if hd < N_HEADS:
            qT_ref[base:base + ROT_HALF, :] = r1 * PROMPT_Q_SCALE
            qT_ref[base + ROT_HALF:base + ROT_DIM, :] = r2 * PROMPT_Q_SCALE
            qT_ref[base + ROT_DIM:base + HEAD_DIM, :] = rest * PROMPT_Q_SCALE
        else:
            k_rows += [r1, r2, rest]
    kT = jnp.concatenate(k_rows, axis=0)
    kT_ref[...] = kT
    vT = zT[2 * ATTN_WIDTH:3 * ATTN_WIDTH]
    vT_ref[...] = vT
    ones_rows = jnp.ones((V_ROWS - HEAD_DIM, tl), BF16)
    vT16 = vT.astype(BF16)
    vblk_ref[0] = jnp.concatenate(
        [piece for h in range(N_HEADS)
         for piece in (vT16[HEAD_DIM * h:HEAD_DIM * (h + 1)], ones_rows)], axis=0)

    k_rm = kT.T
    lane = lax.broadcasted_iota(jnp.int32, (tl, LANES), 1)
    block_tag = jnp.where(lane - HEAD_DIM == i, 1.0, 0.0)
    low = lane < HEAD_DIM
    for g in range(ATTN_WIDTH // LANES):
        kr = k_rm[:, LANES * g:LANES * (g + 1)]
        kaug_ref[2 * g] = jnp.where(low, kr, block_tag).astype(BF16)
        kaug_ref[2 * g + 1] = jnp.where(low, pltpu.roll(kr, HEAD_DIM, 1), block_tag).astype(BF16)
        ksum = jnp.sum(kr, axis=0, keepdims=True) * (1.0 / tl)
        kmean_ref[0, :, LANES * g:LANES * (g + 1)] = jnp.broadcast_to(ksum, (8, LANES))

    u = _dot(hb, wu_ref[...])

    @pl.when(i == 0)
    def _():
        ext_ref[0:HALO, :] = jnp.zeros((HALO, POOL_WIDTH), F32)

    ext_ref[HALO:HALO + tl, :] = u
    row = lax.broadcasted_iota(jnp.int32, (tl, 1), 0) + i * tl
    for g, w in enumerate(POOL_WINDOWS):
        lanes = slice(POOL_CH * g, POOL_CH * (g + 1))
        ug = ext_ref[HALO:HALO + tl, lanes]
        acc = ug
        for j in range(1, w):
            acc = acc + ext_ref[HALO - j:HALO - j + tl, lanes]
        cnt = jnp.minimum(row + 1, w).astype(F32)
        d = acc / cnt - ug
        y = _dot(d.astype(BF16), wpool_ref[g]) * pscale_ref[:, lanes]
        pool_ref[:, lanes] = y.astype(BF16)
    tail = ext_ref[tl:tl + HALO, :]
    ext_ref[0:HALO, :] = tail
    utail_ref[...] = tail


def _prompt_proj(x, g1, wqkvT, wu, cosT, sinT, wpool, pscale):
    seq = x.shape[0]
    tl = MOBA_BLOCK
    nb = seq // tl
    bpt = ATTN_TILE // tl
    const2 = lambda i: (0, 0)
    return pl.pallas_call(
        _proj_kernel,
        grid=(nb,),
        in_specs=[
            pl.BlockSpec((tl, D_MODEL), lambda i: (i, 0)),
            pl.BlockSpec((1, D_MODEL), const2),
            pl.BlockSpec((3 * ATTN_WIDTH, D_MODEL), const2),
            pl.BlockSpec((D_MODEL, POOL_WIDTH), const2),
            pl.BlockSpec((ROT_HALF, tl), lambda i: (0, i)),
            pl.BlockSpec((ROT_HALF, tl), lambda i: (0, i)),
            pl.BlockSpec((len(POOL_WINDOWS), POOL_CH, POOL_CH), lambda i: (0, 0, 0)),
            pl.BlockSpec((1, POOL_WIDTH), const2),
        ],
        out_specs=[
            pl.BlockSpec((ATTN_WIDTH, tl), lambda i: (0, i)),
            pl.BlockSpec((ATTN_WIDTH, tl), lambda i: (0, i)),
            pl.BlockSpec((ATTN_WIDTH, tl), lambda i: (0, i)),
            pl.BlockSpec((1, N_HEADS * V_ROWS, tl), lambda i: (i // bpt, 0, i % bpt)),
            pl.BlockSpec((N_HEADS, tl, LANES), lambda i: (0, i, 0)),
            pl.BlockSpec((1, 8, ATTN_WIDTH), lambda i: (i, 0, 0)),
            pl.BlockSpec((tl, POOL_WIDTH), lambda i: (i, 0)),
            pl.BlockSpec((HALO, POOL_WIDTH), const2),
        ],
        out_shape=[
            jax.ShapeDtypeStruct((ATTN_WIDTH, seq), F32),
            jax.ShapeDtypeStruct((ATTN_WIDTH, seq), F32),
            jax.ShapeDtypeStruct((ATTN_WIDTH, seq), F32),
            jax.ShapeDtypeStruct((nb // bpt, N_HEADS * V_ROWS, bpt * tl), BF16),
            jax.ShapeDtypeStruct((N_HEADS, seq, LANES), BF16),
            jax.ShapeDtypeStruct((nb, 8, ATTN_WIDTH), F32),
            jax.ShapeDtypeStruct((seq, POOL_WIDTH), BF16),
            jax.ShapeDtypeStruct((HALO, POOL_WIDTH), F32),
        ],
        scratch_shapes=[pltpu.VMEM((HALO + tl, POOL_WIDTH), F32)],
        compiler_params=pltpu.CompilerParams(dimension_semantics=("arbitrary",),
                                             vmem_limit_bytes=VMEM_LIMIT),
        name="prompt_proj",
    )(x, g1, wqkvT, wu, cosT, sinT, wpool, pscale)


def _attn_kernel(q_ref, k_ref, v_ref, km_ref, o_ref, s_ref, cmax_ref):
    tile_id = pl.program_id(1)
    nblk = km_ref.shape[1]
    tile = q_ref.shape[1]
    chunk = v_ref.shape[2]
    cpt = tile // chunk
    blk_idx = lax.broadcasted_iota(jnp.int32, (nblk, tile), 0)
    q_blk = (tile_id * (tile // MOBA_BLOCK)
             + lax.broadcasted_iota(jnp.int32, (nblk, tile), 1) // MOBA_BLOCK)
    key_idx = lax.broadcasted_iota(jnp.int32, (chunk, tile), 0)
    qry_idx = lax.broadcasted_iota(jnp.int32, (chunk, tile), 1)

    q_aug = [None, None]

    def select_blocks(hh):
        qT = q_ref[HEAD_DIM * hh:HEAD_DIM * (hh + 1), :]
        gate = _dot3(km_ref[hh], qT)
        sel = _top_k_mask(gate, blk_idx < q_blk, blk_idx, 0)
        bias = jnp.where(jnp.logical_or(sel, blk_idx == q_blk), 0.0, NEG)
        q_aug[hh] = jnp.concatenate([qT.astype(BF16), bias.astype(BF16)], axis=0)

    def scores(hh, c):
        start = pl.multiple_of(c * chunk, chunk)
        return _dot(k_ref[hh, pl.ds(start, chunk), :], q_aug[hh])

    def values(hh, c):
        return v_ref[c, V_ROWS * hh:V_ROWS * (hh + 1), :]

    def produce(slot, c, diag_offset=None, heads=(0, 1)):
        for hh in heads:
            s = scores(hh, c)
            if diag_offset is not None:
                s = jnp.where(key_idx + diag_offset <= qry_idx, s, NEG)
            s_ref[slot, hh] = s
            cmax_ref[slot, hh] = jnp.broadcast_to(jnp.max(s, axis=0, keepdims=True), (8, tile))

    def consume(slot, c, carry, heads=(0, 1)):
        out = []
        for hh in heads:
            m, acc = carry[2 * hh:2 * hh + 2]
            m_new = jnp.maximum(m, cmax_ref[slot, hh, 0:1, :])
            alpha = jnp.exp2(m - m_new)
            p = jnp.exp2(s_ref[slot, hh] - m_new).astype(BF16)
            out += [m_new, acc * alpha + _dot(values(hh, c), p)]
        return tuple(out)

    assert cpt == 2
    diag0 = tile_id * cpt
    init = (jnp.full((1, tile), NEG, F32), jnp.zeros((V_ROWS, tile), F32))
    for hh in range(2):
        select_blocks(hh)
    produce(0, diag0, 0)

    def stage(p_slot, p_chunk, c_slot, c_chunk, carry, diag_offset=None):
        half = chunk // ATTN_PIECES
        m_new, acc, cmax = [], [], [None, None]
        for hh in range(2):
            m, a = carry[2 * hh:2 * hh + 2]
            m_new.append(jnp.maximum(m, cmax_ref[c_slot, hh, 0:1, :]))
            acc.append(a * jnp.exp2(m - m_new[hh]))
        for kh in range(ATTN_PIECES):
            for hh in range(2):
                start = pl.multiple_of(p_chunk * chunk + kh * half, half)
                s = _dot(k_ref[hh, pl.ds(start, half), :], q_aug[hh])
                if diag_offset is not None:
                    k_pos = lax.broadcasted_iota(jnp.int32, (half, tile), 0) + (diag_offset + kh * half)
                    s = jnp.where(k_pos <= lax.broadcasted_iota(jnp.int32, (half, tile), 1), s, NEG)
                s_ref[p_slot, hh, kh * half:(kh + 1) * half, :] = s
                part = jnp.max(s, axis=0, keepdims=True)
                cmax[hh] = part if cmax[hh] is None else jnp.maximum(cmax[hh], part)
                p = jnp.exp2(s_ref[c_slot, hh, kh * half:(kh + 1) * half, :] - m_new[hh]).astype(BF16)
                acc[hh] = acc[hh] + _dot(
                    v_ref[c_chunk, V_ROWS * hh:V_ROWS * (hh + 1), kh * half:(kh + 1) * half], p)
        for hh in range(2):
            cmax_ref[p_slot, hh] = jnp.broadcast_to(cmax[hh], (8, tile))
        return (m_new[0], acc[0], m_new[1], acc[1])

    def pair(i, carry):
        carry = stage(0, 2 * i, 1, jnp.where(i == 0, diag0 + 1, 2 * i - 1), carry)
        return stage(1, 2 * i + 1, 0, 2 * i, carry)

    state = stage(1, diag0 + 1, 0, diag0, init + init, diag_offset=chunk)
    state = lax.fori_loop(0, tile_id, pair, state)
    state = consume(1, jnp.where(tile_id == 0, diag0 + 1, diag0 - 1), state)
    oT = jnp.concatenate([acc[:HEAD_DIM] / acc[HEAD_DIM:HEAD_DIM + 1] for acc in (state[1], state[3])],
                         axis=0)
    o_ref[...] = oT.T.astype(BF16)


def _prompt_attn(qT, k_aug, v_tiles, kmean_hm):
    seq = qT.shape[1]
    nc, _, chunk = v_tiles.shape
    tile = Q_TILE
    nblk = kmean_hm.shape[1]
    return pl.pallas_call(
        _attn_kernel,
        grid=(N_HEADS // 2, seq // tile),
        in_specs=[
            pl.BlockSpec((2 * HEAD_DIM, tile), lambda p, t: (p, t)),
            pl.BlockSpec((2, seq, LANES), lambda p, t: (p, 0, 0)),
            pl.BlockSpec((nc, 2 * V_ROWS, chunk), lambda p, t: (0, p, 0)),
            pl.BlockSpec((2, nblk, HEAD_DIM), lambda p, t: (p, 0, 0)),
        ],
        out_specs=pl.BlockSpec((tile, 2 * HEAD_DIM), lambda p, t: (t, p)),
        out_shape=jax.ShapeDtypeStruct((seq, ATTN_WIDTH), BF16),
        scratch_shapes=[pltpu.VMEM((2, 2, chunk, tile), F32), pltpu.VMEM((2, 2, 8, tile), F32)],
        compiler_params=pltpu.CompilerParams(dimension_semantics=("arbitrary", "arbitrary"),
                                             vmem_limit_bytes=VMEM_LIMIT),
        name="prompt_attn",
    )(qT, k_aug, v_tiles, kmean_hm)


def _mlp_kernel(x_ref, attn_ref, pool_ref, wo_ref, g2_ref, wup_ref, wdown_ref, gf_ref, y_ref,
                x1_ref, hn_ref, acc_ref):
    j = pl.program_id(1)

    @pl.when(j == 0)
    def _():
        x1 = (x_ref[...] + _dot(attn_ref[...], wo_ref[0:ATTN_WIDTH, :])
              + _dot(pool_ref[...], wo_ref[ATTN_WIDTH:, :]))
        x1_ref[...] = x1
        hn_ref[...] = _rmsnorm(x1, g2_ref[...]).astype(BF16)
        acc_ref[...] = jnp.zeros(acc_ref.shape, F32)

    hid = jnp.maximum(_dot(hn_ref[...], wup_ref[...]), 0.0)
    acc_ref[...] += _dot((hid * hid).astype(BF16), wdown_ref[...])

    @pl.when(j == pl.num_programs(1) - 1)
    def _():
        y_ref[...] = _rmsnorm(x1_ref[...] + acc_ref[...], gf_ref[...])


def _out_mlp(x, attn, pool, wo, g2, wup, wdown, gf, tl, tf):
    rows = x.shape[0]
    const2 = lambda i, j: (0, 0)
    return pl.pallas_call(
        _mlp_kernel,
        grid=(rows // tl, D_FF // tf),
        in_specs=[
            pl.BlockSpec((tl, D_MODEL), lambda i, j: (i, 0)),
            pl.BlockSpec((tl, ATTN_WIDTH), lambda i, j: (i, 0)),
            pl.BlockSpec((tl, POOL_WIDTH), lambda i, j: (i, 0)),
            pl.BlockSpec((D_MODEL, D_MODEL), const2),
            pl.BlockSpec((1, D_MODEL), const2),
            pl.BlockSpec((D_MODEL, tf), lambda i, j: (0, j)),
            pl.BlockSpec((tf, D_MODEL), lambda i, j: (j, 0)),
            pl.BlockSpec((1, D_MODEL), const2),
        ],
        out_specs=pl.BlockSpec((tl, D_MODEL), lambda i, j: (i, 0)),
        out_shape=jax.ShapeDtypeStruct((rows, D_MODEL), F32),
        scratch_shapes=[pltpu.VMEM((tl, D_MODEL), F32), pltpu.VMEM((tl, D_MODEL), BF16),
                        pltpu.VMEM((tl, D_MODEL), F32)],
        compiler_params=pltpu.CompilerParams(dimension_semantics=("arbitrary", "arbitrary"),
                                             vmem_limit_bytes=VMEM_LIMIT),
        name="out_mlp",
    )(x, attn, pool, wo, g2, wup, wdown, gf)


def _sproj_kernel(x_ref, g_ref, win_ref, c_ref, s1_ref, s2_ref, state_ref, wpool_ref, pscale_ref,
                  q_ref, k_ref, v_ref, u_ref, pool_ref, ext_ref):
    rows = x_ref.shape[0]
    nseq, dec = state_ref.shape[0], rows // state_ref.shape[0]
    hb = _rmsnorm(x_ref[...], g_ref[...]).astype(BF16)
    z = _dot(hb, win_ref[...])
    for g in range(2 * ATTN_WIDTH // LANES):
        xg = z[:, LANES * g:LANES * (g + 1)]
        r = (xg * c_ref[...] + pltpu.roll(xg, ROT_HALF, 1) * s2_ref[...]
             + pltpu.roll(xg, LANES - ROT_HALF, 1) * s1_ref[...])
        if LANES * g < ATTN_WIDTH:
            q_ref[:, LANES * g:LANES * (g + 1)] = r * ATTN_SCALE
        else:
            k_ref[:, LANES * g - ATTN_WIDTH:LANES * (g + 1) - ATTN_WIDTH] = r
    v_ref[...] = z[:, 2 * ATTN_WIDTH:3 * ATTN_WIDTH]
    u = z[:, 3 * ATTN_WIDTH:]
    u_ref[...] = u

    ext_ref[:, 0:HALO, :] = state_ref[...]
    ext_ref[:, HALO:HALO + dec, :] = u.reshape(nseq, dec, POOL_WIDTH)
    for g, w in enumerate(POOL_WINDOWS):
        lanes = slice(POOL_CH * g, POOL_CH * (g + 1))
        ug = ext_ref[:, HALO:HALO + dec, lanes]
        acc = ug
        for j in range(1, w):
            acc = acc + ext_ref[:, HALO - j:HALO - j + dec, lanes]
        d = (acc / float(w) - ug).reshape(rows, POOL_CH)
        y = _dot(d.astype(BF16), wpool_ref[g]) * pscale_ref[:, lanes]
        pool_ref[:, lanes] = y.astype(BF16)


def _sample_proj(x, g1, win, c_tab, s1_tab, s2_tab, state_pad, wpool, pscale):
    rows = x.shape[0]
    nseq = state_pad.shape[0]
    shapes = [jax.ShapeDtypeStruct((rows, ATTN_WIDTH), F32)] * 3 + [
        jax.ShapeDtypeStruct((rows, POOL_WIDTH), F32), jax.ShapeDtypeStruct((rows, POOL_WIDTH), BF16)]
    return pl.pallas_call(
        _sproj_kernel,
        out_shape=shapes,
        scratch_shapes=[pltpu.VMEM((nseq, HALO + rows // nseq, POOL_WIDTH), F32)],
        compiler_params=pltpu.CompilerParams(vmem_limit_bytes=VMEM_LIMIT),
        name="sample_proj",
    )(x, g1, win, c_tab, s1_tab, s2_tab, state_pad, wpool, pscale)


GROUP = 16
K_BUF = 4 * GROUP
V_BUF = 4 * GROUP
K_PRIORITY = 0
V_PRIORITY = 1


def _dec_kernel(pt_ref, q_ref, kn_ref, vn_ref, ck_hbm, cv_hbm, o_ref,
                kbuf, vbuf, ksem, vsem, s_scr, p_scr, bias_scr):
    b = pl.program_id(0)
    n_pages = pt_ref.shape[1]
    page = kbuf.shape[-1]
    ppb = MOBA_BLOCK // page
    nblk = n_pages // ppb
    n_groups = n_pages // GROUP
    dec = q_ref.shape[1]
    rows = N_HEADS * dec

    def slot_of(g, e, n_buf):
        return (g % (n_buf // GROUP)) * GROUP + e

    def k_copy(j, slot, seq=b):
        return pltpu.make_async_copy(ck_hbm.at[pt_ref[seq, j]], kbuf.at[slot], ksem.at[slot])

    def v_copy(j, slot):
        return pltpu.make_async_copy(cv_hbm.at[pt_ref[b, j]], vbuf.at[slot], vsem.at[slot])

    @pl.when(b == 0)
    def _():
        for s in range(K_BUF):
            k_copy(s, s).start(priority=K_PRIORITY)

    for s in range(V_BUF):
        v_copy(s, s).start(priority=V_PRIORITY)

    r_idx = lax.broadcasted_iota(jnp.int32, (rows, ATTN_WIDTH), 0)
    c_idx = lax.broadcasted_iota(jnp.int32, (rows, ATTN_WIDTH), 1)
    head_mask = (c_idx // HEAD_DIM) == (r_idx // dec)
    q_bd = jnp.where(head_mask, jnp.concatenate([q_ref[0]] * N_HEADS, axis=0), 0.0)
    q_hi, q_lo = _split_bf16(q_bd)
    q_hilo = jnp.concatenate([q_hi, q_lo], axis=0)
    lane = lax.broadcasted_iota(jnp.int32, (rows, LANES), 1)

    def refill(copy, g, n_buf, priority):
        @pl.when((g + n_buf // GROUP) * GROUP < n_pages)
        def _():
            for e in range(GROUP):
                copy(g * GROUP + e + n_buf, slot_of(g, e, n_buf)).start(priority=priority)

    def block_of(buf, g, e, n_buf):
        return jnp.concatenate(
            [buf[slot_of(g, e + i, n_buf)].reshape(ATTN_WIDTH, page).astype(BF16) for i in range(ppb)],
            axis=1)

    def k_group(g, carry):
        for e in range(GROUP):
            k_copy(g * GROUP + e, slot_of(g, e, K_BUF)).wait()
        for e in range(0, GROUP, ppb):
            kt = block_of(kbuf, g, e, K_BUF)
            s2 = _dot(q_hilo, kt)
            s_scr[(g * GROUP + e) // ppb] = s2[:rows] + s2[rows:]
        refill(k_copy, g, K_BUF, K_PRIORITY)
        return carry

    lax.fori_loop(0, n_groups, k_group, 0)

    @pl.when(b + 1 < pl.num_programs(0))
    def _():
        for s in range(K_BUF):
            k_copy(s, s, b + 1).start(priority=K_PRIORITY)

    def gate_group(g, gate):
        for e in range(GROUP):
            n = g * GROUP + e
            gate = jnp.where(lane == n, jnp.sum(s_scr[n], axis=1, keepdims=True), gate)
        return gate

    gate = lax.fori_loop(0, nblk // GROUP, gate_group, jnp.zeros((rows, LANES), F32)) * (1.0 / MOBA_BLOCK)
    sel = _top_k_mask(gate, lane < nblk, lane, 1)
    bias = jnp.where(sel, 0.0, NEG)

    def bias_group(g, carry):
        for e in range(GROUP):
            n = g * GROUP + e
            col = jnp.sum(jnp.where(lane == n, bias, 0.0), axis=1, keepdims=True)
            bias_scr[n] = jnp.broadcast_to(col, (rows, MOBA_BLOCK))
        return carry

    lax.fori_loop(0, nblk // GROUP, bias_group, 0)

    lane_b = lax.broadcasted_iota(jnp.int32, (rows, MOBA_BLOCK), 1)
    q_pos = lax.broadcasted_iota(jnp.int32, (rows, MOBA_BLOCK), 0) % dec
    s_own = jnp.where(lane_b <= q_pos, _dot_nt(q_hi, kn_ref[0]), NEG)

    def max_group(g, mv):
        for e in range(GROUP):
            n = g * GROUP + e
            mv = jnp.maximum(mv, s_scr[n] + bias_scr[n])
        return mv

    m = jnp.max(lax.fori_loop(0, nblk // GROUP, max_group, s_own), axis=1, keepdims=True)
    p_own = jnp.exp(s_own - m)

    def exp_group(g, lv):
        for e in range(GROUP):
            n = g * GROUP + e
            p = jnp.exp(s_scr[n] + (bias_scr[n] - m))
            p_scr[n] = p.astype(BF16)
            lv = lv + p
        return lv

    l = jnp.sum(lax.fori_loop(0, nblk // GROUP, exp_group, p_own), axis=1, keepdims=True)

    def v_group(g, acc):
        for e in range(GROUP):
            v_copy(g * GROUP + e, slot_of(g, e, V_BUF)).wait()
        for e in range(0, GROUP, ppb):
            acc = acc + _dot_nt(p_scr[(g * GROUP + e) // ppb], block_of(vbuf, g, e, V_BUF))
        refill(v_copy, g, V_BUF, V_PRIORITY)
        return acc

    acc = lax.fori_loop(0, n_groups, v_group, _dot(p_own.astype(BF16), vn_ref[0]))
    o = jnp.where(head_mask, acc / l, 0.0)
    out = o[0:dec]
    for h in range(1, N_HEADS):
        out = out + o[h * dec:(h + 1) * dec]
    o_ref[0] = out


def _sample_attn(page_table, q_s, kn_pad, vn_pad, cache_kT, cache_vT):
    nseq, n_pages = page_table.shape
    dec = q_s.shape[1]
    page = cache_kT.shape[-1]
    nblk = n_pages * page // MOBA_BLOCK
    grid_spec = pltpu.PrefetchScalarGridSpec(
        num_scalar_prefetch=1,
        grid=(nseq,),
        in_specs=[
            pl.BlockSpec((1, dec, ATTN_WIDTH), lambda b, pt: (b, 0, 0)),
            pl.BlockSpec((1, MOBA_BLOCK, ATTN_WIDTH), lambda b, pt: (b, 0, 0)),
            pl.BlockSpec((1, MOBA_BLOCK, ATTN_WIDTH), lambda b, pt: (b, 0, 0)),
            pl.BlockSpec(memory_space=pl.ANY),
            pl.BlockSpec(memory_space=pl.ANY),
        ],
        out_specs=pl.BlockSpec((1, dec, ATTN_WIDTH), lambda b, pt: (b, 0, 0)),
        scratch_shapes=[
            pltpu.VMEM((K_BUF, N_HEADS, HEAD_DIM, page), F32),
            pltpu.VMEM((V_BUF, N_HEADS, HEAD_DIM, page), F32),
            pltpu.SemaphoreType.DMA((K_BUF,)),
            pltpu.SemaphoreType.DMA((V_BUF,)),
            pltpu.VMEM((nblk, N_HEADS * dec, MOBA_BLOCK), F32),
            pltpu.VMEM((nblk, N_HEADS * dec, MOBA_BLOCK), BF16),
            pltpu.VMEM((nblk, N_HEADS * dec, MOBA_BLOCK), F32),
        ],
    )
    return pl.pallas_call(
        _dec_kernel,
        grid_spec=grid_spec,
        out_shape=jax.ShapeDtypeStruct((nseq, dec, ATTN_WIDTH), F32),
        compiler_params=pltpu.CompilerParams(dimension_semantics=("arbitrary",),
                                             vmem_limit_bytes=VMEM_LIMIT),
        name="sample_attn",
    )(page_table, q_s, kn_pad, vn_pad, cache_kT, cache_vT)


def _rope_angles(pos):
    inv = jnp.power(ROPE_THETA, -jnp.arange(ROT_HALF, dtype=F32) * (2.0 / ROT_DIM))
    ang = pos.astype(F32)[:, None] * inv[None, :]
    return jnp.cos(ang), jnp.sin(ang)


def _rope_tables(pos):
    cos, sin = _rope_angles(pos)
    lane = jnp.arange(LANES) % HEAD_DIM
    f = lane % ROT_HALF
    c_tab = jnp.where(lane[None, :] < ROT_DIM, cos[:, f], 1.0)
    s1_tab = jnp.where(lane[None, :] < ROT_HALF, -sin[:, f], 0.0)
    s2_tab = jnp.where((lane[None, :] >= ROT_HALF) & (lane[None, :] < ROT_DIM), sin[:, f], 0.0)
    return c_tab, s1_tab, s2_tab


def _layer(l, xp, xs, cache_k, cache_v, state_pool, page_table, norm1_g, w_in, w_pool, pool_scale,
           w_o, norm2_g, w_up, w_down, final_g):
    seq = xp.shape[0]
    nseq, dec, _ = xs.shape
    past_len = page_table.shape[1] * cache_k.shape[2]

    g1 = norm1_g[l][None, :]
    g2 = norm2_g[l][None, :]
    win = w_in[l].astype(BF16)
    wqkvT = win[:, :3 * ATTN_WIDTH].T
    wu = win[:, 3 * ATTN_WIDTH:]
    wpool = w_pool[l].astype(BF16)
    pscale = pool_scale[l][None, :]
    wo = w_o[l].astype(BF16)
    wup = w_up[l].astype(BF16)
    wdown = w_down[l].astype(BF16)
    gf = final_g[None, :]

    cos, sin = _rope_angles(jnp.arange(seq, dtype=jnp.int32))
    qT, kT, vT, v_tiles, k_aug, kmean, pool_p, u_tail = _prompt_proj(
        xp, g1, wqkvT, wu, cos.T, sin.T, wpool, pscale)
    nb = seq // MOBA_BLOCK
    kmean_hm = kmean[:, 0, :].reshape(nb, N_HEADS, HEAD_DIM).transpose(1, 0, 2)
    attn_p = _prompt_attn(qT, k_aug, v_tiles, kmean_hm)
    yp = _out_mlp(xp, attn_p, pool_p, wo, g2, wup, wdown, gf, tl=512, tf=1024)
    k_prompt = kT.reshape(N_HEADS, HEAD_DIM, seq).transpose(2, 0, 1)[None]
    v_prompt = vT.reshape(N_HEADS, HEAD_DIM, seq).transpose(2, 0, 1)[None]
    pool_prompt = u_tail[HALO - POOL_STATE:][None]

    rows = nseq * dec
    pos_s = past_len + jnp.arange(dec, dtype=jnp.int32)
    c_s, s1_s, s2_s = _rope_tables(jnp.tile(pos_s, nseq))
    state = state_pool[l].astype(F32)
    state_pad = jnp.pad(state, ((0, 0), (HALO - POOL_STATE, 0), (0, 0)))
    q_s, k_s, v_s, u_s, pool_s = _sample_proj(
        xs.reshape(rows, D_MODEL), g1, win, c_s, s1_s, s2_s, state_pad, wpool, pscale)
    pad = ((0, 0), (0, MOBA_BLOCK - dec), (0, 0))
    kn_pad = jnp.pad(k_s.reshape(nseq, dec, ATTN_WIDTH), pad).astype(BF16)
    vn_pad = jnp.pad(v_s.reshape(nseq, dec, ATTN_WIDTH), pad).astype(BF16)
    cache_kT = jnp.transpose(cache_k[l], (0, 2, 3, 1))
    cache_vT = jnp.transpose(cache_v[l], (0, 2, 3, 1))
    attn_s = _sample_attn(page_table, q_s.reshape(nseq, dec, ATTN_WIDTH), kn_pad, vn_pad,
                          cache_kT, cache_vT)
    ys = _out_mlp(xs.reshape(rows, D_MODEL), attn_s.reshape(rows, ATTN_WIDTH).astype(BF16), pool_s,
                  wo, g2, wup, wdown, gf, tl=rows, tf=1024)
    k_sample = k_s.reshape(nseq, dec, N_HEADS, HEAD_DIM)
    v_sample = v_s.reshape(nseq, dec, N_HEADS, HEAD_DIM)
    u_ext = jnp.concatenate([state, u_s.reshape(nseq, dec, POOL_WIDTH)], axis=1)
    pool_sample = u_ext[:, u_ext.shape[1] - POOL_STATE:]
    return (yp, ys.reshape(nseq, dec, D_MODEL), k_prompt, v_prompt, pool_prompt,
            k_sample, v_sample, pool_sample)


def kernel(x_prompt, x_sample, cache_k, cache_v, state_pool, page_table, norm1_g, w_in, w_pool,
           pool_scale, w_o, norm2_g, w_up, w_down, final_g):
    depth = w_in.shape[0]
    assert depth == 1 and x_prompt.shape[0] == 1, "single layer, single prompt sequence"
    (yp, ys, k_p, v_p, pool_p, k_s, v_s, pool_s) = _layer(
        0, x_prompt[0], x_sample, cache_k, cache_v, state_pool, page_table, norm1_g, w_in, w_pool,
        pool_scale, w_o, norm2_g, w_up, w_down, final_g)
    return (yp[None], ys, k_p[None], v_p[None], pool_p[None], k_s[None], v_s[None], pool_s[None])
```

```python
import functools

import jax
import jax.numpy as jnp
from jax import lax
from jax.experimental import pallas as pl
from jax.experimental.pallas import tpu as pltpu

F32 = jnp.float32
BF16 = jnp.bfloat16

D_MODEL = 1024
HEAD_DIM = 64
N_HEADS = 8
ATTN_WIDTH = N_HEADS * HEAD_DIM
POOL_WINDOWS = (2, 4, 8, 16)
POOL_CH = 128
POOL_WIDTH = POOL_CH * len(POOL_WINDOWS)
POOL_STATE = max(POOL_WINDOWS) - 1
HALO = 16
D_FF = 4 * D_MODEL
MOBA_BLOCK = 256
MOBA_TOP_K = 3
ROPE_THETA = 500000.0
ROT_DIM = HEAD_DIM // 4
ROT_HALF = ROT_DIM // 2
ATTN_SCALE = HEAD_DIM ** -0.5
LOG2E = 1.4426950408889634
PROMPT_Q_SCALE = ATTN_SCALE * LOG2E
RMS_EPS = 1e-6
NEG = -1e30
LANES = 128
ATTN_TILE = 512
Q_TILE = 2 * ATTN_TILE
ATTN_PIECES = 2
V_ROWS = HEAD_DIM + 16
VMEM_LIMIT = 56 * 1024 * 1024

_NT = (((1,), (1,)), ((), ()))


def _dot(a, b):
    return jnp.dot(a, b, preferred_element_type=F32)


def _dot_nt(a, b):
    return lax.dot_general(a, b, _NT, preferred_element_type=F32)


def _split_bf16(x):
    hi = x.astype(BF16)
    lo = (x - hi.astype(F32)).astype(BF16)
    return hi, lo


def _dot3(a, b):
    a_hi, a_lo = _split_bf16(a)
    b_hi, b_lo = _split_bf16(b)
    return _dot(a_hi, b_hi) + _dot(a_hi, b_lo) + _dot(a_lo, b_hi)


def _rmsnorm(x, g):
    ms = jnp.mean(x * x, axis=-1, keepdims=True)
    return x * lax.rsqrt(ms + RMS_EPS) * g


def _top_k_mask(gate, valid, index, axis):
    limit = gate.shape[axis]
    g = jnp.where(valid, gate, -jnp.inf)
    sel = jnp.zeros(gate.shape, jnp.bool_)
    for _ in range(MOBA_TOP_K):
        mx = jnp.max(g, axis=axis, keepdims=True)
        first = jnp.min(jnp.where(g == mx, index, limit), axis=axis, keepdims=True)
        pick = jnp.logical_and(index == first, mx > -jnp.inf)
        sel = jnp.logical_or(sel, pick)
        g = jnp.where(pick, -jnp.inf, g)
    return sel


def _proj_kernel(x_ref, g_ref, wqkvT_ref, wu_ref, cosT_ref, sinT_ref, wpool_ref, pscale_ref,
                 qT_ref, kT_ref, vT_ref, vblk_ref, kaug_ref, kmean_ref, pool_ref, utail_ref,
                 ext_ref):
    i = pl.program_id(0)
    tl = x_ref.shape[0]
    hb = _rmsnorm(x_ref[...], g_ref[...]).astype(BF16)

    zT = _dot_nt(wqkvT_ref[...], hb)
    cosT = cosT_ref[...]
    sinT = sinT_ref[...]
    k_rows = []
    for hd in range(2 * N_HEADS):
        base = hd * HEAD_DIM
        x1 = zT[base:base + ROT_HALF]
        x2 = zT[base + ROT_HALF:base + ROT_DIM]
        rest = zT[base + ROT_DIM:base + HEAD_DIM]
        r1 = x1 * cosT - x2 * sinT
        r2 = x2 * cosT + x1 * sinT
        if hd < N_HEADS:
            qT_ref[base:base + ROT_HALF, :] = r1 * PROMPT_Q_SCALE
            qT_ref[base + ROT_HALF:base + ROT_DIM, :] = r2 * PROMPT_Q_SCALE
            qT_ref[base + ROT_DIM:base + HEAD_DIM, :] = rest * PROMPT_Q_SCALE
        else:
            k_rows += [r1, r2, rest]
    kT = jnp.concatenate(k_rows, axis=0)
    kT_ref[...] = kT
    vT = zT[2 * ATTN_WIDTH:3 * ATTN_WIDTH]
    vT_ref[...] = vT
    ones_rows = jnp.ones((V_ROWS - HEAD_DIM, tl), BF16)
    vT16 = vT.astype(BF16)
    vblk_ref[0] = jnp.concatenate(
        [piece for h in range(N_HEADS)
         for piece in (vT16[HEAD_DIM * h:HEAD_DIM * (h + 1)], ones_rows)], axis=0)

    k_rm = kT.T
    lane = lax.broadcasted_iota(jnp.int32, (tl, LANES), 1)
    block_tag = jnp.where(lane - HEAD_DIM == i, 1.0, 0.0)
    low = lane < HEAD_DIM
    for g in range(ATTN_WIDTH // LANES):
        kr = k_rm[:, LANES * g:LANES * (g + 1)]
        kaug_ref[2 * g] = jnp.where(low, kr, block_tag).astype(BF16)
        kaug_ref[2 * g + 1] = jnp.where(low, pltpu.roll(kr, HEAD_DIM, 1), block_tag).astype(BF16)
        ksum = jnp.sum(kr, axis=0, keepdims=True) * (1.0 / tl)
        kmean_ref[0, :, LANES * g:LANES * (g + 1)] = jnp.broadcast_to(ksum, (8, LANES))

    u = _dot(hb, wu_ref[...])

    @pl.when(i == 0)
    def _():
        ext_ref[0:HALO, :] = jnp.zeros((HALO, POOL_WIDTH), F32)

    ext_ref[HALO:HALO + tl, :] = u
    row = lax.broadcasted_iota(jnp.int32, (tl, 1), 0) + i * tl
    for g, w in enumerate(POOL_WINDOWS):
        lanes = slice(POOL_CH * g, POOL_CH * (g + 1))
        ug = ext_ref[HALO:HALO + tl, lanes]
        acc = ug
        for j in range(1, w):
            acc = acc + ext_ref[HALO - j:HALO - j + tl, lanes]
        cnt = jnp.minimum(row + 1, w).astype(F32)
        d = acc / cnt - ug
        y = _dot(d.astype(BF16), wpool_ref[g]) * pscale_ref[:, lanes]
        pool_ref[:, lanes] = y.astype(BF16)
    tail = ext_ref[tl:tl + HALO, :]
    ext_ref[0:HALO, :] = tail
    utail_ref[...] = tail


def _prompt_proj(x, g1, wqkvT, wu, cosT, sinT, wpool, pscale):
    seq = x.shape[0]
    tl = MOBA_BLOCK
    nb = seq // tl
    bpt = ATTN_TILE // tl
    const2 = lambda i: (0, 0)
    return pl.pallas_call(
        _proj_kernel,
        grid=(nb,),
        in_specs=[
            pl.BlockSpec((tl, D_MODEL), lambda i: (i, 0)),
            pl.BlockSpec((1, D_MODEL), const2),
            pl.BlockSpec((3 * ATTN_WIDTH, D_MODEL), const2),
            pl.BlockSpec((D_MODEL, POOL_WIDTH), const2),
            pl.BlockSpec((ROT_HALF, tl), lambda i: (0, i)),
            pl.BlockSpec((ROT_HALF, tl), lambda i: (0, i)),
            pl.BlockSpec((len(POOL_WINDOWS), POOL_CH, POOL_CH), lambda i: (0, 0, 0)),
            pl.BlockSpec((1, POOL_WIDTH), const2),
        ],
        out_specs=[
            pl.BlockSpec((ATTN_WIDTH, tl), lambda i: (0, i)),
            pl.BlockSpec((ATTN_WIDTH, tl), lambda i: (0, i)),
            pl.BlockSpec((ATTN_WIDTH, tl), lambda i: (0, i)),
            pl.BlockSpec((1, N_HEADS * V_ROWS, tl), lambda i: (i // bpt, 0, i % bpt)),
            pl.BlockSpec((N_HEADS, tl, LANES), lambda i: (0, i, 0)),
            pl.BlockSpec((1, 8, ATTN_WIDTH), lambda i: (i, 0, 0)),
            pl.BlockSpec((tl, POOL_WIDTH), lambda i: (i, 0)),
            pl.BlockSpec((HALO, POOL_WIDTH), const2),
        ],
        out_shape=[
            jax.ShapeDtypeStruct((ATTN_WIDTH, seq), F32),
            jax.ShapeDtypeStruct((ATTN_WIDTH, seq), F32),
            jax.ShapeDtypeStruct((ATTN_WIDTH, seq), F32),
            jax.ShapeDtypeStruct((nb // bpt, N_HEADS * V_ROWS, bpt * tl), BF16),
            jax.ShapeDtypeStruct((N_HEADS, seq, LANES), BF16),
            jax.ShapeDtypeStruct((nb, 8, ATTN_WIDTH), F32),
            jax.ShapeDtypeStruct((seq, POOL_WIDTH), BF16),
            jax.ShapeDtypeStruct((HALO, POOL_WIDTH), F32),
        ],
        scratch_shapes=[pltpu.VMEM((HALO + tl, POOL_WIDTH), F32)],
        compiler_params=pltpu.CompilerParams(dimension_semantics=("arbitrary",),
                                             vmem_limit_bytes=VMEM_LIMIT),
        name="prompt_proj",
    )(x, g1, wqkvT, wu, cosT, sinT, wpool, pscale)


def _attn_kernel(q_ref, k_ref, v_ref, km_ref, o_ref, s_ref, cmax_ref):
    tile_id = pl.program_id(1)
    nblk = km_ref.shape[1]
    tile = q_ref.shape[1]
    chunk = v_ref.shape[2]
    cpt = tile // chunk
    blk_idx = lax.broadcasted_iota(jnp.int32, (nblk, tile), 0)
    q_blk = (tile_id * (tile // MOBA_BLOCK)
             + lax.broadcasted_iota(jnp.int32, (nblk, tile), 1) // MOBA_BLOCK)
    key_idx = lax.broadcasted_iota(jnp.int32, (chunk, tile), 0)
    qry_idx = lax.broadcasted_iota(jnp.int32, (chunk, tile), 1)

    q_aug = [None, None]

    def select_blocks(hh):
        qT = q_ref[HEAD_DIM * hh:HEAD_DIM * (hh + 1), :]
        gate = _dot3(km_ref[hh], qT)
        sel = _top_k_mask(gate, blk_idx < q_blk, blk_idx, 0)
        bias = jnp.where(jnp.logical_or(sel, blk_idx == q_blk), 0.0, NEG)
        q_aug[hh] = jnp.concatenate([qT.astype(BF16), bias.astype(BF16)], axis=0)

    def scores(hh, c):
        start = pl.multiple_of(c * chunk, chunk)
        return _dot(k_ref[hh, pl.ds(start, chunk), :], q_aug[hh])

    def values(hh, c):
        return v_ref[c, V_ROWS * hh:V_ROWS * (hh + 1), :]

    def produce(slot, c, diag_offset=None, heads=(0, 1)):
        for hh in heads:
            s = scores(hh, c)
            if diag_offset is not None:
                s = jnp.where(key_idx + diag_offset <= qry_idx, s, NEG)
            s_ref[slot, hh] = s
            cmax_ref[slot, hh] = jnp.broadcast_to(jnp.max(s, axis=0, keepdims=True), (8, tile))

    def consume(slot, c, carry, heads=(0, 1)):
        out = []
        for hh in heads:
            m, acc = carry[2 * hh:2 * hh + 2]
            m_new = jnp.maximum(m, cmax_ref[slot, hh, 0:1, :])
            alpha = jnp.exp2(m - m_new)
            p = jnp.exp2(s_ref[slot, hh] - m_new).astype(BF16)
            out += [m_new, acc * alpha + _dot(values(hh, c), p)]
        return tuple(out)

    assert cpt == 2
    diag0 = tile_id * cpt
    init = (jnp.full((1, tile), NEG, F32), jnp.zeros((V_ROWS, tile), F32))
    for hh in range(2):
        select_blocks(hh)
    produce(0, diag0, 0)

    def stage(p_slot, p_chunk, c_slot, c_chunk, carry, diag_offset=None):
        half = chunk // ATTN_PIECES
        m_new, acc, cmax = [], [], [None, None]
        for hh in range(2):
            m, a = carry[2 * hh:2 * hh + 2]
            m_new.append(jnp.maximum(m, cmax_ref[c_slot, hh, 0:1, :]))
            acc.append(a * jnp.exp2(m - m_new[hh]))
        for kh in range(ATTN_PIECES):
            for hh in range(2):
                start = pl.multiple_of(p_chunk * chunk + kh * half, half)
                if diag_offset is None:
                    s = _dot(k_ref[hh, pl.ds(start, half), :], q_aug[hh])
                else:
                    live = tile - diag_offset
                    s = _dot(k_ref[hh, pl.ds(start, half), :], q_aug[hh][:, diag_offset:])
                    k_pos = lax.broadcasted_iota(jnp.int32, (half, live), 0) + kh * half
                    s = jnp.where(k_pos <= lax.broadcasted_iota(jnp.int32, (half, live), 1), s, NEG)
                    s = jnp.concatenate([jnp.full((half, diag_offset), NEG, F32), s], axis=1)
                s_ref[p_slot, hh, kh * half:(kh + 1) * half, :] = s
                part = jnp.max(s, axis=0, keepdims=True)
                cmax[hh] = part if cmax[hh] is None else jnp.maximum(cmax[hh], part)
                p = jnp.exp2(s_ref[c_slot, hh, kh * half:(kh + 1) * half, :] - m_new[hh]).astype(BF16)
                acc[hh] = acc[hh] + _dot(
                    v_ref[c_chunk, V_ROWS * hh:V_ROWS * (hh + 1), kh * half:(kh + 1) * half], p)
        for hh in range(2):
            cmax_ref[p_slot, hh] = jnp.broadcast_to(cmax[hh], (8, tile))
        return (m_new[0], acc[0], m_new[1], acc[1])

    def pair(i, carry):
        carry = stage(0, 2 * i, 1, jnp.where(i == 0, diag0 + 1, 2 * i - 1), carry)
        return stage(1, 2 * i + 1, 0, 2 * i, carry)

    state = stage(1, diag0 + 1, 0, diag0, init + init, diag_offset=chunk)
    state = lax.fori_loop(0, tile_id, pair, state)
    state = consume(1, jnp.where(tile_id == 0, diag0 + 1, diag0 - 1), state)
    oT = jnp.concatenate([acc[:HEAD_DIM] / acc[HEAD_DIM:HEAD_DIM + 1] for acc in (state[1], state[3])],
                         axis=0)
    o_ref[...] = oT.T.astype(BF16)


def _prompt_attn(qT, k_aug, v_tiles, kmean_hm):
    seq = qT.shape[1]
    nc, _, chunk = v_tiles.shape
    tile = Q_TILE
    nblk = kmean_hm.shape[1]
    return pl.pallas_call(
        _attn_kernel,
        grid=(N_HEADS // 2, seq // tile),
        in_specs=[
            pl.BlockSpec((2 * HEAD_DIM, tile), lambda p, t: (p, t)),
            pl.BlockSpec((2, seq, LANES), lambda p, t: (p, 0, 0)),
            pl.BlockSpec((nc, 2 * V_ROWS, chunk), lambda p, t: (0, p, 0)),
            pl.BlockSpec((2, nblk, HEAD_DIM), lambda p, t: (p, 0, 0)),
        ],
        out_specs=pl.BlockSpec((tile, 2 * HEAD_DIM), lambda p, t: (t, p)),
        out_shape=jax.ShapeDtypeStruct((seq, ATTN_WIDTH), BF16),
        scratch_shapes=[pltpu.VMEM((2, 2, chunk, tile), F32), pltpu.VMEM((2, 2, 8, tile), F32)],
        compiler_params=pltpu.CompilerParams(dimension_semantics=("arbitrary", "arbitrary"),
                                             vmem_limit_bytes=VMEM_LIMIT),
        name="prompt_attn",
    )(qT, k_aug, v_tiles, kmean_hm)


def _mlp_kernel(x_ref, attn_ref, pool_ref, wo_ref, g2_ref, wup_ref, wdown_ref, gf_ref, y_ref,
                x1_ref, hn_ref, acc_ref):
    j = pl.program_id(1)

    @pl.when(j == 0)
    def _():
        x1 = (x_ref[...] + _dot(attn_ref[...], wo_ref[0:ATTN_WIDTH, :])
              + _dot(pool_ref[...], wo_ref[ATTN_WIDTH:, :]))
        x1_ref[...] = x1
        hn_ref[...] = _rmsnorm(x1, g2_ref[...]).astype(BF16)
        acc_ref[...] = jnp.zeros(acc_ref.shape, F32)

    hid = jnp.maximum(_dot(hn_ref[...], wup_ref[...]), 0.0)
    acc_ref[...] += _dot((hid * hid).astype(BF16), wdown_ref[...])

    @pl.when(j == pl.num_programs(1) - 1)
    def _():
        y_ref[...] = _rmsnorm(x1_ref[...] + acc_ref[...], gf_ref[...])


def _out_mlp(x, attn, pool, wo, g2, wup, wdown, gf, tl, tf):
    rows = x.shape[0]
    const2 = lambda i, j: (0, 0)
    return pl.pallas_call(
        _mlp_kernel,
        grid=(rows // tl, D_FF // tf),
        in_specs=[
            pl.BlockSpec((tl, D_MODEL), lambda i, j: (i, 0)),
            pl.BlockSpec((tl, ATTN_WIDTH), lambda i, j: (i, 0)),
            pl.BlockSpec((tl, POOL_WIDTH), lambda i, j: (i, 0)),
            pl.BlockSpec((D_MODEL, D_MODEL), const2),
            pl.BlockSpec((1, D_MODEL), const2),
            pl.BlockSpec((D_MODEL, tf), lambda i, j: (0, j)),
            pl.BlockSpec((tf, D_MODEL), lambda i, j: (j, 0)),
            pl.BlockSpec((1, D_MODEL), const2),
        ],
        out_specs=pl.BlockSpec((tl, D_MODEL), lambda i, j: (i, 0)),
        out_shape=jax.ShapeDtypeStruct((rows, D_MODEL), F32),
        scratch_shapes=[pltpu.VMEM((tl, D_MODEL), F32), pltpu.VMEM((tl, D_MODEL), BF16),
                        pltpu.VMEM((tl, D_MODEL), F32)],
        compiler_params=pltpu.CompilerParams(dimension_semantics=("arbitrary", "arbitrary"),
                                             vmem_limit_bytes=VMEM_LIMIT),
        name="out_mlp",
    )(x, attn, pool, wo, g2, wup, wdown, gf)


def _sproj_kernel(x_ref, g_ref, win_ref, c_ref, s1_ref, s2_ref, state_ref, wpool_ref, pscale_ref,
                  q_ref, k_ref, v_ref, u_ref, pool_ref, ext_ref):
    rows = x_ref.shape[0]
    nseq, dec = state_ref.shape[0], rows // state_ref.shape[0]
    hb = _rmsnorm(x_ref[...], g_ref[...]).astype(BF16)
    z = _dot(hb, win_ref[...])
    for g in range(2 * ATTN_WIDTH // LANES):
        xg = z[:, LANES * g:LANES * (g + 1)]
        r = (xg * c_ref[...] + pltpu.roll(xg, ROT_HALF, 1) * s2_ref[...]
             + pltpu.roll(xg, LANES - ROT_HALF, 1) * s1_ref[...])
        if LANES * g < ATTN_WIDTH:
            q_ref[:, LANES * g:LANES * (g + 1)] = r * ATTN_SCALE
        else:
            k_ref[:, LANES * g - ATTN_WIDTH:LANES * (g + 1) - ATTN_WIDTH] = r
    v_ref[...] = z[:, 2 * ATTN_WIDTH:3 * ATTN_WIDTH]
    u = z[:, 3 * ATTN_WIDTH:]
    u_ref[...] = u

    ext_ref[:, 0:HALO, :] = state_ref[...]
    ext_ref[:, HALO:HALO + dec, :] = u.reshape(nseq, dec, POOL_WIDTH)
    for g, w in enumerate(POOL_WINDOWS):
        lanes = slice(POOL_CH * g, POOL_CH * (g + 1))
        ug = ext_ref[:, HALO:HALO + dec, lanes]
        acc = ug
        for j in range(1, w):
            acc = acc + ext_ref[:, HALO - j:HALO - j + dec, lanes]
        d = (acc / float(w) - ug).reshape(rows, POOL_CH)
        y = _dot(d.astype(BF16), wpool_ref[g]) * pscale_ref[:, lanes]
        pool_ref[:, lanes] = y.astype(BF16)


def _sample_proj(x, g1, win, c_tab, s1_tab, s2_tab, state_pad, wpool, pscale):
    rows = x.shape[0]
    nseq = state_pad.shape[0]
    shapes = [jax.ShapeDtypeStruct((rows, ATTN_WIDTH), F32)] * 3 + [
        jax.ShapeDtypeStruct((rows, POOL_WIDTH), F32), jax.ShapeDtypeStruct((rows, POOL_WIDTH), BF16)]
    return pl.pallas_call(
        _sproj_kernel,
        out_shape=shapes,
        scratch_shapes=[pltpu.VMEM((nseq, HALO + rows // nseq, POOL_WIDTH), F32)],
        compiler_params=pltpu.CompilerParams(vmem_limit_bytes=VMEM_LIMIT),
        name="sample_proj",
    )(x, g1, win, c_tab, s1_tab, s2_tab, state_pad, wpool, pscale)


GROUP = 16
K_BUF = 4 * GROUP
V_BUF = 4 * GROUP
K_PRIORITY = 0
V_PRIORITY = 1


def _dec_kernel(pt_ref, q_ref, kn_ref, vn_ref, ck_hbm, cv_hbm, o_ref,
                kbuf, vbuf, ksem, vsem, s_scr, p_scr, bias_scr):
    b = pl.program_id(0)
    n_pages = pt_ref.shape[1]
    page = kbuf.shape[-1]
    ppb = MOBA_BLOCK // page
    nblk = n_pages // ppb
    n_groups = n_pages // GROUP
    dec = q_ref.shape[1]
    rows = N_HEADS * dec

    def slot_of(g, e, n_buf):
        return (g % (n_buf // GROUP)) * GROUP + e

    def k_copy(j, slot, seq=b):
        return pltpu.make_async_copy(ck_hbm.at[pt_ref[seq, j]], kbuf.at[slot], ksem.at[slot])

    def v_copy(j, slot):
        return pltpu.make_async_copy(cv_hbm.at[pt_ref[b, j]], vbuf.at[slot], vsem.at[slot])

    @pl.when(b == 0)
    def _():
        for s in range(K_BUF):
            k_copy(s, s).start(priority=K_PRIORITY)

    for s in range(V_BUF):
        v_copy(s, s).start(priority=V_PRIORITY)

    r_idx = lax.broadcasted_iota(jnp.int32, (rows, ATTN_WIDTH), 0)
    c_idx = lax.broadcasted_iota(jnp.int32, (rows, ATTN_WIDTH), 1)
    head_mask = (c_idx // HEAD_DIM) == (r_idx // dec)
    q_bd = jnp.where(head_mask, jnp.concatenate([q_ref[0]] * N_HEADS, axis=0), 0.0)
    q_hi, q_lo = _split_bf16(q_bd)
    q_hilo = jnp.concatenate([q_hi, q_lo], axis=0)
    lane = lax.broadcasted_iota(jnp.int32, (rows, LANES), 1)

    def refill(copy, g, n_buf, priority):
        @pl.when((g + n_buf // GROUP) * GROUP < n_pages)
        def _():
            for e in range(GROUP):
                copy(g * GROUP + e + n_buf, slot_of(g, e, n_buf)).start(priority=priority)

    def block_of(buf, g, e, n_buf):
        return jnp.concatenate(
            [buf[slot_of(g, e + i, n_buf)].reshape(ATTN_WIDTH, page).astype(BF16) for i in range(ppb)],
            axis=1)

    def k_group(g, carry):
        for e in range(GROUP):
            k_copy(g * GROUP + e, slot_of(g, e, K_BUF)).wait()
        for e in range(0, GROUP, ppb):
            kt = block_of(kbuf, g, e, K_BUF)
            s2 = _dot(q_hilo, kt)
            s_scr[(g * GROUP + e) // ppb] = s2[:rows] + s2[rows:]
        refill(k_copy, g, K_BUF, K_PRIORITY)
        return carry

    lax.fori_loop(0, n_groups, k_group, 0)

    @pl.when(b + 1 < pl.num_programs(0))
    def _():
        for s in range(K_BUF):
            k_copy(s, s, b + 1).start(priority=K_PRIORITY)

    def gate_group(g, gate):
        for e in range(GROUP):
            n = g * GROUP + e
            gate = jnp.where(lane == n, jnp.sum(s_scr[n], axis=1, keepdims=True), gate)
        return gate

    gate = lax.fori_loop(0, nblk // GROUP, gate_group, jnp.zeros((rows, LANES), F32)) * (1.0 / MOBA_BLOCK)
    sel = _top_k_mask(gate, lane < nblk, lane, 1)
    bias = jnp.where(sel, 0.0, NEG)

    def bias_group(g, carry):
        for e in range(GROUP):
            n = g * GROUP + e
            col = jnp.sum(jnp.where(lane == n, bias, 0.0), axis=1, keepdims=True)
            bias_scr[n] = jnp.broadcast_to(col, (rows, MOBA_BLOCK))
        return carry

    lax.fori_loop(0, nblk // GROUP, bias_group, 0)

    lane_b = lax.broadcasted_iota(jnp.int32, (rows, MOBA_BLOCK), 1)
    q_pos = lax.broadcasted_iota(jnp.int32, (rows, MOBA_BLOCK), 0) % dec
    s_own = jnp.where(lane_b <= q_pos, _dot_nt(q_hi, kn_ref[0]), NEG)

    def max_group(g, mv):
        for e in range(GROUP):
            n = g * GROUP + e
            mv = jnp.maximum(mv, s_scr[n] + bias_scr[n])
        return mv

    m = jnp.max(lax.fori_loop(0, nblk // GROUP, max_group, s_own), axis=1, keepdims=True)
    p_own = jnp.exp(s_own - m)

    def exp_group(g, lv):
        for e in range(GROUP):
            n = g * GROUP + e
            p = jnp.exp(s_scr[n] + (bias_scr[n] - m))
            p_scr[n] = p.astype(BF16)
            lv = lv + p
        return lv

    l = jnp.sum(lax.fori_loop(0, nblk // GROUP, exp_group, p_own), axis=1, keepdims=True)

    def v_group(g, acc):
        for e in range(GROUP):
            v_copy(g * GROUP + e, slot_of(g, e, V_BUF)).wait()
        for e in range(0, GROUP, ppb):
            acc = acc + _dot_nt(p_scr[(g * GROUP + e) // ppb], block_of(vbuf, g, e, V_BUF))
        refill(v_copy, g, V_BUF, V_PRIORITY)
        return acc

    acc = lax.fori_loop(0, n_groups, v_group, _dot(p_own.astype(BF16), vn_ref[0]))
    o = jnp.where(head_mask, acc / l, 0.0)
    out = o[0:dec]
    for h in range(1, N_HEADS):
        out = out + o[h * dec:(h + 1) * dec]
    o_ref[0] = out


def _sample_attn(page_table, q_s, kn_pad, vn_pad, cache_kT, cache_vT):
    nseq, n_pages = page_table.shape
    dec = q_s.shape[1]
    page = cache_kT.shape[-1]
    nblk = n_pages * page // MOBA_BLOCK
    grid_spec = pltpu.PrefetchScalarGridSpec(
        num_scalar_prefetch=1,
        grid=(nseq,),
        in_specs=[
            pl.BlockSpec((1, dec, ATTN_WIDTH), lambda b, pt: (b, 0, 0)),
            pl.BlockSpec((1, MOBA_BLOCK, ATTN_WIDTH), lambda b, pt: (b, 0, 0)),
            pl.BlockSpec((1, MOBA_BLOCK, ATTN_WIDTH), lambda b, pt: (b, 0, 0)),
            pl.BlockSpec(memory_space=pl.ANY),
            pl.BlockSpec(memory_space=pl.ANY),
        ],
        out_specs=pl.BlockSpec((1, dec, ATTN_WIDTH), lambda b, pt: (b, 0, 0)),
        scratch_shapes=[
            pltpu.VMEM((K_BUF, N_HEADS, HEAD_DIM, page), F32),
            pltpu.VMEM((V_BUF, N_HEADS, HEAD_DIM, page), F32),
            pltpu.SemaphoreType.DMA((K_BUF,)),
            pltpu.SemaphoreType.DMA((V_BUF,)),
            pltpu.VMEM((nblk, N_HEADS * dec, MOBA_BLOCK), F32),
            pltpu.VMEM((nblk, N_HEADS * dec, MOBA_BLOCK), BF16),
            pltpu.VMEM((nblk, N_HEADS * dec, MOBA_BLOCK), F32),
        ],
    )
    return pl.pallas_call(
        _dec_kernel,
        grid_spec=grid_spec,
        out_shape=jax.ShapeDtypeStruct((nseq, dec, ATTN_WIDTH), F32),
        compiler_params=pltpu.CompilerParams(dimension_semantics=("arbitrary",),
                                             vmem_limit_bytes=VMEM_LIMIT),
        name="sample_attn",
    )(page_table, q_s, kn_pad, vn_pad, cache_kT, cache_vT)


def _rope_angles(pos):
    inv = jnp.power(ROPE_THETA, -jnp.arange(ROT_HALF, dtype=F32) * (2.0 / ROT_DIM))
    ang = pos.astype(F32)[:, None] * inv[None, :]
    return jnp.cos(ang), jnp.sin(ang)


def _rope_tables(pos):
    cos, sin = _rope_angles(pos)
    lane = jnp.arange(LANES) % HEAD_DIM
    f = lane % ROT_HALF
    c_tab = jnp.where(lane[None, :] < ROT_DIM, cos[:, f], 1.0)
    s1_tab = jnp.where(lane[None, :] < ROT_HALF, -sin[:, f], 0.0)
    s2_tab = jnp.where((lane[None, :] >= ROT_HALF) & (lane[None, :] < ROT_DIM), sin[:, f], 0.0)
    return c_tab, s1_tab, s2_tab


def _layer(l, xp, xs, cache_k, cache_v, state_pool, page_table, norm1_g, w_in, w_pool, pool_scale,
           w_o, norm2_g, w_up, w_down, final_g):
    seq = xp.shape[0]
    nseq, dec, _ = xs.shape
    past_len = page_table.shape[1] * cache_k.shape[2]

    g1 = norm1_g[l][None, :]
    g2 = norm2_g[l][None, :]
    win = w_in[l].astype(BF16)
    wqkvT = win[:, :3 * ATTN_WIDTH].T
    wu = win[:, 3 * ATTN_WIDTH:]
    wpool = w_pool[l].astype(BF16)
    pscale = pool_scale[l][None, :]
    wo = w_o[l].astype(BF16)
    wup = w_up[l].astype(BF16)
    wdown = w_down[l].astype(BF16)
    gf = final_g[None, :]

    cos, sin = _rope_angles(jnp.arange(seq, dtype=jnp.int32))
    qT, kT, vT, v_tiles, k_aug, kmean, pool_p, u_tail = _prompt_proj(
        xp, g1, wqkvT, wu, cos.T, sin.T, wpool, pscale)
    nb = seq // MOBA_BLOCK
    kmean_hm = kmean[:, 0, :].reshape(nb, N_HEADS, HEAD_DIM).transpose(1, 0, 2)
    attn_p = _prompt_attn(qT, k_aug, v_tiles, kmean_hm)
    yp = _out_mlp(xp, attn_p, pool_p, wo, g2, wup, wdown, gf, tl=1024, tf=1024)
    k_prompt = kT.reshape(N_HEADS, HEAD_DIM, seq).transpose(2, 0, 1)[None]
    v_prompt = vT.reshape(N_HEADS, HEAD_DIM, seq).transpose(2, 0, 1)[None]
    pool_prompt = u_tail[HALO - POOL_STATE:][None]

    rows = nseq * dec
    pos_s = past_len + jnp.arange(dec, dtype=jnp.int32)
    c_s, s1_s, s2_s = _rope_tables(jnp.tile(pos_s, nseq))
    state = state_pool[l].astype(F32)
    state_pad = jnp.pad(state, ((0, 0), (HALO - POOL_STATE, 0), (0, 0)))
    q_s, k_s, v_s, u_s, pool_s = _sample_proj(
        xs.reshape(rows, D_MODEL), g1, win, c_s, s1_s, s2_s, state_pad, wpool, pscale)
    pad = ((0, 0), (0, MOBA_BLOCK - dec), (0, 0))
    kn_pad = jnp.pad(k_s.reshape(nseq, dec, ATTN_WIDTH), pad).astype(BF16)
    vn_pad = jnp.pad(v_s.reshape(nseq, dec, ATTN_WIDTH), pad).astype(BF16)
    cache_kT = jnp.transpose(cache_k[l], (0, 2, 3, 1))
    cache_vT = jnp.transpose(cache_v[l], (0, 2, 3, 1))
    attn_s = _sample_attn(page_table, q_s.reshape(nseq, dec, ATTN_WIDTH), kn_pad, vn_pad,
                          cache_kT, cache_vT)
    ys = _out_mlp(xs.reshape(rows, D_MODEL), attn_s.reshape(rows, ATTN_WIDTH).astype(BF16), pool_s,
                  wo, g2, wup, wdown, gf, tl=rows, tf=1024)
    k_sample = k_s.reshape(nseq, dec, N_HEADS, HEAD_DIM)
    v_sample = v_s.reshape(nseq, dec, N_HEADS, HEAD_DIM)
    u_ext = jnp.concatenate([state, u_s.reshape(nseq, dec, POOL_WIDTH)], axis=1)
    pool_sample = u_ext[:, u_ext.shape[1] - POOL_STATE:]
    return (yp, ys.reshape(nseq, dec, D_MODEL), k_prompt, v_prompt, pool_prompt,
            k_sample, v_sample, pool_sample)


def kernel(x_prompt, x_sample, cache_k, cache_v, state_pool, page_table, norm1_g, w_in, w_pool,
           pool_scale, w_o, norm2_g, w_up, w_down, final_g):
    depth = w_in.shape[0]
    assert depth == 1 and x_prompt.shape[0] == 1, "single layer, single prompt sequence"
    (yp, ys, k_p, v_p, pool_p, k_s, v_s, pool_s) = _layer(
        0, x_prompt[0], x_sample, cache_k, cache_v, state_pool, page_table, norm1_g, w_in, w_pool,
        pool_scale, w_o, norm2_g, w_up, w_down, final_g)
    return (yp[None], ys, k_p[None], v_p[None], pool_p[None], k_s[None], v_s[None], pool_s[None])
```

```python
import functools

import jax
import jax.numpy as jnp
from jax import lax
from jax.experimental import pallas as pl
from jax.experimental.pallas import tpu as pltpu

F32 = jnp.float32
BF16 = jnp.bfloat16

D_MODEL = 1024
HEAD_DIM = 64
N_HEADS = 8
ATTN_WIDTH = N_HEADS * HEAD_DIM
POOL_WINDOWS = (2, 4, 8, 16)
POOL_CH = 128
POOL_WIDTH = POOL_CH * len(POOL_WINDOWS)
POOL_STATE = max(POOL_WINDOWS) - 1
HALO = 16
D_FF = 4 * D_MODEL
MOBA_BLOCK = 256
MOBA_TOP_K = 3
ROPE_THETA = 500000.0
ROT_DIM = HEAD_DIM // 4
ROT_HALF = ROT_DIM // 2
ATTN_SCALE = HEAD_DIM ** -0.5
LOG2E = 1.4426950408889634
PROMPT_Q_SCALE = ATTN_SCALE * LOG2E
RMS_EPS = 1e-6
NEG = -1e30
LANES = 128
ATTN_TILE = 512
Q_TILE = 2 * ATTN_TILE
ATTN_PIECES = 2
V_ROWS = HEAD_DIM + 16
MLP_ROW_TILE = 1024
MLP_FF_TILE = 1024
VMEM_LIMIT = 56 * 1024 * 1024

_NT = (((1,), (1,)), ((), ()))


def _dot(a, b):
    return jnp.dot(a, b, preferred_element_type=F32)


def _dot_nt(a, b):
    return lax.dot_general(a, b, _NT, preferred_element_type=F32)


def _split_bf16(x):
    hi = x.astype(BF16)
    lo = (x - hi.astype(F32)).astype(BF16)
    return hi, lo


def _dot3(a, b):
    a_hi, a_lo = _split_bf16(a)
    b_hi, b_lo = _split_bf16(b)
    return _dot(a_hi, b_hi) + _dot(a_hi, b_lo) + _dot(a_lo, b_hi)


def _rmsnorm(x, g):
    ms = jnp.mean(x * x, axis=-1, keepdims=True)
    return x * lax.rsqrt(ms + RMS_EPS) * g


def _top_k_mask(gate, valid, index, axis):
    limit = gate.shape[axis]
    g = jnp.where(valid, gate, -jnp.inf)
    sel = jnp.zeros(gate.shape, jnp.bool_)
    for _ in range(MOBA_TOP_K):
        mx = jnp.max(g, axis=axis, keepdims=True)
        first = jnp.min(jnp.where(g == mx, index, limit), axis=axis, keepdims=True)
        pick = jnp.logical_and(index == first, mx > -jnp.inf)
        sel = jnp.logical_or(sel, pick)
        g = jnp.where(pick, -jnp.inf, g)
    return sel


def _proj_kernel(x_ref, g_ref, wqkvT_ref, wu_ref, cosT_ref, sinT_ref, wpool_ref, pscale_ref,
                 qT_ref, kT_ref, vT_ref, vblk_ref, kaug_ref, kmean_ref, pool_ref, utail_ref,
                 ext_ref):
    i = pl.program_id(0)
    tl = x_ref.shape[0]
    hb = _rmsnorm(x_ref[...], g_ref[...]).astype(BF16)

    zT = _dot_nt(wqkvT_ref[...], hb)
    cosT = cosT_ref[...]
    sinT = sinT_ref[...]
    k_rows = []
    for hd in range(2 * N_HEADS):
        base = hd * HEAD_DIM
        x1 = zT[base:base + ROT_HALF]
        x2 = zT[base + ROT_HALF:base + ROT_DIM]
        rest = zT[base + ROT_DIM:base + HEAD_DIM]
        r1 = x1 * cosT - x2 * sinT
        r2 = x2 * cosT + x1 * sinT
        if hd < N_HEADS:
            qT_ref[base:base + ROT_HALF, :] = r1 * PROMPT_Q_SCALE
            qT_ref[base + ROT_HALF:base + ROT_DIM, :] = r2 * PROMPT_Q_SCALE
            qT_ref[base + ROT_DIM:base + HEAD_DIM, :] = rest * PROMPT_Q_SCALE
        else:
            k_rows += [r1, r2, rest]
    kT = jnp.concatenate(k_rows, axis=0)
    kT_ref[...] = kT
    vT = zT[2 * ATTN_WIDTH:3 * ATTN_WIDTH]
    vT_ref[...] = vT
    ones_rows = jnp.ones((V_ROWS - HEAD_DIM, tl), BF16)
    vT16 = vT.astype(BF16)
    vblk_ref[0] = jnp.concatenate(
        [piece for h in range(N_HEADS)
         for piece in (vT16[HEAD_DIM * h:HEAD_DIM * (h + 1)], ones_rows)], axis=0)

    k_rm = kT.T
    lane = lax.broadcasted_iota(jnp.int32, (tl, LANES), 1)
    block_tag = jnp.where(lane - HEAD_DIM == i, 1.0, 0.0)
    low = lane < HEAD_DIM
    for g in range(ATTN_WIDTH // LANES):
        kr = k_rm[:, LANES * g:LANES * (g + 1)]
        kaug_ref[2 * g] = jnp.where(low, kr, block_tag).astype(BF16)
        kaug_ref[2 * g + 1] = jnp.where(low, pltpu.roll(kr, HEAD_DIM, 1), block_tag).astype(BF16)
        ksum = jnp.sum(kr, axis=0, keepdims=True) * (1.0 / tl)
        kmean_ref[0, :, LANES * g:LANES * (g + 1)] = jnp.broadcast_to(ksum, (8, LANES))

    u = _dot(hb, wu_ref[...])

    @pl.when(i == 0)
    def _():
        ext_ref[0:HALO, :] = jnp.zeros((HALO, POOL_WIDTH), F32)

    ext_ref[HALO:HALO + tl, :] = u
    row = lax.broadcasted_iota(jnp.int32, (tl, 1), 0) + i * tl
    for g, w in enumerate(POOL_WINDOWS):
        lanes = slice(POOL_CH * g, POOL_CH * (g + 1))
        ug = ext_ref[HALO:HALO + tl, lanes]
        acc = ug
        for j in range(1, w):
            acc = acc + ext_ref[HALO - j:HALO - j + tl, lanes]
        cnt = jnp.minimum(row + 1, w).astype(F32)
        d = acc / cnt - ug
        y = _dot(d.astype(BF16), wpool_ref[g]) * pscale_ref[:, lanes]
        pool_ref[:, lanes] = y.astype(BF16)
    tail = ext_ref[tl:tl + HALO, :]
    ext_ref[0:HALO, :] = tail
    utail_ref[...] = tail


def _prompt_proj(x, g1, wqkvT, wu, cosT, sinT, wpool, pscale):
    seq = x.shape[0]
    tl = MOBA_BLOCK
    nb = seq // tl
    bpt = ATTN_TILE // tl
    const2 = lambda i: (0, 0)
    return pl.pallas_call(
        _proj_kernel,
        grid=(nb,),
        in_specs=[
            pl.BlockSpec((tl, D_MODEL), lambda i: (i, 0)),
            pl.BlockSpec((1, D_MODEL), const2),
            pl.BlockSpec((3 * ATTN_WIDTH, D_MODEL), const2),
            pl.BlockSpec((D_MODEL, POOL_WIDTH), const2),
            pl.BlockSpec((ROT_HALF, tl), lambda i: (0, i)),
            pl.BlockSpec((ROT_HALF, tl), lambda i: (0, i)),
            pl.BlockSpec((len(POOL_WINDOWS), POOL_CH, POOL_CH), lambda i: (0, 0, 0)),
            pl.BlockSpec((1, POOL_WIDTH), const2),
        ],
        out_specs=[
            pl.BlockSpec((ATTN_WIDTH, tl), lambda i: (0, i)),
            pl.BlockSpec((ATTN_WIDTH, tl), lambda i: (0, i)),
            pl.BlockSpec((ATTN_WIDTH, tl), lambda i: (0, i)),
            pl.BlockSpec((1, N_HEADS * V_ROWS, tl), lambda i: (i // bpt, 0, i % bpt)),
            pl.BlockSpec((N_HEADS, tl, LANES), lambda i: (0, i, 0)),
            pl.BlockSpec((1, 8, ATTN_WIDTH), lambda i: (i, 0, 0)),
            pl.BlockSpec((tl, POOL_WIDTH), lambda i: (i, 0)),
            pl.BlockSpec((HALO, POOL_WIDTH), const2),
        ],
        out_shape=[
            jax.ShapeDtypeStruct((ATTN_WIDTH, seq), F32),
            jax.ShapeDtypeStruct((ATTN_WIDTH, seq), F32),
            jax.ShapeDtypeStruct((ATTN_WIDTH, seq), F32),
            jax.ShapeDtypeStruct((nb // bpt, N_HEADS * V_ROWS, bpt * tl), BF16),
            jax.ShapeDtypeStruct((N_HEADS, seq, LANES), BF16),
            jax.ShapeDtypeStruct((nb, 8, ATTN_WIDTH), F32),
            jax.ShapeDtypeStruct((seq, POOL_WIDTH), BF16),
            jax.ShapeDtypeStruct((HALO, POOL_WIDTH), F32),
        ],
        scratch_shapes=[pltpu.VMEM((HALO + tl, POOL_WIDTH), F32)],
        compiler_params=pltpu.CompilerParams(dimension_semantics=("arbitrary",),
                                             vmem_limit_bytes=VMEM_LIMIT),
        name="prompt_proj",
    )(x, g1, wqkvT, wu, cosT, sinT, wpool, pscale)


def _attn_kernel(q_ref, k_ref, v_ref, km_ref, o_ref, s_ref, cmax_ref):
    tile_id = pl.program_id(1)
    nblk = km_ref.shape[1]
    tile = q_ref.shape[1]
    chunk = v_ref.shape[2]
    cpt = tile // chunk
    blk_idx = lax.broadcasted_iota(jnp.int32, (nblk, tile), 0)
    q_blk = (tile_id * (tile // MOBA_BLOCK)
             + lax.broadcasted_iota(jnp.int32, (nblk, tile), 1) // MOBA_BLOCK)
    key_idx = lax.broadcasted_iota(jnp.int32, (chunk, tile), 0)
    qry_idx = lax.broadcasted_iota(jnp.int32, (chunk, tile), 1)

    q_aug = [None, None]

    def select_blocks(hh):
        qT = q_ref[HEAD_DIM * hh:HEAD_DIM * (hh + 1), :]
        gate = _dot3(km_ref[hh], qT)
        sel = _top_k_mask(gate, blk_idx < q_blk, blk_idx, 0)
        bias = jnp.where(jnp.logical_or(sel, blk_idx == q_blk), 0.0, NEG)
        q_aug[hh] = jnp.concatenate([qT.astype(BF16), bias.astype(BF16)], axis=0)

    def scores(hh, c):
        start = pl.multiple_of(c * chunk, chunk)
        return _dot(k_ref[hh, pl.ds(start, chunk), :], q_aug[hh])

    def values(hh, c):
        return v_ref[c, V_ROWS * hh:V_ROWS * (hh + 1), :]

    def produce(slot, c, diag_offset=None, heads=(0, 1)):
        for hh in heads:
            s = scores(hh, c)
            if diag_offset is not None:
                s = jnp.where(key_idx + diag_offset <= qry_idx, s, NEG)
            s_ref[slot, hh] = s
            cmax_ref[slot, hh] = jnp.broadcast_to(jnp.max(s, axis=0, keepdims=True), (8, tile))

    def consume(slot, c, carry, heads=(0, 1)):
        out = []
        for hh in heads:
            m, acc = carry[2 * hh:2 * hh + 2]
            m_new = jnp.maximum(m, cmax_ref[slot, hh, 0:1, :])
            alpha = jnp.exp2(m - m_new)
            p = jnp.exp2(s_ref[slot, hh] - m_new).astype(BF16)
            out += [m_new, acc * alpha + _dot(values(hh, c), p)]
        return tuple(out)

    assert cpt == 2
    diag0 = tile_id * cpt
    init = (jnp.full((1, tile), NEG, F32), jnp.zeros((V_ROWS, tile), F32))
    for hh in range(2):
        select_blocks(hh)
    produce(0, diag0, 0)

    def stage(p_slot, p_chunk, c_slot, c_chunk, carry, diag_offset=None):
        half = chunk // ATTN_PIECES
        m_new, acc, cmax = [], [], [None, None]
        for hh in range(2):
            m, a = carry[2 * hh:2 * hh + 2]
            m_new.append(jnp.maximum(m, cmax_ref[c_slot, hh, 0:1, :]))
            acc.append(a * jnp.exp2(m - m_new[hh]))
        for kh in range(ATTN_PIECES):
            for hh in range(2):
                start = pl.multiple_of(p_chunk * chunk + kh * half, half)
                if diag_offset is None:
                    s = _dot(k_ref[hh, pl.ds(start, half), :], q_aug[hh])
                else:
                    live = tile - diag_offset
                    s = _dot(k_ref[hh, pl.ds(start, half), :], q_aug[hh][:, diag_offset:])
                    k_pos = lax.broadcasted_iota(jnp.int32, (half, live), 0) + kh * half
                    s = jnp.where(k_pos <= lax.broadcasted_iota(jnp.int32, (half, live), 1), s, NEG)
                    s = jnp.concatenate([jnp.full((half, diag_offset), NEG, F32), s], axis=1)
                s_ref[p_slot, hh, kh * half:(kh + 1) * half, :] = s
                part = jnp.max(s, axis=0, keepdims=True)
                cmax[hh] = part if cmax[hh] is None else jnp.maximum(cmax[hh], part)
                p = jnp.exp2(s_ref[c_slot, hh, kh * half:(kh + 1) * half, :] - m_new[hh]).astype(BF16)
                acc[hh] = acc[hh] + _dot(
                    v_ref[c_chunk, V_ROWS * hh:V_ROWS * (hh + 1), kh * half:(kh + 1) * half], p)
        for hh in range(2):
            cmax_ref[p_slot, hh] = jnp.broadcast_to(cmax[hh], (8, tile))
        return (m_new[0], acc[0], m_new[1], acc[1])

    def pair(i, carry):
        carry = stage(0, 2 * i, 1, jnp.where(i == 0, diag0 + 1, 2 * i - 1), carry)
        return stage(1, 2 * i + 1, 0, 2 * i, carry)

    state = stage(1, diag0 + 1, 0, diag0, init + init, diag_offset=chunk)
    state = lax.fori_loop(0, tile_id, pair, state)
    state = consume(1, jnp.where(tile_id == 0, diag0 + 1, diag0 - 1), state)
    oT = jnp.concatenate([acc[:HEAD_DIM] / acc[HEAD_DIM:HEAD_DIM + 1] for acc in (state[1], state[3])],
                         axis=0)
    o_ref[...] = oT.T.astype(BF16)


def _prompt_attn(qT, k_aug, v_tiles, kmean_hm):
    seq = qT.shape[1]
    nc, _, chunk = v_tiles.shape
    tile = Q_TILE
    nblk = kmean_hm.shape[1]
    return pl.pallas_call(
        _attn_kernel,
        grid=(N_HEADS // 2, seq // tile),
        in_specs=[
            pl.BlockSpec((2 * HEAD_DIM, tile), lambda p, t: (p, t)),
            pl.BlockSpec((2, seq, LANES), lambda p, t: (p, 0, 0)),
            pl.BlockSpec((nc, 2 * V_ROWS, chunk), lambda p, t: (0, p, 0)),
            pl.BlockSpec((2, nblk, HEAD_DIM), lambda p, t: (p, 0, 0)),
        ],
        out_specs=pl.BlockSpec((tile, 2 * HEAD_DIM), lambda p, t: (t, p)),
        out_shape=jax.ShapeDtypeStruct((seq, ATTN_WIDTH), BF16),
        scratch_shapes=[pltpu.VMEM((2, 2, chunk, tile), F32), pltpu.VMEM((2, 2, 8, tile), F32)],
        compiler_params=pltpu.CompilerParams(dimension_semantics=("arbitrary", "arbitrary"),
                                             vmem_limit_bytes=VMEM_LIMIT),
        name="prompt_attn",
    )(qT, k_aug, v_tiles, kmean_hm)


def _mlp_kernel(x_ref, attn_ref, pool_ref, wo_ref, g2_ref, wup_ref, wdown_ref, gf_ref, y_ref,
                x1_ref, hn_ref, acc_ref):
    j = pl.program_id(1)

    @pl.when(j == 0)
    def _():
        x1 = (x_ref[...] + _dot(attn_ref[...], wo_ref[0:ATTN_WIDTH, :])
              + _dot(pool_ref[...], wo_ref[ATTN_WIDTH:, :]))
        x1_ref[...] = x1
        hn_ref[...] = _rmsnorm(x1, g2_ref[...]).astype(BF16)
        acc_ref[...] = jnp.zeros(acc_ref.shape, F32)

    hid = jnp.maximum(_dot(hn_ref[...], wup_ref[...]), 0.0)
    acc_ref[...] += _dot((hid * hid).astype(BF16), wdown_ref[...])

    @pl.when(j == pl.num_programs(1) - 1)
    def _():
        y_ref[...] = _rmsnorm(x1_ref[...] + acc_ref[...], gf_ref[...])


def _out_mlp(x, attn, pool, wo, g2, wup, wdown, gf, tl, tf):
    rows = x.shape[0]
    const2 = lambda i, j: (0, 0)
    return pl.pallas_call(
        _mlp_kernel,
        grid=(rows // tl, D_FF // tf),
        in_specs=[
            pl.BlockSpec((tl, D_MODEL), lambda i, j: (i, 0)),
            pl.BlockSpec((tl, ATTN_WIDTH), lambda i, j: (i, 0)),
            pl.BlockSpec((tl, POOL_WIDTH), lambda i, j: (i, 0)),
            pl.BlockSpec((D_MODEL, D_MODEL), const2),
            pl.BlockSpec((1, D_MODEL), const2),
            pl.BlockSpec((D_MODEL, tf), lambda i, j: (0, j)),
            pl.BlockSpec((tf, D_MODEL), lambda i, j: (j, 0)),
            pl.BlockSpec((1, D_MODEL), const2),
        ],
        out_specs=pl.BlockSpec((tl, D_MODEL), lambda i, j: (i, 0)),
        out_shape=jax.ShapeDtypeStruct((rows, D_MODEL), F32),
        scratch_shapes=[pltpu.VMEM((tl, D_MODEL), F32), pltpu.VMEM((tl, D_MODEL), BF16),
                        pltpu.VMEM((tl, D_MODEL), F32)],
        compiler_params=pltpu.CompilerParams(dimension_semantics=("arbitrary", "arbitrary"),
                                             vmem_limit_bytes=VMEM_LIMIT),
        name="out_mlp",
    )(x, attn, pool, wo, g2, wup, wdown, gf)


def _sproj_kernel(x_ref, g_ref, win_ref, c_ref, s1_ref, s2_ref, state_ref, wpool_ref, pscale_ref,
                  q_ref, k_ref, v_ref, u_ref, pool_ref, ext_ref):
    rows = x_ref.shape[0]
    nseq, dec = state_ref.shape[0], rows // state_ref.shape[0]
    hb = _rmsnorm(x_ref[...], g_ref[...]).astype(BF16)
    z = _dot(hb, win_ref[...])
    for g in range(2 * ATTN_WIDTH // LANES):
        xg = z[:, LANES * g:LANES * (g + 1)]
        r = (xg * c_ref[...] + pltpu.roll(xg, ROT_HALF, 1) * s2_ref[...]
             + pltpu.roll(xg, LANES - ROT_HALF, 1) * s1_ref[...])
        if LANES * g < ATTN_WIDTH:
            q_ref[:, LANES * g:LANES * (g + 1)] = r * ATTN_SCALE
        else:
            k_ref[:, LANES * g - ATTN_WIDTH:LANES * (g + 1) - ATTN_WIDTH] = r
    v_ref[...] = z[:, 2 * ATTN_WIDTH:3 * ATTN_WIDTH]
    u = z[:, 3 * ATTN_WIDTH:]
    u_ref[...] = u

    ext_ref[:, 0:HALO, :] = state_ref[...]
    ext_ref[:, HALO:HALO + dec, :] = u.reshape(nseq, dec, POOL_WIDTH)
    for g, w in enumerate(POOL_WINDOWS):
        lanes = slice(POOL_CH * g, POOL_CH * (g + 1))
        ug = ext_ref[:, HALO:HALO + dec, lanes]
        acc = ug
        for j in range(1, w):
            acc = acc + ext_ref[:, HALO - j:HALO - j + dec, lanes]
        d = (acc / float(w) - ug).reshape(rows, POOL_CH)
        y = _dot(d.astype(BF16), wpool_ref[g]) * pscale_ref[:, lanes]
        pool_ref[:, lanes] = y.astype(BF16)


def _sample_proj(x, g1, win, c_tab, s1_tab, s2_tab, state_pad, wpool, pscale):
    rows = x.shape[0]
    nseq = state_pad.shape[0]
    shapes = [jax.ShapeDtypeStruct((rows, ATTN_WIDTH), F32)] * 3 + [
        jax.ShapeDtypeStruct((rows, POOL_WIDTH), F32), jax.ShapeDtypeStruct((rows, POOL_WIDTH), BF16)]
    return pl.pallas_call(
        _sproj_kernel,
        out_shape=shapes,
        scratch_shapes=[pltpu.VMEM((nseq, HALO + rows // nseq, POOL_WIDTH), F32)],
        compiler_params=pltpu.CompilerParams(vmem_limit_bytes=VMEM_LIMIT),
        name="sample_proj",
    )(x, g1, win, c_tab, s1_tab, s2_tab, state_pad, wpool, pscale)


GROUP = 16
K_BUF = 4 * GROUP
V_BUF = 4 * GROUP
K_PRIORITY = 0
V_PRIORITY = 1


def _dec_kernel(pt_ref, q_ref, kn_ref, vn_ref, ck_hbm, cv_hbm, o_ref,
                kbuf, vbuf, ksem, vsem, s_scr, p_scr, bias_scr):
    b = pl.program_id(0)
    n_pages = pt_ref.shape[1]
    page = kbuf.shape[-1]
    ppb = MOBA_BLOCK // page
    nblk = n_pages // ppb
    n_groups = n_pages // GROUP
    dec = q_ref.shape[1]
    rows = N_HEADS * dec

    def slot_of(g, e, n_buf):
        return (g % (n_buf // GROUP)) * GROUP + e

    def k_copy(j, slot, seq=b):
        return pltpu.make_async_copy(ck_hbm.at[pt_ref[seq, j]], kbuf.at[slot], ksem.at[slot])

    def v_copy(j, slot):
        return pltpu.make_async_copy(cv_hbm.at[pt_ref[b, j]], vbuf.at[slot], vsem.at[slot])

    @pl.when(b == 0)
    def _():
        for s in range(K_BUF):
            k_copy(s, s).start(priority=K_PRIORITY)

    for s in range(V_BUF):
        v_copy(s, s).start(priority=V_PRIORITY)

    r_idx = lax.broadcasted_iota(jnp.int32, (rows, ATTN_WIDTH), 0)
    c_idx = lax.broadcasted_iota(jnp.int32, (rows, ATTN_WIDTH), 1)
    head_mask = (c_idx // HEAD_DIM) == (r_idx // dec)
    q_bd = jnp.where(head_mask, jnp.concatenate([q_ref[0]] * N_HEADS, axis=0), 0.0)
    q_hi, q_lo = _split_bf16(q_bd)
    q_hilo = jnp.concatenate([q_hi, q_lo], axis=0)
    lane = lax.broadcasted_iota(jnp.int32, (rows, LANES), 1)

    def refill(copy, g, n_buf, priority):
        @pl.when((g + n_buf // GROUP) * GROUP < n_pages)
        def _():
            for e in range(GROUP):
                copy(g * GROUP + e + n_buf, slot_of(g, e, n_buf)).start(priority=priority)

    def block_of(buf, g, e, n_buf):
        return jnp.concatenate(
            [buf[slot_of(g, e + i, n_buf)].reshape(ATTN_WIDTH, page).astype(BF16) for i in range(ppb)],
            axis=1)

    def k_group(g, carry):
        for e in range(GROUP):
            k_copy(g * GROUP + e, slot_of(g, e, K_BUF)).wait()
        for e in range(0, GROUP, ppb):
            kt = block_of(kbuf, g, e, K_BUF)
            s2 = _dot(q_hilo, kt)
            s_scr[(g * GROUP + e) // ppb] = s2[:rows] + s2[rows:]
        refill(k_copy, g, K_BUF, K_PRIORITY)
        return carry

    lax.fori_loop(0, n_groups, k_group, 0)

    @pl.when(b + 1 < pl.num_programs(0))
    def _():
        for s in range(K_BUF):
            k_copy(s, s, b + 1).start(priority=K_PRIORITY)

    def gate_group(g, gate):
        for e in range(GROUP):
            n = g * GROUP + e
            gate = jnp.where(lane == n, jnp.sum(s_scr[n], axis=1, keepdims=True), gate)
        return gate

    gate = lax.fori_loop(0, nblk // GROUP, gate_group, jnp.zeros((rows, LANES), F32)) * (1.0 / MOBA_BLOCK)
    sel = _top_k_mask(gate, lane < nblk, lane, 1)
    bias = jnp.where(sel, 0.0, NEG)

    def bias_group(g, carry):
        for e in range(GROUP):
            n = g * GROUP + e
            col = jnp.sum(jnp.where(lane == n, bias, 0.0), axis=1, keepdims=True)
            bias_scr[n] = jnp.broadcast_to(col, (rows, MOBA_BLOCK))
        return carry

    lax.fori_loop(0, nblk // GROUP, bias_group, 0)

    lane_b = lax.broadcasted_iota(jnp.int32, (rows, MOBA_BLOCK), 1)
    q_pos = lax.broadcasted_iota(jnp.int32, (rows, MOBA_BLOCK), 0) % dec
    s_own = jnp.where(lane_b <= q_pos, _dot_nt(q_hi, kn_ref[0]), NEG)

    def max_group(g, mv):
        for e in range(GROUP):
            n = g * GROUP + e
            mv = jnp.maximum(mv, s_scr[n] + bias_scr[n])
        return mv

    m = jnp.max(lax.fori_loop(0, nblk // GROUP, max_group, s_own), axis=1, keepdims=True)
    p_own = jnp.exp(s_own - m)

    def exp_group(g, lv):
        for e in range(GROUP):
            n = g * GROUP + e
            p = jnp.exp(s_scr[n] + (bias_scr[n] - m))
            p_scr[n] = p.astype(BF16)
            lv = lv + p
        return lv

    l = jnp.sum(lax.fori_loop(0, nblk // GROUP, exp_group, p_own), axis=1, keepdims=True)

    def v_group(g, acc):
        for e in range(GROUP):
            v_copy(g * GROUP + e, slot_of(g, e, V_BUF)).wait()
        for e in range(0, GROUP, ppb):
            acc = acc + _dot_nt(p_scr[(g * GROUP + e) // ppb], block_of(vbuf, g, e, V_BUF))
        refill(v_copy, g, V_BUF, V_PRIORITY)
        return acc

    acc = lax.fori_loop(0, n_groups, v_group, _dot(p_own.astype(BF16), vn_ref[0]))
    o = jnp.where(head_mask, acc / l, 0.0)
    out = o[0:dec]
    for h in range(1, N_HEADS):
        out = out + o[h * dec:(h + 1) * dec]
    o_ref[0] = out


def _sample_attn(page_table, q_s, kn_pad, vn_pad, cache_kT, cache_vT):
    nseq, n_pages = page_table.shape
    dec = q_s.shape[1]
    page = cache_kT.shape[-1]
    nblk = n_pages * page // MOBA_BLOCK
    grid_spec = pltpu.PrefetchScalarGridSpec(
        num_scalar_prefetch=1,
        grid=(nseq,),
        in_specs=[
            pl.BlockSpec((1, dec, ATTN_WIDTH), lambda b, pt: (b, 0, 0)),
            pl.BlockSpec((1, MOBA_BLOCK, ATTN_WIDTH), lambda b, pt: (b, 0, 0)),
            pl.BlockSpec((1, MOBA_BLOCK, ATTN_WIDTH), lambda b, pt: (b, 0, 0)),
            pl.BlockSpec(memory_space=pl.ANY),
            pl.BlockSpec(memory_space=pl.ANY),
        ],
        out_specs=pl.BlockSpec((1, dec, ATTN_WIDTH), lambda b, pt: (b, 0, 0)),
        scratch_shapes=[
            pltpu.VMEM((K_BUF, N_HEADS, HEAD_DIM, page), F32),
            pltpu.VMEM((V_BUF, N_HEADS, HEAD_DIM, page), F32),
            pltpu.SemaphoreType.DMA((K_BUF,)),
            pltpu.SemaphoreType.DMA((V_BUF,)),
            pltpu.VMEM((nblk, N_HEADS * dec, MOBA_BLOCK), F32),
            pltpu.VMEM((nblk, N_HEADS * dec, MOBA_BLOCK), BF16),
            pltpu.VMEM((nblk, N_HEADS * dec, MOBA_BLOCK), F32),
        ],
    )
    return pl.pallas_call(
        _dec_kernel,
        grid_spec=grid_spec,
        out_shape=jax.ShapeDtypeStruct((nseq, dec, ATTN_WIDTH), F32),
        compiler_params=pltpu.CompilerParams(dimension_semantics=("arbitrary",),
                                             vmem_limit_bytes=VMEM_LIMIT),
        name="sample_attn",
    )(page_table, q_s, kn_pad, vn_pad, cache_kT, cache_vT)


def _rope_angles(pos):
    inv = jnp.power(ROPE_THETA, -jnp.arange(ROT_HALF, dtype=F32) * (2.0 / ROT_DIM))
    ang = pos.astype(F32)[:, None] * inv[None, :]
    return jnp.cos(ang), jnp.sin(ang)


def _rope_tables(pos):
    cos, sin = _rope_angles(pos)
    lane = jnp.arange(LANES) % HEAD_DIM
    f = lane % ROT_HALF
    c_tab = jnp.where(lane[None, :] < ROT_DIM, cos[:, f], 1.0)
    s1_tab = jnp.where(lane[None, :] < ROT_HALF, -sin[:, f], 0.0)
    s2_tab = jnp.where((lane[None, :] >= ROT_HALF) & (lane[None, :] < ROT_DIM), sin[:, f], 0.0)
    return c_tab, s1_tab, s2_tab


def _layer(l, xp, xs, cache_k, cache_v, state_pool, page_table, norm1_g, w_in, w_pool, pool_scale,
           w_o, norm2_g, w_up, w_down, final_g):
    seq = xp.shape[0]
    nseq, dec, _ = xs.shape
    past_len = page_table.shape[1] * cache_k.shape[2]

    g1 = norm1_g[l][None, :]
    g2 = norm2_g[l][None, :]
    win = w_in[l].astype(BF16)
    wqkvT = win[:, :3 * ATTN_WIDTH].T
    wu = win[:, 3 * ATTN_WIDTH:]
    wpool = w_pool[l].astype(BF16)
    pscale = pool_scale[l][None, :]
    wo = w_o[l].astype(BF16)
    wup = w_up[l].astype(BF16)
    wdown = w_down[l].astype(BF16)
    gf = final_g[None, :]

    cos, sin = _rope_angles(jnp.arange(seq, dtype=jnp.int32))
    qT, kT, vT, v_tiles, k_aug, kmean, pool_p, u_tail = _prompt_proj(
        xp, g1, wqkvT, wu, cos.T, sin.T, wpool, pscale)
    nb = seq // MOBA_BLOCK
    kmean_hm = kmean[:, 0, :].reshape(nb, N_HEADS, HEAD_DIM).transpose(1, 0, 2)
    attn_p = _prompt_attn(qT, k_aug, v_tiles, kmean_hm)
    yp = _out_mlp(xp, attn_p, pool_p, wo, g2, wup, wdown, gf, tl=MLP_ROW_TILE, tf=MLP_FF_TILE)
    k_prompt = kT.reshape(N_HEADS, HEAD_DIM, seq).transpose(2, 0, 1)[None]
    v_prompt = vT.reshape(N_HEADS, HEAD_DIM, seq).transpose(2, 0, 1)[None]
    pool_prompt = u_tail[HALO - POOL_STATE:][None]

    rows = nseq * dec
    pos_s = past_len + jnp.arange(dec, dtype=jnp.int32)
    c_s, s1_s, s2_s = _rope_tables(jnp.tile(pos_s, nseq))
    state = state_pool[l].astype(F32)
    state_pad = jnp.pad(state, ((0, 0), (HALO - POOL_STATE, 0), (0, 0)))
    q_s, k_s, v_s, u_s, pool_s = _sample_proj(
        xs.reshape(rows, D_MODEL), g1, win, c_s, s1_s, s2_s, state_pad, wpool, pscale)
    pad = ((0, 0), (0, MOBA_BLOCK - dec), (0, 0))
    kn_pad = jnp.pad(k_s.reshape(nseq, dec, ATTN_WIDTH), pad).astype(BF16)
    vn_pad = jnp.pad(v_s.reshape(nseq, dec, ATTN_WIDTH), pad).astype(BF16)
    cache_kT = jnp.transpose(cache_k[l], (0, 2, 3, 1))
    cache_vT = jnp.transpose(cache_v[l], (0, 2, 3, 1))
    attn_s = _sample_attn(page_table, q_s.reshape(nseq, dec, ATTN_WIDTH), kn_pad, vn_pad,
                          cache_kT, cache_vT)
    ys = _out_mlp(xs.reshape(rows, D_MODEL), attn_s.reshape(rows, ATTN_WIDTH).astype(BF16), pool_s,
                  wo, g2, wup, wdown, gf, tl=rows, tf=MLP_FF_TILE)
    k_sample = k_s.reshape(nseq, dec, N_HEADS, HEAD_DIM)
    v_sample = v_s.reshape(nseq, dec, N_HEADS, HEAD_DIM)
    u_ext = jnp.concatenate([state, u_s.reshape(nseq, dec, POOL_WIDTH)], axis=1)
    pool_sample = u_ext[:, u_ext.shape[1] - POOL_STATE:]
    return (yp, ys.reshape(nseq, dec, D_MODEL), k_prompt, v_prompt, pool_prompt,
            k_sample, v_sample, pool_sample)


def kernel(x_prompt, x_sample, cache_k, cache_v, state_pool, page_table, norm1_g, w_in, w_pool,
           pool_scale, w_o, norm2_g, w_up, w_down, final_g):
    depth = w_in.shape[0]
    assert depth == 1 and x_prompt.shape[0] == 1, "single layer, single prompt sequence"
    (yp, ys, k_p, v_p, pool_p, k_s, v_s, pool_s) = _layer(
        0, x_prompt[0], x_sample, cache_k, cache_v, state_pool, page_table, norm1_g, w_in, w_pool,
        pool_scale, w_o, norm2_g, w_up, w_down, final_g)
    return (yp[None], ys, k_p[None], v_p[None], pool_p[None], k_s[None], v_s[None], pool_s[None])
```

```python
import functools

import jax
import jax.numpy as jnp
from jax import lax
from jax.experimental import pallas as pl
from jax.experimental.pallas import tpu as pltpu

F32 = jnp.float32
BF16 = jnp.bfloat16

D_MODEL = 1024
HEAD_DIM = 64
N_HEADS = 8
ATTN_WIDTH = N_HEADS * HEAD_DIM
POOL_WINDOWS = (2, 4, 8, 16)
POOL_CH = 128
POOL_WIDTH = POOL_CH * len(POOL_WINDOWS)
POOL_STATE = max(POOL_WINDOWS) - 1
HALO = 16
D_FF = 4 * D_MODEL
MOBA_BLOCK = 256
MOBA_TOP_K = 3
ROPE_THETA = 500000.0
ROT_DIM = HEAD_DIM // 4
ROT_HALF = ROT_DIM // 2
ATTN_SCALE = HEAD_DIM ** -0.5
LOG2E = 1.4426950408889634
PROMPT_Q_SCALE = ATTN_SCALE * LOG2E
RMS_EPS = 1e-6
NEG = -1e30
LANES = 128
ATTN_TILE = 512
Q_TILE = 2 * ATTN_TILE
ATTN_PIECES = 2
V_ROWS = HEAD_DIM + 16
MLP_ROW_TILE = 1024
MLP_FF_TILE = 1024
VMEM_LIMIT = 56 * 1024 * 1024

_NT = (((1,), (1,)), ((), ()))


def _dot(a, b):
    return jnp.dot(a, b, preferred_element_type=F32)


def _dot_nt(a, b):
    return lax.dot_general(a, b, _NT, preferred_element_type=F32)


def _split_bf16(x):
    hi = x.astype(BF16)
    lo = (x - hi.astype(F32)).astype(BF16)
    return hi, lo


def _dot3(a, b):
    a_hi, a_lo = _split_bf16(a)
    b_hi, b_lo = _split_bf16(b)
    return _dot(a_hi, b_hi) + _dot(a_hi, b_lo) + _dot(a_lo, b_hi)


def _rmsnorm(x, g):
    ms = jnp.mean(x * x, axis=-1, keepdims=True)
    return x * lax.rsqrt(ms + RMS_EPS) * g


def _top_k_mask(gate, valid, index, axis):
    limit = gate.shape[axis]
    g = jnp.where(valid, gate, -jnp.inf)
    sel = jnp.zeros(gate.shape, jnp.bool_)
    for _ in range(MOBA_TOP_K):
        mx = jnp.max(g, axis=axis, keepdims=True)
        first = jnp.min(jnp.where(g == mx, index, limit), axis=axis, keepdims=True)
        pick = jnp.logical_and(index == first, mx > -jnp.inf)
        sel = jnp.logical_or(sel, pick)
        g = jnp.where(pick, -jnp.inf, g)
    return sel


def _proj_kernel(x_ref, g_ref, wqkvT_ref, wu_ref, cosT_ref, sinT_ref, wpool_ref, pscale_ref,
                 qT_ref, kT_ref, vT_ref, vblk_ref, kaug_ref, kmean_ref, pool_ref, utail_ref,
                 ext_ref):
    i = pl.program_id(0)
    tl = x_ref.shape[0]
    hb = _rmsnorm(x_ref[...], g_ref[...]).astype(BF16)

    zT = _dot_nt(wqkvT_ref[...], hb)
    cosT = cosT_ref[...]
    sinT = sinT_ref[...]
    k_rows = []
    for hd in range(2 * N_HEADS):
        base = hd * HEAD_DIM
        x1 = zT[base:base + ROT_HALF]
        x2 = zT[base + ROT_HALF:base + ROT_DIM]
        rest = zT[base + ROT_DIM:base + HEAD_DIM]
        r1 = x1 * cosT - x2 * sinT
        r2 = x2 * cosT + x1 * sinT
        if hd < N_HEADS:
            qT_ref[base:base + ROT_HALF, :] = r1 * PROMPT_Q_SCALE
            qT_ref[base + ROT_HALF:base + ROT_DIM, :] = r2 * PROMPT_Q_SCALE
            qT_ref[base + ROT_DIM:base + HEAD_DIM, :] = rest * PROMPT_Q_SCALE
        else:
            k_rows += [r1, r2, rest]
    kT = jnp.concatenate(k_rows, axis=0)
    kT_ref[...] = kT
    vT = zT[2 * ATTN_WIDTH:3 * ATTN_WIDTH]
    vT_ref[...] = vT
    ones_rows = jnp.ones((V_ROWS - HEAD_DIM, tl), BF16)
    vT16 = vT.astype(BF16)
    vblk_ref[0] = jnp.concatenate(
        [piece for h in range(N_HEADS)
         for piece in (vT16[HEAD_DIM * h:HEAD_DIM * (h + 1)], ones_rows)], axis=0)

    k_rm = kT.T
    lane = lax.broadcasted_iota(jnp.int32, (tl, LANES), 1)
    block_tag = jnp.where(lane - HEAD_DIM == i, 1.0, 0.0)
    low = lane < HEAD_DIM
    for g in range(ATTN_WIDTH // LANES):
        kr = k_rm[:, LANES * g:LANES * (g + 1)]
        kaug_ref[2 * g] = jnp.where(low, kr, block_tag).astype(BF16)
        kaug_ref[2 * g + 1] = jnp.where(low, pltpu.roll(kr, HEAD_DIM, 1), block_tag).astype(BF16)
        ksum = jnp.sum(kr, axis=0, keepdims=True) * (1.0 / tl)
        kmean_ref[0, :, LANES * g:LANES * (g + 1)] = jnp.broadcast_to(ksum, (8, LANES))

    u = _dot(hb, wu_ref[...])

    @pl.when(i == 0)
    def _():
        ext_ref[0:HALO, :] = jnp.zeros((HALO, POOL_WIDTH), F32)

    ext_ref[HALO:HALO + tl, :] = u
    row = lax.broadcasted_iota(jnp.int32, (tl, 1), 0) + i * tl
    for g, w in enumerate(POOL_WINDOWS):
        lanes = slice(POOL_CH * g, POOL_CH * (g + 1))
        ug = ext_ref[HALO:HALO + tl, lanes]
        acc = ug
        for j in range(1, w):
            acc = acc + ext_ref[HALO - j:HALO - j + tl, lanes]
        cnt = jnp.minimum(row + 1, w).astype(F32)
        d = acc / cnt - ug
        y = _dot(d.astype(BF16), wpool_ref[g]) * pscale_ref[:, lanes]
        pool_ref[:, lanes] = y.astype(BF16)
    tail = ext_ref[tl:tl + HALO, :]
    ext_ref[0:HALO, :] = tail
    utail_ref[...] = tail


def _prompt_proj(x, g1, wqkvT, wu, cosT, sinT, wpool, pscale):
    seq = x.shape[0]
    tl = MOBA_BLOCK
    nb = seq // tl
    bpt = ATTN_TILE // tl
    const2 = lambda i: (0, 0)
    return pl.pallas_call(
        _proj_kernel,
        grid=(nb,),
        in_specs=[
            pl.BlockSpec((tl, D_MODEL), lambda i: (i, 0)),
            pl.BlockSpec((1, D_MODEL), const2),
            pl.BlockSpec((3 * ATTN_WIDTH, D_MODEL), const2),
            pl.BlockSpec((D_MODEL, POOL_WIDTH), const2),
            pl.BlockSpec((ROT_HALF, tl), lambda i: (0, i)),
            pl.BlockSpec((ROT_HALF, tl), lambda i: (0, i)),
            pl.BlockSpec((len(POOL_WINDOWS), POOL_CH, POOL_CH), lambda i: (0, 0, 0)),
            pl.BlockSpec((1, POOL_WIDTH), const2),
        ],
        out_specs=[
            pl.BlockSpec((ATTN_WIDTH, tl), lambda i: (0, i)),
            pl.BlockSpec((ATTN_WIDTH, tl), lambda i: (0, i)),
            pl.BlockSpec((ATTN_WIDTH, tl), lambda i: (0, i)),
            pl.BlockSpec((1, N_HEADS * V_ROWS, tl), lambda i: (i // bpt, 0, i % bpt)),
            pl.BlockSpec((N_HEADS, tl, LANES), lambda i: (0, i, 0)),
            pl.BlockSpec((1, 8, ATTN_WIDTH), lambda i: (i, 0, 0)),
            pl.BlockSpec((tl, POOL_WIDTH), lambda i: (i, 0)),
            pl.BlockSpec((HALO, POOL_WIDTH), const2),
        ],
        out_shape=[
            jax.ShapeDtypeStruct((ATTN_WIDTH, seq), F32),
            jax.ShapeDtypeStruct((ATTN_WIDTH, seq), F32),
            jax.ShapeDtypeStruct((ATTN_WIDTH, seq), F32),
            jax.ShapeDtypeStruct((nb // bpt, N_HEADS * V_ROWS, bpt * tl), BF16),
            jax.ShapeDtypeStruct((N_HEADS, seq, LANES), BF16),
            jax.ShapeDtypeStruct((nb, 8, ATTN_WIDTH), F32),
            jax.ShapeDtypeStruct((seq, POOL_WIDTH), BF16),
            jax.ShapeDtypeStruct((HALO, POOL_WIDTH), F32),
        ],
        scratch_shapes=[pltpu.VMEM((HALO + tl, POOL_WIDTH), F32)],
        compiler_params=pltpu.CompilerParams(dimension_semantics=("arbitrary",),
                                             vmem_limit_bytes=VMEM_LIMIT,
                                             allow_input_fusion=[False, False, True, True, True, True,
                                                                 True, False]),
        name="prompt_proj",
    )(x, g1, wqkvT, wu, cosT, sinT, wpool, pscale)


def _attn_kernel(q_ref, k_ref, v_ref, km_ref, o_ref, s_ref, cmax_ref):
    tile_id = pl.program_id(1)
    nblk = km_ref.shape[1]
    tile = q_ref.shape[1]
    chunk = v_ref.shape[2]
    cpt = tile // chunk
    blk_idx = lax.broadcasted_iota(jnp.int32, (nblk, tile), 0)
    q_blk = (tile_id * (tile // MOBA_BLOCK)
             + lax.broadcasted_iota(jnp.int32, (nblk, tile), 1) // MOBA_BLOCK)
    key_idx = lax.broadcasted_iota(jnp.int32, (chunk, tile), 0)
    qry_idx = lax.broadcasted_iota(jnp.int32, (chunk, tile), 1)

    q_aug = [None, None]

    def select_blocks(hh):
        qT = q_ref[HEAD_DIM * hh:HEAD_DIM * (hh + 1), :]
        gate = _dot3(km_ref[hh], qT)
        sel = _top_k_mask(gate, blk_idx < q_blk, blk_idx, 0)
        bias = jnp.where(jnp.logical_or(sel, blk_idx == q_blk), 0.0, NEG)
        q_aug[hh] = jnp.concatenate([qT.astype(BF16), bias.astype(BF16)], axis=0)

    def scores(hh, c):
        start = pl.multiple_of(c * chunk, chunk)
        return _dot(k_ref[hh, pl.ds(start, chunk), :], q_aug[hh])

    def values(hh, c):
        return v_ref[c, V_ROWS * hh:V_ROWS * (hh + 1), :]

    def produce(slot, c, diag_offset=None, heads=(0, 1)):
        for hh in heads:
            s = scores(hh, c)
            if diag_offset is not None:
                s = jnp.where(key_idx + diag_offset <= qry_idx, s, NEG)
            s_ref[slot, hh] = s
            cmax_ref[slot, hh] = jnp.broadcast_to(jnp.max(s, axis=0, keepdims=True), (8, tile))

    def consume(slot, c, carry, heads=(0, 1)):
        out = []
        for hh in heads:
            m, acc = carry[2 * hh:2 * hh + 2]
            m_new = jnp.maximum(m, cmax_ref[slot, hh, 0:1, :])
            alpha = jnp.exp2(m - m_new)
            p = jnp.exp2(s_ref[slot, hh] - m_new).astype(BF16)
            out += [m_new, acc * alpha + _dot(values(hh, c), p)]
        return tuple(out)

    assert cpt == 2
    diag0 = tile_id * cpt
    init = (jnp.full((1, tile), NEG, F32), jnp.zeros((V_ROWS, tile), F32))
    for hh in range(2):
        select_blocks(hh)
    produce(0, diag0, 0)

    def stage(p_slot, p_chunk, c_slot, c_chunk, carry, diag_offset=None):
        half = chunk // ATTN_PIECES
        m_new, acc, cmax = [], [], [None, None]
        for hh in range(2):
            m, a = carry[2 * hh:2 * hh + 2]
            m_new.append(jnp.maximum(m, cmax_ref[c_slot, hh, 0:1, :]))
            acc.append(a * jnp.exp2(m - m_new[hh]))
        for kh in range(ATTN_PIECES):
            for hh in range(2):
                start = pl.multiple_of(p_chunk * chunk + kh * half, half)
                if diag_offset is None:
                    s = _dot(k_ref[hh, pl.ds(start, half), :], q_aug[hh])
                else:
                    live = tile - diag_offset
                    s = _dot(k_ref[hh, pl.ds(start, half), :], q_aug[hh][:, diag_offset:])
                    k_pos = lax.broadcasted_iota(jnp.int32, (half, live), 0) + kh * half
                    s = jnp.where(k_pos <= lax.broadcasted_iota(jnp.int32, (half, live), 1), s, NEG)
                    s = jnp.concatenate([jnp.full((half, diag_offset), NEG, F32), s], axis=1)
                s_ref[p_slot, hh, kh * half:(kh + 1) * half, :] = s
                part = jnp.max(s, axis=0, keepdims=True)
                cmax[hh] = part if cmax[hh] is None else jnp.maximum(cmax[hh], part)
                p = jnp.exp2(s_ref[c_slot, hh, kh * half:(kh + 1) * half, :] - m_new[hh]).astype(BF16)
                acc[hh] = acc[hh] + _dot(
                    v_ref[c_chunk, V_ROWS * hh:V_ROWS * (hh + 1), kh * half:(kh + 1) * half], p)
        for hh in range(2):
            cmax_ref[p_slot, hh] = jnp.broadcast_to(cmax[hh], (8, tile))
        return (m_new[0], acc[0], m_new[1], acc[1])

    def pair(i, carry):
        carry = stage(0, 2 * i, 1, jnp.where(i == 0, diag0 + 1, 2 * i - 1), carry)
        return stage(1, 2 * i + 1, 0, 2 * i, carry)

    state = stage(1, diag0 + 1, 0, diag0, init + init, diag_offset=chunk)
    state = lax.fori_loop(0, tile_id, pair, state)
    state = consume(1, jnp.where(tile_id == 0, diag0 + 1, diag0 - 1), state)
    oT = jnp.concatenate([acc[:HEAD_DIM] / acc[HEAD_DIM:HEAD_DIM + 1] for acc in (state[1], state[3])],
                         axis=0)
    o_ref[...] = oT.T.astype(BF16)


def _prompt_attn(qT, k_aug, v_tiles, kmean_hm):
    seq = qT.shape[1]
    nc, _, chunk = v_tiles.shape
    tile = Q_TILE
    nblk = kmean_hm.shape[1]
    return pl.pallas_call(
        _attn_kernel,
        grid=(N_HEADS // 2, seq // tile),
        in_specs=[
            pl.BlockSpec((2 * HEAD_DIM, tile), lambda p, t: (p, t)),
            pl.BlockSpec((2, seq, LANES), lambda p, t: (p, 0, 0)),
            pl.BlockSpec((nc, 2 * V_ROWS, chunk), lambda p, t: (0, p, 0)),
            pl.BlockSpec((2, nblk, HEAD_DIM), lambda p, t: (p, 0, 0)),
        ],
        out_specs=pl.BlockSpec((tile, 2 * HEAD_DIM), lambda p, t: (t, p)),
        out_shape=jax.ShapeDtypeStruct((seq, ATTN_WIDTH), BF16),
        scratch_shapes=[pltpu.VMEM((2, 2, chunk, tile), F32), pltpu.VMEM((2, 2, 8, tile), F32)],
        compiler_params=pltpu.CompilerParams(dimension_semantics=("arbitrary", "arbitrary"),
                                             vmem_limit_bytes=VMEM_LIMIT),
        name="prompt_attn",
    )(qT, k_aug, v_tiles, kmean_hm)


def _mlp_kernel(x_ref, attn_ref, pool_ref, wo_ref, g2_ref, wup_ref, wdown_ref, gf_ref, y_ref,
                x1_ref, hn_ref, acc_ref):
    j = pl.program_id(1)

    @pl.when(j == 0)
    def _():
        x1 = (x_ref[...] + _dot(attn_ref[...], wo_ref[0:ATTN_WIDTH, :])
              + _dot(pool_ref[...], wo_ref[ATTN_WIDTH:, :]))
        x1_ref[...] = x1
        hn_ref[...] = _rmsnorm(x1, g2_ref[...]).astype(BF16)
        acc_ref[...] = jnp.zeros(acc_ref.shape, F32)

    hid = jnp.maximum(_dot(hn_ref[...], wup_ref[...]), 0.0)
    acc_ref[...] += _dot((hid * hid).astype(BF16), wdown_ref[...])

    @pl.when(j == pl.num_programs(1) - 1)
    def _():
        y_ref[...] = _rmsnorm(x1_ref[...] + acc_ref[...], gf_ref[...])


def _out_mlp(x, attn, pool, wo, g2, wup, wdown, gf, tl, tf):
    rows = x.shape[0]
    const2 = lambda i, j: (0, 0)
    return pl.pallas_call(
        _mlp_kernel,
        grid=(rows // tl, D_FF // tf),
        in_specs=[
            pl.BlockSpec((tl, D_MODEL), lambda i, j: (i, 0)),
            pl.BlockSpec((tl, ATTN_WIDTH), lambda i, j: (i, 0)),
            pl.BlockSpec((tl, POOL_WIDTH), lambda i, j: (i, 0)),
            pl.BlockSpec((D_MODEL, D_MODEL), const2),
            pl.BlockSpec((1, D_MODEL), const2),
            pl.BlockSpec((D_MODEL, tf), lambda i, j: (0, j)),
            pl.BlockSpec((tf, D_MODEL), lambda i, j: (j, 0)),
            pl.BlockSpec((1, D_MODEL), const2),
        ],
        out_specs=pl.BlockSpec((tl, D_MODEL), lambda i, j: (i, 0)),
        out_shape=jax.ShapeDtypeStruct((rows, D_MODEL), F32),
        scratch_shapes=[pltpu.VMEM((tl, D_MODEL), F32), pltpu.VMEM((tl, D_MODEL), BF16),
                        pltpu.VMEM((tl, D_MODEL), F32)],
        compiler_params=pltpu.CompilerParams(dimension_semantics=("arbitrary", "arbitrary"),
                                             vmem_limit_bytes=VMEM_LIMIT),
        name="out_mlp",
    )(x, attn, pool, wo, g2, wup, wdown, gf)


def _sproj_kernel(x_ref, g_ref, win_ref, c_ref, s1_ref, s2_ref, state_ref, wpool_ref, pscale_ref,
                  q_ref, k_ref, v_ref, u_ref, pool_ref, ext_ref):
    rows = x_ref.shape[0]
    nseq, dec = state_ref.shape[0], rows // state_ref.shape[0]
    hb = _rmsnorm(x_ref[...], g_ref[...]).astype(BF16)
    z = _dot(hb, win_ref[...])
    for g in range(2 * ATTN_WIDTH // LANES):
        xg = z[:, LANES * g:LANES * (g + 1)]
        r = (xg * c_ref[...] + pltpu.roll(xg, ROT_HALF, 1) * s2_ref[...]
             + pltpu.roll(xg, LANES - ROT_HALF, 1) * s1_ref[...])
        if LANES * g < ATTN_WIDTH:
            q_ref[:, LANES * g:LANES * (g + 1)] = r * ATTN_SCALE
        else:
            k_ref[:, LANES * g - ATTN_WIDTH:LANES * (g + 1) - ATTN_WIDTH] = r
    v_ref[...] = z[:, 2 * ATTN_WIDTH:3 * ATTN_WIDTH]
    u = z[:, 3 * ATTN_WIDTH:]
    u_ref[...] = u

    ext_ref[:, 0:HALO, :] = state_ref[...]
    ext_ref[:, HALO:HALO + dec, :] = u.reshape(nseq, dec, POOL_WIDTH)
    for g, w in enumerate(POOL_WINDOWS):
        lanes = slice(POOL_CH * g, POOL_CH * (g + 1))
        ug = ext_ref[:, HALO:HALO + dec, lanes]
        acc = ug
        for j in range(1, w):
            acc = acc + ext_ref[:, HALO - j:HALO - j + dec, lanes]
        d = (acc / float(w) - ug).reshape(rows, POOL_CH)
        y = _dot(d.astype(BF16), wpool_ref[g]) * pscale_ref[:, lanes]
        pool_ref[:, lanes] = y.astype(BF16)


def _sample_proj(x, g1, win, c_tab, s1_tab, s2_tab, state_pad, wpool, pscale):
    rows = x.shape[0]
    nseq = state_pad.shape[0]
    shapes = [jax.ShapeDtypeStruct((rows, ATTN_WIDTH), F32)] * 3 + [
        jax.ShapeDtypeStruct((rows, POOL_WIDTH), F32), jax.ShapeDtypeStruct((rows, POOL_WIDTH), BF16)]
    return pl.pallas_call(
        _sproj_kernel,
        out_shape=shapes,
        scratch_shapes=[pltpu.VMEM((nseq, HALO + rows // nseq, POOL_WIDTH), F32)],
        compiler_params=pltpu.CompilerParams(vmem_limit_bytes=VMEM_LIMIT),
        name="sample_proj",
    )(x, g1, win, c_tab, s1_tab, s2_tab, state_pad, wpool, pscale)


GROUP = 16
K_BUF = 4 * GROUP
V_BUF = 4 * GROUP
K_PRIORITY = 0
V_PRIORITY = 1


def _dec_kernel(pt_ref, q_ref, kn_ref, vn_ref, ck_hbm, cv_hbm, o_ref,
                kbuf, vbuf, ksem, vsem, s_scr, p_scr, bias_scr):
    b = pl.program_id(0)
    n_pages = pt_ref.shape[1]
    page = kbuf.shape[-1]
    ppb = MOBA_BLOCK // page
    nblk = n_pages // ppb
    n_groups = n_pages // GROUP
    dec = q_ref.shape[1]
    rows = N_HEADS * dec

    def slot_of(g, e, n_buf):
        return (g % (n_buf // GROUP)) * GROUP + e

    def k_copy(j, slot, seq=b):
        return pltpu.make_async_copy(ck_hbm.at[pt_ref[seq, j]], kbuf.at[slot], ksem.at[slot])

    def v_copy(j, slot):
        return pltpu.make_async_copy(cv_hbm.at[pt_ref[b, j]], vbuf.at[slot], vsem.at[slot])

    @pl.when(b == 0)
    def _():
        for s in range(K_BUF):
            k_copy(s, s).start(priority=K_PRIORITY)

    for s in range(V_BUF):
        v_copy(s, s).start(priority=V_PRIORITY)

    r_idx = lax.broadcasted_iota(jnp.int32, (rows, ATTN_WIDTH), 0)
    c_idx = lax.broadcasted_iota(jnp.int32, (rows, ATTN_WIDTH), 1)
    head_mask = (c_idx // HEAD_DIM) == (r_idx // dec)
    q_bd = jnp.where(head_mask, jnp.concatenate([q_ref[0]] * N_HEADS, axis=0), 0.0)
    q_hi, q_lo = _split_bf16(q_bd)
    q_hilo = jnp.concatenate([q_hi, q_lo], axis=0)
    lane = lax.broadcasted_iota(jnp.int32, (rows, LANES), 1)

    def refill(copy, g, n_buf, priority):
        @pl.when((g + n_buf // GROUP) * GROUP < n_pages)
        def _():
            for e in range(GROUP):
                copy(g * GROUP + e + n_buf, slot_of(g, e, n_buf)).start(priority=priority)

    def block_of(buf, g, e, n_buf):
        return jnp.concatenate(
            [buf[slot_of(g, e + i, n_buf)].reshape(ATTN_WIDTH, page).astype(BF16) for i in range(ppb)],
            axis=1)

    def k_group(g, carry):
        for e in range(GROUP):
            k_copy(g * GROUP + e, slot_of(g, e, K_BUF)).wait()
        for e in range(0, GROUP, ppb):
            kt = block_of(kbuf, g, e, K_BUF)
            s2 = _dot(q_hilo, kt)
            s_scr[(g * GROUP + e) // ppb] = s2[:rows] + s2[rows:]
        refill(k_copy, g, K_BUF, K_PRIORITY)
        return carry

    lax.fori_loop(0, n_groups, k_group, 0)

    @pl.when(b + 1 < pl.num_programs(0))
    def _():
        for s in range(K_BUF):
            k_copy(s, s, b + 1).start(priority=K_PRIORITY)

    def gate_group(g, gate):
        for e in range(GROUP):
            n = g * GROUP + e
            gate = jnp.where(lane == n, jnp.sum(s_scr[n], axis=1, keepdims=True), gate)
        return gate

    gate = lax.fori_loop(0, nblk // GROUP, gate_group, jnp.zeros((rows, LANES), F32)) * (1.0 / MOBA_BLOCK)
    sel = _top_k_mask(gate, lane < nblk, lane, 1)
    bias = jnp.where(sel, 0.0, NEG)

    def bias_group(g, carry):
        for e in range(GROUP):
            n = g * GROUP + e
            col = jnp.sum(jnp.where(lane == n, bias, 0.0), axis=1, keepdims=True)
            bias_scr[n] = jnp.broadcast_to(col, (rows, MOBA_BLOCK))
        return carry

    lax.fori_loop(0, nblk // GROUP, bias_group, 0)

    lane_b = lax.broadcasted_iota(jnp.int32, (rows, MOBA_BLOCK), 1)
    q_pos = lax.broadcasted_iota(jnp.int32, (rows, MOBA_BLOCK), 0) % dec
    s_own = jnp.where(lane_b <= q_pos, _dot_nt(q_hi, kn_ref[0]), NEG)

    def max_group(g, mv):
        for e in range(GROUP):
            n = g * GROUP + e
            mv = jnp.maximum(mv, s_scr[n] + bias_scr[n])
        return mv

    m = jnp.max(lax.fori_loop(0, nblk // GROUP, max_group, s_own), axis=1, keepdims=True)
    p_own = jnp.exp(s_own - m)

    def exp_group(g, lv):
        for e in range(GROUP):
            n = g * GROUP + e
            p = jnp.exp(s_scr[n] + (bias_scr[n] - m))
            p_scr[n] = p.astype(BF16)
            lv = lv + p
        return lv

    l = jnp.sum(lax.fori_loop(0, nblk // GROUP, exp_group, p_own), axis=1, keepdims=True)

    def v_group(g, acc):
        for e in range(GROUP):
            v_copy(g * GROUP + e, slot_of(g, e, V_BUF)).wait()
        for e in range(0, GROUP, ppb):
            acc = acc + _dot_nt(p_scr[(g * GROUP + e) // ppb], block_of(vbuf, g, e, V_BUF))
        refill(v_copy, g, V_BUF, V_PRIORITY)
        return acc

    acc = lax.fori_loop(0, n_groups, v_group, _dot(p_own.astype(BF16), vn_ref[0]))
    o = jnp.where(head_mask, acc / l, 0.0)
    out = o[0:dec]
    for h in range(1, N_HEADS):
        out = out + o[h * dec:(h + 1) * dec]
    o_ref[0] = out


def _sample_attn(page_table, q_s, kn_pad, vn_pad, cache_kT, cache_vT):
    nseq, n_pages = page_table.shape
    dec = q_s.shape[1]
    page = cache_kT.shape[-1]
    nblk = n_pages * page // MOBA_BLOCK
    grid_spec = pltpu.PrefetchScalarGridSpec(
        num_scalar_prefetch=1,
        grid=(nseq,),
        in_specs=[
            pl.BlockSpec((1, dec, ATTN_WIDTH), lambda b, pt: (b, 0, 0)),
            pl.BlockSpec((1, MOBA_BLOCK, ATTN_WIDTH), lambda b, pt: (b, 0, 0)),
            pl.BlockSpec((1, MOBA_BLOCK, ATTN_WIDTH), lambda b, pt: (b, 0, 0)),
            pl.BlockSpec(memory_space=pl.ANY),
            pl.BlockSpec(memory_space=pl.ANY),
        ],
        out_specs=pl.BlockSpec((1, dec, ATTN_WIDTH), lambda b, pt: (b, 0, 0)),
        scratch_shapes=[
            pltpu.VMEM((K_BUF, N_HEADS, HEAD_DIM, page), F32),
            pltpu.VMEM((V_BUF, N_HEADS, HEAD_DIM, page), F32),
            pltpu.SemaphoreType.DMA((K_BUF,)),
            pltpu.SemaphoreType.DMA((V_BUF,)),
            pltpu.VMEM((nblk, N_HEADS * dec, MOBA_BLOCK), F32),
            pltpu.VMEM((nblk, N_HEADS * dec, MOBA_BLOCK), BF16),
            pltpu.VMEM((nblk, N_HEADS * dec, MOBA_BLOCK), F32),
        ],
    )
    return pl.pallas_call(
        _dec_kernel,
        grid_spec=grid_spec,
        out_shape=jax.ShapeDtypeStruct((nseq, dec, ATTN_WIDTH), F32),
        compiler_params=pltpu.CompilerParams(dimension_semantics=("arbitrary",),
                                             vmem_limit_bytes=VMEM_LIMIT),
        name="sample_attn",
    )(page_table, q_s, kn_pad, vn_pad, cache_kT, cache_vT)


def _rope_angles(pos):
    inv = jnp.power(ROPE_THETA, -jnp.arange(ROT_HALF, dtype=F32) * (2.0 / ROT_DIM))
    ang = pos.astype(F32)[:, None] * inv[None, :]
    return jnp.cos(ang), jnp.sin(ang)


def _rope_tables(pos):
    cos, sin = _rope_angles(pos)
    lane = jnp.arange(LANES) % HEAD_DIM
    f = lane % ROT_HALF
    c_tab = jnp.where(lane[None, :] < ROT_DIM, cos[:, f], 1.0)
    s1_tab = jnp.where(lane[None, :] < ROT_HALF, -sin[:, f], 0.0)
    s2_tab = jnp.where((lane[None, :] >= ROT_HALF) & (lane[None, :] < ROT_DIM), sin[:, f], 0.0)
    return c_tab, s1_tab, s2_tab


def _layer(l, xp, xs, cache_k, cache_v, state_pool, page_table, norm1_g, w_in, w_pool, pool_scale,
           w_o, norm2_g, w_up, w_down, final_g):
    seq = xp.shape[0]
    nseq, dec, _ = xs.shape
    past_len = page_table.shape[1] * cache_k.shape[2]

    g1 = norm1_g[l][None, :]
    g2 = norm2_g[l][None, :]
    win = w_in[l].astype(BF16)
    wqkvT = win[:, :3 * ATTN_WIDTH].T
    wu = win[:, 3 * ATTN_WIDTH:]
    wpool = w_pool[l].astype(BF16)
    pscale = pool_scale[l][None, :]
    wo = w_o[l].astype(BF16)
    wup = w_up[l].astype(BF16)
    wdown = w_down[l].astype(BF16)
    gf = final_g[None, :]

    cos, sin = _rope_angles(jnp.arange(seq, dtype=jnp.int32))
    qT, kT, vT, v_tiles, k_aug, kmean, pool_p, u_tail = _prompt_proj(
        xp, g1, wqkvT, wu, cos.T, sin.T, wpool, pscale)
    nb = seq // MOBA_BLOCK
    kmean_hm = kmean[:, 0, :].reshape(nb, N_HEADS, HEAD_DIM).transpose(1, 0, 2)
    attn_p = _prompt_attn(qT, k_aug, v_tiles, kmean_hm)
    yp = _out_mlp(xp, attn_p, pool_p, wo, g2, wup, wdown, gf, tl=MLP_ROW_TILE, tf=MLP_FF_TILE)
    k_prompt = kT.reshape(N_HEADS, HEAD_DIM, seq).transpose(2, 0, 1)[None]
    v_prompt = vT.reshape(N_HEADS, HEAD_DIM, seq).transpose(2, 0, 1)[None]
    pool_prompt = u_tail[HALO - POOL_STATE:][None]

    rows = nseq * dec
    pos_s = past_len + jnp.arange(dec, dtype=jnp.int32)
    c_s, s1_s, s2_s = _rope_tables(jnp.tile(pos_s, nseq))
    state = state_pool[l].astype(F32)
    state_pad = jnp.pad(state, ((0, 0), (HALO - POOL_STATE, 0), (0, 0)))
    q_s, k_s, v_s, u_s, pool_s = _sample_proj(
        xs.reshape(rows, D_MODEL), g1, win, c_s, s1_s, s2_s, state_pad, wpool, pscale)
    pad = ((0, 0), (0, MOBA_BLOCK - dec), (0, 0))
    kn_pad = jnp.pad(k_s.reshape(nseq, dec, ATTN_WIDTH), pad).astype(BF16)
    vn_pad = jnp.pad(v_s.reshape(nseq, dec, ATTN_WIDTH), pad).astype(BF16)
    cache_kT = jnp.transpose(cache_k[l], (0, 2, 3, 1))
    cache_vT = jnp.transpose(cache_v[l], (0, 2, 3, 1))
    attn_s = _sample_attn(page_table, q_s.reshape(nseq, dec, ATTN_WIDTH), kn_pad, vn_pad,
                          cache_kT, cache_vT)
    ys = _out_mlp(xs.reshape(rows, D_MODEL), attn_s.reshape(rows, ATTN_WIDTH).astype(BF16), pool_s,
                  wo, g2, wup, wdown, gf, tl=rows, tf=MLP_FF_TILE)
    k_sample = k_s.reshape(nseq, dec, N_HEADS, HEAD_DIM)
    v_sample = v_s.reshape(nseq, dec, N_HEADS, HEAD_DIM)
    u_ext = jnp.concatenate([state, u_s.reshape(nseq, dec, POOL_WIDTH)], axis=1)
    pool_sample = u_ext[:, u_ext.shape[1] - POOL_STATE:]
    return (yp, ys.reshape(nseq, dec, D_MODEL), k_prompt, v_prompt, pool_prompt,
            k_sample, v_sample, pool_sample)


def kernel(x_prompt, x_sample, cache_k, cache_v, state_pool, page_table, norm1_g, w_in, w_pool,
           pool_scale, w_o, norm2_g, w_up, w_down, final_g):
    depth = w_in.shape[0]
    assert depth == 1 and x_prompt.shape[0] == 1, "single layer, single prompt sequence"
    (yp, ys, k_p, v_p, pool_p, k_s, v_s, pool_s) = _layer(
        0, x_prompt[0], x_sample, cache_k, cache_v, state_pool, page_table, norm1_g, w_in, w_pool,
        pool_scale, w_o, norm2_g, w_up, w_down, final_g)
    return (yp[None], ys, k_p[None], v_p[None], pool_p[None], k_s[None], v_s[None], pool_s[None])
```
